```python
import math
import jax, jax.numpy as jnp
from jax import lax
import numpy as np

D_MODEL = 1024
BATCH = 8
SEQ = 2048
DEPTH = 2

ATTN_PATTERNS = ((128, 1), (512, 4), (2048, 16))
N_ATTN_GROUPS = len(ATTN_PATTERNS)
ATTN_HEADS = 8
ATTN_HEAD_DIM = 64
ATTN_WIDTH = ATTN_HEADS * ATTN_HEAD_DIM
RET_HEADS = 4
RET_HEAD_DIM = D_MODEL // RET_HEADS
RET_WIDTH = RET_HEADS * RET_HEAD_DIM
RET_CHUNK = 128
N_EXPERTS = 16
EXPERT_FF = 2 * D_MODEL
CAPACITY_FACTOR = 2
EPS = 1e-6
ATTN_IN = N_ATTN_GROUPS * 3 * ATTN_WIDTH
RET_IN = 4 * RET_WIDTH
GATE_IN = 2 * D_MODEL
N_IN = ATTN_IN + RET_IN + GATE_IN
SPLITS = (ATTN_IN,
          ATTN_IN + RET_WIDTH,
          ATTN_IN + 2 * RET_WIDTH,
          ATTN_IN + 3 * RET_WIDTH,
          ATTN_IN + 4 * RET_WIDTH)

kernel_name = "hybrid_dilated_attn_retention_ec_moe"


def rms_norm(x, gain):
    xf = x.astype(jnp.float32)
    y = xf * lax.rsqrt(jnp.mean(xf * xf, axis=-1, keepdims=True) + EPS)
    return (y * gain.astype(jnp.float32)).astype(x.dtype)


def alibi_slopes(n):
    return 2.0 ** (-8.0 * (jnp.arange(n, dtype=jnp.float32) + 1.0) / n)


def dilated_band_attention(q, k, v, dilation, half, slopes):
    B, S, H, Dh = q.shape
    L = S // dilation
    nb = -(-L // half)
    Lp = nb * half

    def to_residue(t):
        return t.reshape(B, L, dilation, H, Dh).transpose(0, 3, 2, 1, 4)

    qr = jnp.pad(to_residue(q), ((0, 0), (0, 0), (0, 0), (0, Lp - L), (0, 0)))
    qr = qr.reshape(B, H, dilation, nb, half, Dh)

    def windows(t):
        t = jnp.pad(to_residue(t), ((0, 0), (0, 0), (0, 0), (half, Lp - L + half), (0, 0)))
        t = t.reshape(B, H, dilation, nb + 2, half, Dh)
        return jnp.concatenate([t[:, :, :, :-2], t[:, :, :, 1:-1], t[:, :, :, 2:]], axis=4)

    kw, vw = windows(k), windows(v)
    qi = jnp.arange(half)
    ki = jnp.arange(3 * half)
    rel = ki[None, :] - half - qi[:, None]
    kpos = jnp.arange(nb)[:, None] * half - half + ki[None, :]
    valid = (jnp.abs(rel) <= half)[None] & ((kpos >= 0) & (kpos < L))[:, None, :]
    dist = (dilation * jnp.abs(rel)).astype(jnp.float32)

    s = jnp.einsum('bhrnqd,bhrnkd->bhrnqk', qr, kw).astype(jnp.float32) * (Dh ** -0.5)
    s = s - slopes[None, :, None, None, None, None] * dist
    s = jnp.where(valid, s, -jnp.inf)
    m = jnp.max(s, axis=-1, keepdims=True)
    p = jnp.exp(s - m)
    den = jnp.sum(p, axis=-1)
    o = jnp.einsum('bhrnqk,bhrnkd->bhrnqd', p, vw.astype(jnp.float32)) / den[..., None]
    lse = m[..., 0] + jnp.log(den)

    o = o.reshape(B, H, dilation, Lp, Dh)[:, :, :, :L]
    o = o.transpose(0, 3, 2, 1, 4).reshape(B, S, H, Dh)
    lse = lse.reshape(B, H, dilation, Lp)[:, :, :, :L]
    lse = lse.transpose(0, 3, 2, 1).reshape(B, S, H)
    return o, lse


def retention_chunkwise(q, k, v, log_gamma, strict):
    B, H, S, Dk = q.shape
    Dv = v.shape[-1]
    N = S // RET_CHUNK
    qc = q.reshape(B, H, N, RET_CHUNK, Dk).astype(jnp.float32) * (Dk ** -0.5)
    kc = k.reshape(B, H, N, RET_CHUNK, Dk).astype(jnp.float32)
    vc = v.reshape(B, H, N, RET_CHUNK, Dv).astype(jnp.float32)
    pos = jnp.arange(RET_CHUNK, dtype=jnp.float32)
    rel = pos[:, None] - pos[None, :]
    mask = (rel > 0) if strict else (rel >= 0)
    lg = log_gamma[:, None, None]
    decay = jnp.where(mask, jnp.exp(lg * jnp.where(mask, rel, 0.0)), 0.0)
    scores = jnp.einsum('bhnqd,bhnkd->bhnqk', qc, kc) * decay[None, :, None]
    y = jnp.einsum('bhnqk,bhnkv->bhnqv', scores, vc)
    zeta = jnp.exp(log_gamma[:, None] * (RET_CHUNK - 1 - pos)[None, :])
    u = jnp.einsum('bhnkd,bhnkv->bhndv', kc * zeta[None, :, None, :, None], vc)
    chunk_decay = jnp.exp(log_gamma * RET_CHUNK)[None, :, None, None]

    def step(state, u_i):
        return state * chunk_decay + u_i, state

    _, s_prev = lax.scan(step, jnp.zeros((B, H, Dk, Dv), jnp.float32), jnp.moveaxis(u, 2, 0))
    s_prev = jnp.moveaxis(s_prev, 0, 2)
    xi = jnp.exp(log_gamma[:, None] * (pos + 1.0)[None, :])
    y = y + jnp.einsum('bhnqd,bhndv->bhnqv', qc, s_prev) * xi[None, :, None, :, None]
    return y.reshape(B, H, S, Dv)


def expert_choice_ffn(h, w_router, w_gate, w_up, w_down):
    B, S, D = h.shape
    cap = CAPACITY_FACTOR * S // N_EXPERTS
    logits = jnp.einsum('bsd,de->bse', h, w_router).astype(jnp.float32)
    aff = jax.nn.softmax(logits, axis=-1)
    g, idx = lax.top_k(jnp.swapaxes(aff, 1, 2), cap)
    xin = jax.vmap(lambda hb, ib: hb[ib])(h, idx)
    a = jnp.einsum('becd,edf->becf', xin, w_gate)
    b = jnp.einsum('becd,edf->becf', xin, w_up)
    y = jnp.einsum('becf,efd->becd', jax.nn.silu(a) * b, w_down)
    y = y * g[..., None].astype(y.dtype)
    return jax.vmap(lambda yb, ib: jnp.zeros((S, D), yb.dtype).at[ib.reshape(-1)].add(yb.reshape(-1, D)))(y, idx)


def setup_inputs(seed: int = 0) -> dict:
    key = jax.random.key(seed)
    ks = jax.random.split(key, 16)
    f32 = jnp.float32
    D = D_MODEL
    x = jax.random.normal(ks[0], (BATCH, SEQ, D), f32)
    w_in = jax.random.normal(ks[1], (DEPTH, D, N_IN), f32) * D ** -0.5
    w_attn_out = jax.random.normal(ks[2], (DEPTH, ATTN_WIDTH, D), f32) * ATTN_WIDTH ** -0.5
    w_ret_out = jax.random.normal(ks[3], (DEPTH, RET_WIDTH, D), f32) * RET_WIDTH ** -0.5
    w_out = jax.random.normal(ks[4], (DEPTH, D, D), f32) * D ** -0.5
    base = jnp.log(2.0 ** (5.0 + jnp.arange(RET_HEADS, dtype=f32)) - 1.0)
    ret_decay_logit = base[None, None, :] + 0.1 * jax.random.normal(ks[5], (DEPTH, 2, RET_HEADS), f32)
    norm_mix = 1.0 + 0.05 * jax.random.normal(ks[6], (DEPTH, D), f32)
    norm_ffn = 1.0 + 0.05 * jax.random.normal(ks[7], (DEPTH, D), f32)
    w_router = jax.random.normal(ks[8], (DEPTH, D, N_EXPERTS), f32) * D ** -0.5
    w_gate = jax.random.normal(ks[9], (DEPTH, N_EXPERTS, D, EXPERT_FF), f32) * D ** -0.5
    w_up = jax.random.normal(ks[10], (DEPTH, N_EXPERTS, D, EXPERT_FF), f32) * D ** -0.5
    w_down = jax.random.normal(ks[11], (DEPTH, N_EXPERTS, EXPERT_FF, D), f32) * EXPERT_FF ** -0.5
    norm_final = 1.0 + 0.05 * jax.random.normal(ks[12], (D,), f32)
    return {"x": x, "w_in": w_in, "w_attn_out": w_attn_out, "w_ret_out": w_ret_out,
            "w_out": w_out, "ret_decay_logit": ret_decay_logit, "norm_mix": norm_mix,
            "norm_ffn": norm_ffn, "w_router": w_router, "w_gate": w_gate, "w_up": w_up,
            "w_down": w_down, "norm_final": norm_final}


def reference(x, w_in, w_attn_out, w_ret_out, w_out, ret_decay_logit, norm_mix, norm_ffn,
              w_router, w_gate, w_up, w_down, norm_final):
    B, S, D = x.shape
    slopes = alibi_slopes(ATTN_HEADS)
    for layer in range(DEPTH):
        h = rms_norm(x, norm_mix[layer])
        proj = jnp.einsum('bsd,dn->bsn', h, w_in[layer])
        a_qkv, r_q, r_k, r_v, r_g, gates = jnp.split(proj, SPLITS, axis=-1)

        a_qkv = a_qkv.reshape(B, S, N_ATTN_GROUPS, 3, ATTN_HEADS, ATTN_HEAD_DIM)
        outs, lses = [], []
        for gi, (window, dilation) in enumerate(ATTN_PATTERNS):
            half = window // (2 * dilation)
            o, lse = dilated_band_attention(a_qkv[:, :, gi, 0], a_qkv[:, :, gi, 1], a_qkv[:, :, gi, 2],
                                            dilation, half, slopes)
            outs.append(o)
            lses.append(lse)
        alpha = jax.nn.softmax(jnp.stack(lses, axis=0), axis=0)
        attn = sum(alpha[gi][..., None] * outs[gi] for gi in range(N_ATTN_GROUPS))
        attn = attn.reshape(B, S, ATTN_WIDTH).astype(x.dtype)

        def heads(t):
            return t.reshape(B, S, RET_HEADS, RET_HEAD_DIM).transpose(0, 2, 1, 3)
        q, k, v = heads(r_q), heads(r_k), heads(r_v)
        log_gamma = jax.nn.log_sigmoid(ret_decay_logit[layer].astype(jnp.float32))
        fwd = retention_chunkwise(q, k, v, log_gamma[0], strict=False)
        bwd = jnp.flip(retention_chunkwise(jnp.flip(q, 2), jnp.flip(k, 2), jnp.flip(v, 2),
                                           log_gamma[1], strict=True), 2)
        r = fwd + bwd
        r = r * lax.rsqrt(jnp.mean(r * r, axis=-1, keepdims=True) + EPS)
        r = r.transpose(0, 2, 1, 3).reshape(B, S, RET_WIDTH)
        ret = (jax.nn.silu(r_g.astype(jnp.float32)) * r).astype(x.dtype)

        g_a, g_r = jnp.split(gates, 2, axis=-1)
        merged = (jax.nn.sigmoid(g_a) * jnp.einsum('bsw,wd->bsd', attn, w_attn_out[layer])
                  + jax.nn.sigmoid(g_r) * jnp.einsum('bsw,wd->bsd', ret, w_ret_out[layer]))
        x = x + jnp.einsum('bsd,de->bse', merged, w_out[layer])

        h2 = rms_norm(x, norm_ffn[layer])
        x = x + expert_choice_ffn(h2, w_router[layer], w_gate[layer], w_up[layer], w_down[layer])
    return rms_norm(x, norm_final)
```

```python
import functools

import jax
import jax.numpy as jnp
from jax import lax
from jax.experimental import pallas as pl
from jax.experimental.pallas import tpu as pltpu

EPS = 1e-6
ATTN_PATTERNS = ((128, 1), (512, 4), (2048, 16))
ATTN_HEADS = 8
ATTN_HEAD_DIM = 64
ATTN_WIDTH = ATTN_HEADS * ATTN_HEAD_DIM
RET_HEADS = 4
RET_CHUNK = 128
N_EXPERTS = 16
CAPACITY_FACTOR = 2

LANES = 128
MASKED = -1e30
VMEM_LIMIT_BYTES = 56 * 1024 * 1024

F32 = jnp.float32
BF16 = jnp.bfloat16


def _params(*semantics):
    return pltpu.CompilerParams(dimension_semantics=semantics, vmem_limit_bytes=VMEM_LIMIT_BYTES)


def _rms(x, gain):
    return x * lax.rsqrt(jnp.mean(x * x, axis=-1, keepdims=True) + EPS) * gain


def _norm_proj_kernel(x_ref, gain_ref, w_ref, o_ref, h_ref):
    @pl.when(pl.program_id(1) == 0)
    def _():
        h_ref[...] = _rms(x_ref[...], gain_ref[...]).astype(BF16)

    o_ref[...] = jnp.dot(h_ref[...], w_ref[...], preferred_element_type=F32).astype(o_ref.dtype)


def _norm_proj(x2d, gain, w, col0, ncols, out_dtype, tm, tn):
    T, D = x2d.shape
    assert T % tm == 0 and ncols % tn == 0 and col0 % tn == 0
    return pl.pallas_call(
        _norm_proj_kernel,
        grid=(T // tm, ncols // tn),
        in_specs=[pl.BlockSpec((tm, D), lambda i, j: (i, 0)),
                  pl.BlockSpec((1, D), lambda i, j: (0, 0)),
                  pl.BlockSpec((D, tn), lambda i, j: (0, j + col0 // tn))],
        out_specs=pl.BlockSpec((tm, tn), lambda i, j: (i, j)),
        out_shape=jax.ShapeDtypeStruct((T, ncols), out_dtype),
        scratch_shapes=[pltpu.VMEM((tm, D), BF16)],
        compiler_params=_params("parallel", "arbitrary"),
        name="norm_proj",
    )(x2d, gain.reshape(1, D), w)


def _attn_kernel(qkv_ref, o_ref, lse_ref, bias_ref, *, L, dilation, half, tq):
    tk = tq + 2 * half
    W = ATTN_WIDTH

    @pl.when(pl.program_id(0) == 0)
    def _():
        r = lax.broadcasted_iota(jnp.int32, (tq, tk), 0)
        c = lax.broadcasted_iota(jnp.int32, (tq, tk), 1)
        rel = jnp.abs(c - half - r)
        dist = (dilation * rel).astype(F32)
        for h in range(ATTN_HEADS):
            slope = 2.0 ** (-8.0 * (h + 1) / ATTN_HEADS)
            bias_ref[h] = jnp.where(rel <= half, -slope * dist, MASKED)

    lane = lax.broadcasted_iota(jnp.int32, (1, LANES), 1)
    first_head = lane < ATTN_HEAD_DIM

    def block(i, carry):
        q0 = pl.multiple_of(i * tq, tq)
        kpos = lax.broadcasted_iota(jnp.int32, (1, tk), 1) + (i * tq - half)
        edge = jnp.where((kpos >= 0) & (kpos < L), 0.0, MASKED)
        for hp in range(ATTN_HEADS // 2):
            cols = slice(hp * LANES, (hp + 1) * LANES)
            qp = qkv_ref[0, pl.ds(q0 + half, tq), cols]
            kp = qkv_ref[0, pl.ds(q0, tk), W + hp * LANES:W + (hp + 1) * LANES]
            vp = qkv_ref[0, pl.ds(q0, tk), 2 * W + hp * LANES:2 * W + (hp + 1) * LANES]
            outs, lses = [], []
            for hh in range(2):
                keep = first_head if hh == 0 else jnp.logical_not(first_head)
                qh = jnp.where(keep, qp, jnp.zeros_like(qp))
                s = lax.dot_general(qh, kp, (((1,), (1,)), ((), ())), preferred_element_type=F32)
                s = s * (ATTN_HEAD_DIM ** -0.5) + bias_ref[2 * hp + hh] + edge
                m = jnp.max(s, axis=-1, keepdims=True)
                p = jnp.exp(s - m)
                den = jnp.sum(p, axis=-1, keepdims=True)
                pv = jnp.dot(p.astype(BF16), vp, preferred_element_type=F32)
                outs.append(pv / den)
                lses.append(jnp.broadcast_to(m + jnp.log(den), (tq, LANES)))
            o_ref[0, pl.ds(q0, tq), cols] = jnp.where(first_head, outs[0], outs[1])
            lse_ref[0, pl.ds(q0, tq), cols] = jnp.where(first_head, lses[0], lses[1])
        return carry

    lax.fori_loop(0, L // tq, block, 0)


def _dilated_attention(qkv, dilation, half):
    B, S, W3 = qkv.shape
    W = W3 // 3
    L = S // dilation
    tq = 128
    t = qkv.reshape(B, L, dilation, W3).transpose(0, 2, 1, 3).reshape(B * dilation, L, W3)
    t = jnp.pad(t, ((0, 0), (half, half), (0, 0)))
    o, lse = pl.pallas_call(
        functools.partial(_attn_kernel, L=L, dilation=dilation, half=half, tq=tq),
        grid=(B * dilation,),
        in_specs=[pl.BlockSpec((1, L + 2 * half, W3), lambda n: (n, 0, 0))],
        out_specs=[pl.BlockSpec((1, L, W), lambda n: (n, 0, 0)),
                   pl.BlockSpec((1, L, W), lambda n: (n, 0, 0))],
        out_shape=[jax.ShapeDtypeStruct((B * dilation, L, W), F32),
                   jax.ShapeDtypeStruct((B * dilation, L, W), F32)],
        scratch_shapes=[pltpu.VMEM((ATTN_HEADS, tq, tq + 2 * half), F32)],
        compiler_params=_params("arbitrary"),
        name=f"dilated_attn_d{dilation}",
    )(t)

    def back(a):
        return a.reshape(B, dilation, L, W).transpose(0, 2, 1, 3).reshape(B, S, W)

    return back(o), back(lse)


def _log_sigmoid(x):
    return jnp.minimum(x, 0.0) - jnp.log1p(jnp.exp(-jnp.abs(x)))


def _retention_kernel(logit_ref, q_ref, k_ref, v_ref, gate_ref, o_ref, y_ref, state_ref):
    C = RET_CHUNK
    S, Dk = q_ref.shape[1], q_ref.shape[2]
    N = S // C
    h = pl.program_id(1)
    lf = _log_sigmoid(jnp.full((C, 1), logit_ref[0, h], F32))
    lb = _log_sigmoid(jnp.full((C, 1), logit_ref[1, h], F32))
    rel = (lax.broadcasted_iota(jnp.int32, (C, C), 0) - lax.broadcasted_iota(jnp.int32, (C, C), 1)).astype(F32)
    decay = jnp.exp(jnp.where(rel >= 0, lf * rel, -lb * rel))
    pos = lax.broadcasted_iota(jnp.int32, (C, 1), 0).astype(F32)
    xi_f = jnp.exp(lf * (pos + 1.0))
    zeta_f = jnp.exp(lf * (C - 1.0 - pos))
    xi_b = jnp.exp(lb * (C - pos))
    zeta_b = jnp.exp(lb * pos)
    chunk_f = jnp.exp(_log_sigmoid(jnp.full((Dk, 1), logit_ref[0, h], F32)) * C)
    chunk_b = jnp.exp(_log_sigmoid(jnp.full((Dk, 1), logit_ref[1, h], F32)) * C)
    scale = Dk ** -0.5

    def load(i):
        r0 = pl.multiple_of(i * C, C)
        return r0, q_ref[0, pl.ds(r0, C), :] * scale, k_ref[0, pl.ds(r0, C), :], v_ref[0, pl.ds(r0, C), :]

    def cross(qc, kc, vc, xi, zeta, chunk_decay):
        st = state_ref[...]
        y = jnp.dot(qc, st.astype(BF16), preferred_element_type=F32) * xi
        kz = (kc.astype(F32) * zeta).astype(BF16)
        u = lax.dot_general(kz, vc, (((0,), (0,)), ((), ())), preferred_element_type=F32)
        state_ref[...] = st * chunk_decay + u
        return y

    state_ref[...] = jnp.zeros_like(state_ref)

    def fwd(i, carry):
        r0, qc, kc, vc = load(i)
        s = lax.dot_general(qc, kc, (((1,), (1,)), ((), ())), preferred_element_type=F32) * decay
        y = jnp.dot(s.astype(BF16), vc, preferred_element_type=F32)
        y_ref[pl.ds(r0, C), :] = y + cross(qc, kc, vc, xi_f, zeta_f, chunk_f)
        return carry

    lax.fori_loop(0, N, fwd, 0)
    state_ref[...] = jnp.zeros_like(state_ref)

    def bwd(n, carry):
        r0, qc, kc, vc = load(N - 1 - n)
        r = y_ref[pl.ds(r0, C), :] + cross(qc, kc, vc, xi_b, zeta_b, chunk_b)
        r = r * lax.rsqrt(jnp.mean(r * r, axis=-1, keepdims=True) + EPS)
        g = gate_ref[0, pl.ds(r0, C), :]
        o_ref[0, pl.ds(r0, C), :] = (g / (1.0 + jnp.exp(-g)) * r).astype(o_ref.dtype)
        return carry

    lax.fori_loop(0, N, bwd, 0)


def _retention(proj, gates, decay_logit, q_col0):
    B, S, _ = proj.shape
    Dk = gates.shape[2] // 3 // RET_HEADS
    qb = q_col0 // Dk
    nb = RET_HEADS
    return pl.pallas_call(
        _retention_kernel,
        grid=(B, RET_HEADS),
        in_specs=[pl.BlockSpec(memory_space=pltpu.SMEM),
                  pl.BlockSpec((1, S, Dk), lambda b, h: (b, 0, qb + h)),
                  pl.BlockSpec((1, S, Dk), lambda b, h: (b, 0, qb + nb + h)),
                  pl.BlockSpec((1, S, Dk), lambda b, h: (b, 0, qb + 2 * nb + h)),
                  pl.BlockSpec((1, S, Dk), lambda b, h: (b, 0, h))],
        out_specs=pl.BlockSpec((1, S, Dk), lambda b, h: (b, 0, h)),
        out_shape=jax.ShapeDtypeStruct((B, S, RET_HEADS * Dk), BF16),
        scratch_shapes=[pltpu.VMEM((S, Dk), F32), pltpu.VMEM((Dk, Dk), F32)],
        compiler_params=_params("parallel", "parallel"),
        name="retention",
    )(decay_logit, proj, proj, proj, gates)


def _merge_kernel(o1_ref, o2_ref, o3_ref, l1_ref, l2_ref, l3_ref, ret_ref, ga_ref, gr_ref, x_ref,
                  wa_ref, wr_ref, wo_ref, gain_ref, wrt_ref, xo_ref, h_ref, logit_ref):
    l1, l2, l3 = l1_ref[...], l2_ref[...], l3_ref[...]
    m = jnp.maximum(jnp.maximum(l1, l2), l3)
    e1, e2, e3 = jnp.exp(l1 - m), jnp.exp(l2 - m), jnp.exp(l3 - m)
    attn = (e1 * o1_ref[...] + e2 * o2_ref[...] + e3 * o3_ref[...]) / (e1 + e2 + e3)
    pa = jnp.dot(attn.astype(BF16), wa_ref[...], preferred_element_type=F32)
    pr = jnp.dot(ret_ref[...], wr_ref[...], preferred_element_type=F32)
    ga, gr = ga_ref[...], gr_ref[...]
    merged = pa / (1.0 + jnp.exp(-ga)) + pr / (1.0 + jnp.exp(-gr))
    x = x_ref[...] + jnp.dot(merged.astype(BF16), wo_ref[...], preferred_element_type=F32)
    xo_ref[...] = x
    h = _rms(x, gain_ref[...])
    h_ref[...] = h.astype(BF16)
    logit_ref[...] = jnp.dot(h, wrt_ref[...], preferred_element_type=F32, precision=lax.Precision.HIGHEST)


def _merge(outs, lses, ret, gates, x2d, wa, wr, wo, gain, w_router_pad, tm):
    T, D = x2d.shape
    W = ATTN_WIDTH
    tok = lambda width, col=0: pl.BlockSpec((tm, width), lambda i: (i, col))
    full = lambda a: pl.BlockSpec(a.shape, lambda i: (0, 0))
    return pl.pallas_call(
        _merge_kernel,
        grid=(T // tm,),
        in_specs=[tok(W)] * 6 + [tok(D), tok(D, 1), tok(D, 2), tok(D),
                                 full(wa), full(wr), full(wo), pl.BlockSpec((1, D), lambda i: (0, 0)),
                                 full(w_router_pad)],
        out_specs=[tok(D), tok(D), tok(LANES)],
        out_shape=[jax.ShapeDtypeStruct((T, D), F32), jax.ShapeDtypeStruct((T, D), BF16),
                   jax.ShapeDtypeStruct((T, LANES), F32)],
        compiler_params=_params("parallel"),
        name="merge_out_proj",
    )(*outs, *lses, ret, gates, gates, x2d, wa, wr, wo, gain.reshape(1, D), w_router_pad)


def _route_kernel(logit_ref, pos_ref, aff_ref, before_ref, *, n_experts, cap):
    S = logit_ref.shape[1]
    rows = 256

    @pl.when(pl.program_id(0) == 0)
    def _():
        def fill(i, carry):
            r0 = pl.multiple_of(i * rows, rows)
            r = lax.broadcasted_iota(jnp.int32, (rows, S), 0) + r0
            c = lax.broadcasted_iota(jnp.int32, (rows, S), 1)
            before_ref[pl.ds(r0, rows), :] = jnp.where(r < c, 1.0, 0.0).astype(BF16)
            return carry
        lax.fori_loop(0, S // rows, fill, 0)

    lane = lax.broadcasted_iota(jnp.int32, (1, LANES), 1)
    lg = jnp.where(lane < n_experts, logit_ref[0], MASKED)
    ex = jnp.exp(lg - jnp.max(lg, axis=-1, keepdims=True))
    aff = (ex / jnp.sum(ex, axis=-1, keepdims=True)).T[:n_experts]
    bits = lax.bitcast_convert_type(aff, jnp.int32)

    def search(it, thr):
        cand = thr | jnp.left_shift(jnp.int32(1), 30 - it)
        cnt = jnp.sum((bits >= cand).astype(jnp.int32), axis=-1, keepdims=True)
        return jnp.where(cnt >= cap, cand, thr)

    thr = lax.fori_loop(0, 31, search, jnp.zeros((n_experts, 1), jnp.int32))
    above = bits > thr
    tied = bits == thr
    need = (cap - jnp.sum(above.astype(jnp.int32), axis=-1, keepdims=True)).astype(F32)
    count_before = lambda mask: jnp.dot(jnp.where(mask, 1.0, 0.0).astype(BF16), before_ref[...],
                                        preferred_element_type=F32)
    chosen = above | (tied & (count_before(tied) < need))
    pos_ref[0] = jnp.where(chosen, count_before(chosen), -1.0)
    aff_ref[0] = aff


def _route(logits, cap):
    B, S, _ = logits.shape
    E = N_EXPERTS
    return pl.pallas_call(
        functools.partial(_route_kernel, n_experts=E, cap=cap),
        grid=(B,),
        in_specs=[pl.BlockSpec((1, S, LANES), lambda b: (b, 0, 0))],
        out_specs=[pl.BlockSpec((1, E, S), lambda b: (b, 0, 0)), pl.BlockSpec((1, E, S), lambda b: (b, 0, 0))],
        out_shape=[jax.ShapeDtypeStruct((B, E, S), F32), jax.ShapeDtypeStruct((B, E, S), F32)],
        scratch_shapes=[pltpu.VMEM((S, S), BF16)],
        compiler_params=_params("arbitrary"),
        name="route",
    )(logits)


def _slot_mask(pos_ref, cap):
    pos = pos_ref[0, 0].astype(jnp.int32)
    slot = lax.broadcasted_iota(jnp.int32, (cap, pos.shape[1]), 0)
    return slot == pos


def _gather_kernel(h_ref, pos_ref, aff_ref, x_ref, g_ref, *, cap):
    mask = _slot_mask(pos_ref, cap)
    onehot = jnp.where(mask, 1.0, 0.0).astype(BF16)
    x_ref[0, 0] = jnp.dot(onehot, h_ref[0], preferred_element_type=F32).astype(x_ref.dtype)
    g = jnp.sum(jnp.where(mask, aff_ref[0, 0], 0.0), axis=-1, keepdims=True)
    g_ref[0, 0] = jnp.broadcast_to(g, (cap, LANES))


def _gather(h, pos, aff, cap):
    B, S, D = h.shape
    E = pos.shape[1]
    row = pl.BlockSpec((1, 1, 1, S), lambda b, e: (b, e, 0, 0))
    return pl.pallas_call(
        functools.partial(_gather_kernel, cap=cap),
        grid=(B, E),
        in_specs=[pl.BlockSpec((1, S, D), lambda b, e: (b, 0, 0)), row, row],
        out_specs=[pl.BlockSpec((1, 1, cap, D), lambda b, e: (e, b, 0, 0)),
                   pl.BlockSpec((1, 1, cap, LANES), lambda b, e: (e, b, 0, 0))],
        out_shape=[jax.ShapeDtypeStruct((E, B, cap, D), BF16), jax.ShapeDtypeStruct((E, B, cap, LANES), F32)],
        compiler_params=_params("parallel", "arbitrary"),
        name="expert_gather",
    )(h, pos.reshape(B, E, 1, S), aff.reshape(B, E, 1, S))


def _expert_kernel(x_ref, g_ref, wg_ref, wu_ref, wd_ref, y_ref, acc_ref):
    f = pl.program_id(1)
    x = x_ref[0]
    a = jnp.dot(x, wg_ref[0].astype(BF16), preferred_element_type=F32)
    u = jnp.dot(x, wu_ref[0].astype(BF16), preferred_element_type=F32)
    act = (a / (1.0 + jnp.exp(-a)) * u).astype(BF16)
    part = jnp.dot(act, wd_ref[0].astype(BF16), preferred_element_type=F32)

    @pl.when(f == 0)
    def _():
        acc_ref[...] = part

    @pl.when(f > 0)
    def _():
        acc_ref[...] += part

    @pl.when(f == pl.num_programs(1) - 1)
    def _():
        y_ref[0] = (acc_ref[...] * g_ref[0][:, :1]).astype(y_ref.dtype)


def _experts(xin, g, w_gate, w_up, w_down, tf):
    E, M, D = xin.shape
    FF = w_gate.shape[2]
    return pl.pallas_call(
        _expert_kernel,
        grid=(E, FF // tf),
        in_specs=[pl.BlockSpec((1, M, D), lambda e, f: (e, 0, 0)),
                  pl.BlockSpec((1, M, LANES), lambda e, f: (e, 0, 0)),
                  pl.BlockSpec((1, D, tf), lambda e, f: (e, 0, f)),
                  pl.BlockSpec((1, D, tf), lambda e, f: (e, 0, f)),
                  pl.BlockSpec((1, tf, D), lambda e, f: (e, f, 0))],
        out_specs=pl.BlockSpec((1, M, D), lambda e, f: (e, 0, 0)),
        out_shape=jax.ShapeDtypeStruct((E, M, D), BF16),
        scratch_shapes=[pltpu.VMEM((M, D), F32)],
        compiler_params=_params("parallel", "arbitrary"),
        name="expert_swiglu",
    )(xin, g, w_gate, w_up, w_down)


def _scatter_kernel(x_ref, pos_ref, y_ref, o_ref, *, cap):
    onehot = jnp.where(_slot_mask(pos_ref, cap), 1.0, 0.0).astype(BF16)
    add = lax.dot_general(onehot, y_ref[0, 0], (((0,), (0,)), ((), ())), preferred_element_type=F32)

    @pl.when(pl.program_id(2) == 0)
    def _():
        o_ref[0] = x_ref[0] + add

    @pl.when(pl.program_id(2) > 0)
    def _():
        o_ref[0] += add


def _scatter(x, pos, y, cap, tn):
    B, S, D = x.shape
    E = pos.shape[1]
    return pl.pallas_call(
        functools.partial(_scatter_kernel, cap=cap),
        grid=(B, D // tn, E),
        in_specs=[pl.BlockSpec((1, S, tn), lambda b, n, e: (b, 0, n)),
                  pl.BlockSpec((1, 1, 1, S), lambda b, n, e: (b, e, 0, 0)),
                  pl.BlockSpec((1, 1, cap, tn), lambda b, n, e: (e, b, 0, n))],
        out_specs=pl.BlockSpec((1, S, tn), lambda b, n, e: (b, 0, n)),
        out_shape=jax.ShapeDtypeStruct((B, S, D), F32),
        compiler_params=_params("parallel", "parallel", "arbitrary"),
        name="expert_scatter",
    )(x, pos.reshape(B, E, 1, S), y)


def _final_norm_kernel(x_ref, gain_ref, o_ref):
    o_ref[...] = _rms(x_ref[...], gain_ref[...])


def _final_norm(x2d, gain, tm):
    T, D = x2d.shape
    return pl.pallas_call(
        _final_norm_kernel,
        grid=(T // tm,),
        in_specs=[pl.BlockSpec((tm, D), lambda i: (i, 0)), pl.BlockSpec((1, D), lambda i: (0, 0))],
        out_specs=pl.BlockSpec((tm, D), lambda i: (i, 0)),
        out_shape=jax.ShapeDtypeStruct((T, D), F32),
        compiler_params=_params("parallel"),
        name="final_norm",
    )(x2d, gain.reshape(1, D))


def kernel(x, w_in, w_attn_out, w_ret_out, w_out, ret_decay_logit, norm_mix, norm_ffn, w_router, w_gate, w_up,
           w_down, norm_final):
    B, S, D = x.shape
    T = B * S
    depth = w_in.shape[0]
    W = ATTN_WIDTH
    attn_in = len(ATTN_PATTERNS) * 3 * W
    main_cols = attn_in + 3 * D
    gate_cols = 3 * D
    cap = CAPACITY_FACTOR * S // N_EXPERTS
    x2d = x.reshape(T, D)
    for layer in range(depth):
        w_in_b = w_in[layer].astype(BF16)
        proj = _norm_proj(x2d, norm_mix[layer], w_in_b, 0, main_cols, BF16, tm=2048, tn=1536)
        gates = _norm_proj(x2d, norm_mix[layer], w_in_b, main_cols, gate_cols, F32, tm=1024, tn=1536)
        proj3 = proj.reshape(B, S, main_cols)

        outs, lses = [], []
        for gi, (window, dilation) in enumerate(ATTN_PATTERNS):
            o, lse = _dilated_attention(proj3[:, :, gi * 3 * W:(gi + 1) * 3 * W], dilation, window // (2 * dilation))
            outs.append(o.reshape(T, W))
            lses.append(lse.reshape(T, W))

        ret = _retention(proj3, gates.reshape(B, S, gate_cols), ret_decay_logit[layer], attn_in)

        w_router_pad = jnp.pad(w_router[layer], ((0, 0), (0, LANES - N_EXPERTS)))
        x2d, h2, logits = _merge(outs, lses, ret.reshape(T, D), gates, x2d, w_attn_out[layer].astype(BF16),
                                 w_ret_out[layer].astype(BF16), w_out[layer].astype(BF16), norm_ffn[layer],
                                 w_router_pad, tm=512)

        pos, aff = _route(logits.reshape(B, S, LANES), cap)
        xin, g = _gather(h2.reshape(B, S, D), pos, aff, cap)
        y = _experts(xin.reshape(N_EXPERTS, B * cap, D), g.reshape(N_EXPERTS, B * cap, LANES),
                     w_gate[layer], w_up[layer], w_down[layer], tf=256)
        x2d = _scatter(x2d.reshape(B, S, D), pos, y.reshape(N_EXPERTS, B, cap, D), cap, tn=512).reshape(T, D)
    return _final_norm(x2d, norm_final, tm=1024).reshape(B, S, D)
```

```python
import functools

import jax
import jax.numpy as jnp
from jax import lax
from jax.experimental import pallas as pl
from jax.experimental.pallas import tpu as pltpu

EPS = 1e-6
ATTN_PATTERNS = ((128, 1), (512, 4), (2048, 16))
ATTN_HEADS = 8
ATTN_HEAD_DIM = 64
ATTN_WIDTH = ATTN_HEADS * ATTN_HEAD_DIM
RET_HEADS = 4
RET_CHUNK = 128
N_EXPERTS = 16
CAPACITY_FACTOR = 2

LANES = 128
MASKED = -1e30
VMEM_LIMIT_BYTES = 56 * 1024 * 1024

F32 = jnp.float32
BF16 = jnp.bfloat16


def _params(*semantics):
    return pltpu.CompilerParams(dimension_semantics=semantics, vmem_limit_bytes=VMEM_LIMIT_BYTES)


def _rms(x, gain):
    return x * lax.rsqrt(jnp.mean(x * x, axis=-1, keepdims=True) + EPS) * gain


def _norm_proj_kernel(x_ref, gain_ref, w_ref, o_ref, h_ref):
    @pl.when(pl.program_id(1) == 0)
    def _():
        h_ref[...] = _rms(x_ref[...], gain_ref[...]).astype(BF16)

    o_ref[...] = jnp.dot(h_ref[...], w_ref[...], preferred_element_type=F32).astype(o_ref.dtype)


def _norm_proj(x2d, gain, w, col0, ncols, out_dtype, tm, tn):
    T, D = x2d.shape
    assert T % tm == 0 and ncols % tn == 0 and col0 % tn == 0
    return pl.pallas_call(
        _norm_proj_kernel,
        grid=(T // tm, ncols // tn),
        in_specs=[pl.BlockSpec((tm, D), lambda i, j: (i, 0)),
                  pl.BlockSpec((1, D), lambda i, j: (0, 0)),
                  pl.BlockSpec((D, tn), lambda i, j: (0, j + col0 // tn))],
        out_specs=pl.BlockSpec((tm, tn), lambda i, j: (i, j)),
        out_shape=jax.ShapeDtypeStruct((T, ncols), out_dtype),
        scratch_shapes=[pltpu.VMEM((tm, D), BF16)],
        compiler_params=_params("parallel", "arbitrary"),
        name="norm_proj",
    )(x2d, gain.reshape(1, D), w)


ATTN_TQ = 128


def _attn_kernel(*refs, S, patterns):
    n_groups = len(patterns)
    in_refs, o_ref = refs[:3 * n_groups], refs[3 * n_groups]
    stage_ref, qs_ref, ks_ref, vs_ref, og_ref, lg_ref, bias_ref = refs[3 * n_groups + 1:]
    tq = ATTN_TQ
    hp = pl.program_id(0)
    lane = lax.broadcasted_iota(jnp.int32, (1, LANES), 1)
    first_head = lane < ATTN_HEAD_DIM

    for gi, (window, d) in enumerate(patterns):
        half = window // (2 * d)
        tk = tq + 2 * half
        L = S // d
        nblk = L // tq
        seg = L + 2 * half
        q_in, k_in, v_in = in_refs[3 * gi:3 * gi + 3]

        @pl.when(pl.program_id(1) == 0)
        def _(gi=gi, d=d, half=half, tk=tk):
            r = lax.broadcasted_iota(jnp.int32, (tq, tk), 0)
            c = lax.broadcasted_iota(jnp.int32, (tq, tk), 1)
            rel = jnp.abs(c - half - r)
            dist = (d * rel).astype(F32)
            for hh in range(2):
                pow2 = jnp.full((tq, tk), jnp.left_shift(jnp.int32(2), 2 * hp + hh), jnp.int32).astype(F32)
                band = jnp.where(rel <= half, -(dist / pow2), MASKED)
                for kind in range(4):
                    tile_bias = band
                    if kind & 1:
                        tile_bias = jnp.where(c < half, MASKED, tile_bias)
                    if kind & 2:
                        tile_bias = jnp.where(c >= tq + half, MASKED, tile_bias)
                    bias_ref[gi, kind, hh] = tile_bias

        zeros = jnp.zeros((half, LANES), BF16)
        if d == 1:
            ks_ref[pl.ds(0, half), :] = zeros
            ks_ref[pl.ds(half, S), :] = k_in[0]
            ks_ref[pl.ds(half + S, half), :] = zeros
            vs_ref[pl.ds(0, half), :] = zeros
            vs_ref[pl.ds(half, S), :] = v_in[0]
            vs_ref[pl.ds(half + S, half), :] = zeros
        else:
            stage_ref[0] = q_in[0].astype(F32)
            stage_ref[1] = k_in[0].astype(F32)
            stage_ref[2] = v_in[0].astype(F32)

            def regroup(r, carry, d=d, L=L, seg=seg, half=half):
                rows = pl.ds(r, L, stride=d)
                k0 = pl.multiple_of(r * seg, half)
                qs_ref[pl.ds(pl.multiple_of(r * L, tq), L), :] = stage_ref[0, rows, :].astype(BF16)
                ks_ref[pl.ds(k0, half), :] = zeros
                ks_ref[pl.ds(k0 + half, L), :] = stage_ref[1, rows, :].astype(BF16)
                ks_ref[pl.ds(k0 + half + L, half), :] = zeros
                vs_ref[pl.ds(k0, half), :] = zeros
                vs_ref[pl.ds(k0 + half, L), :] = stage_ref[2, rows, :].astype(BF16)
                vs_ref[pl.ds(k0 + half + L, half), :] = zeros
                return carry

            lax.fori_loop(0, d, regroup, 0)

        def tile(u, carry, gi=gi, d=d, nblk=nblk, tk=tk, q_in=q_in):
            r = u // nblk
            i = u % nblk
            q0 = pl.multiple_of(u * tq, tq)
            k0 = pl.multiple_of((u + r) * tq, tq)
            kind = (i == 0).astype(jnp.int32) + 2 * (i == nblk - 1).astype(jnp.int32)
            qp = q_in[0, pl.ds(q0, tq), :] if d == 1 else qs_ref[pl.ds(q0, tq), :]
            qp = qp * (ATTN_HEAD_DIM ** -0.5)
            kp = ks_ref[pl.ds(k0, tk), :]
            vp = vs_ref[pl.ds(k0, tk), :]
            outs, lses = [], []
            for hh in range(2):
                keep = first_head if hh == 0 else jnp.logical_not(first_head)
                qh = jnp.where(keep, qp, jnp.zeros_like(qp))
                s = lax.dot_general(qh, kp, (((1,), (1,)), ((), ())), preferred_element_type=F32)
                s = s + bias_ref[gi, kind, hh]
                m = jnp.max(s, axis=-1, keepdims=True)
                p = jnp.exp(s - m)
                den = jnp.sum(p, axis=-1, keepdims=True)
                pv = jnp.dot(p.astype(BF16), vp, preferred_element_type=F32)
                outs.append(pv * (1.0 / den))
                lses.append(jnp.broadcast_to(m + jnp.log(den), (tq, LANES)))
            rows = pl.ds(q0, tq) if d == 1 else pl.ds(i * (tq * d) + r, tq, stride=d)
            og_ref[gi, rows, :] = jnp.where(first_head, outs[0], outs[1])
            lg_ref[gi, rows, :] = jnp.where(first_head, lses[0], lses[1])
            return carry

        lax.fori_loop(0, S // tq, tile, 0)

    def merge(j, carry):
        rows = pl.ds(pl.multiple_of(j * tq, tq), tq)
        lse = [lg_ref[g, rows, :] for g in range(n_groups)]
        top = functools.reduce(jnp.maximum, lse)
        w = [jnp.exp(l - top) for l in lse]
        num = sum(w[g] * og_ref[g, rows, :] for g in range(n_groups))
        o_ref[0, rows, :] = (num / sum(w)).astype(o_ref.dtype)
        return carry

    lax.fori_loop(0, S // tq, merge, 0)


def _dilated_attention(proj):
    B, S, _ = proj.shape
    W = ATTN_WIDTH
    n_groups = len(ATTN_PATTERNS)
    halves = {w // (2 * d) for w, d in ATTN_PATTERNS}
    assert len(halves) == 1 and all(S % (d * ATTN_TQ) == 0 for _, d in ATTN_PATTERNS)
    half = halves.pop()
    tk = ATTN_TQ + 2 * half
    pairs = W // LANES
    col = lambda c0: pl.BlockSpec((1, S, LANES), lambda hp, b: (b, 0, c0 // LANES + hp))
    return pl.pallas_call(
        functools.partial(_attn_kernel, S=S, patterns=ATTN_PATTERNS),
        grid=(pairs, B),
        in_specs=[col((3 * g + t) * W) for g in range(n_groups) for t in range(3)],
        out_specs=pl.BlockSpec((1, S, LANES), lambda hp, b: (b, 0, hp)),
        out_shape=jax.ShapeDtypeStruct((B, S, W), BF16),
        scratch_shapes=[pltpu.VMEM((3, S, LANES), F32),
                        pltpu.VMEM((S, LANES), BF16),
                        pltpu.VMEM((2 * S, LANES), BF16),
                        pltpu.VMEM((2 * S, LANES), BF16),
                        pltpu.VMEM((n_groups, S, LANES), F32),
                        pltpu.VMEM((n_groups, S, LANES), F32),
                        pltpu.VMEM((n_groups, 4, 2, ATTN_TQ, tk), F32)],
        compiler_params=_params("arbitrary", "arbitrary"),
        name="dilated_attention",
    )(*([proj] * (3 * n_groups)))


def _log_sigmoid(x):
    return jnp.minimum(x, 0.0) - jnp.log1p(jnp.exp(-jnp.abs(x)))


def _retention_kernel(logit_ref, q_ref, k_ref, v_ref, gate_ref, o_ref, y_ref, state_ref):
    C = RET_CHUNK
    S, Dk = q_ref.shape[1], q_ref.shape[2]
    N = S // C
    h = pl.program_id(1)
    lf = _log_sigmoid(jnp.full((C, 1), logit_ref[0, h], F32))
    lb = _log_sigmoid(jnp.full((C, 1), logit_ref[1, h], F32))
    rel = (lax.broadcasted_iota(jnp.int32, (C, C), 0) - lax.broadcasted_iota(jnp.int32, (C, C), 1)).astype(F32)
    decay = jnp.exp(jnp.where(rel >= 0, lf * rel, -lb * rel))
    pos = lax.broadcasted_iota(jnp.int32, (C, 1), 0).astype(F32)
    xi_f = jnp.exp(lf * (pos + 1.0))
    zeta_f = jnp.exp(lf * (C - 1.0 - pos))
    xi_b = jnp.exp(lb * (C - pos))
    zeta_b = jnp.exp(lb * pos)
    chunk_f = jnp.exp(_log_sigmoid(jnp.full((Dk, 1), logit_ref[0, h], F32)) * C)
    chunk_b = jnp.exp(_log_sigmoid(jnp.full((Dk, 1), logit_ref[1, h], F32)) * C)
    scale = Dk ** -0.5

    def load(i):
        r0 = pl.multiple_of(i * C, C)
        return r0, q_ref[0, pl.ds(r0, C), :] * scale, k_ref[0, pl.ds(r0, C), :], v_ref[0, pl.ds(r0, C), :]

    def cross(qc, kc, vc, xi, zeta, chunk_decay):
        st = state_ref[...]
        y = jnp.dot(qc, st.astype(BF16), preferred_element_type=F32) * xi
        kz = (kc.astype(F32) * zeta).astype(BF16)
        u = lax.dot_general(kz, vc, (((0,), (0,)), ((), ())), preferred_element_type=F32)
        state_ref[...] = st * chunk_decay + u
        return y

    state_ref[...] = jnp.zeros_like(state_ref)

    def fwd(i, carry):
        r0, qc, kc, vc = load(i)
        s = lax.dot_general(qc, kc, (((1,), (1,)), ((), ())), preferred_element_type=F32) * decay
        y = jnp.dot(s.astype(BF16), vc, preferred_element_type=F32)
        y_ref[pl.ds(r0, C), :] = y + cross(qc, kc, vc, xi_f, zeta_f, chunk_f)
        return carry

    lax.fori_loop(0, N, fwd, 0)
    state_ref[...] = jnp.zeros_like(state_ref)

    def bwd(n, carry):
        r0, qc, kc, vc = load(N - 1 - n)
        r = y_ref[pl.ds(r0, C), :] + cross(qc, kc, vc, xi_b, zeta_b, chunk_b)
        r = r * lax.rsqrt(jnp.mean(r * r, axis=-1, keepdims=True) + EPS)
        g = gate_ref[0, pl.ds(r0, C), :]
        o_ref[0, pl.ds(r0, C), :] = (g / (1.0 + jnp.exp(-g)) * r).astype(o_ref.dtype)
        return carry

    lax.fori_loop(0, N, bwd, 0)


def _retention(proj, gates, decay_logit, q_col0):
    B, S, _ = proj.shape
    Dk = gates.shape[2] // 3 // RET_HEADS
    qb = q_col0 // Dk
    nb = RET_HEADS
    return pl.pallas_call(
        _retention_kernel,
        grid=(B, RET_HEADS),
        in_specs=[pl.BlockSpec(memory_space=pltpu.SMEM),
                  pl.BlockSpec((1, S, Dk), lambda b, h: (b, 0, qb + h)),
                  pl.BlockSpec((1, S, Dk), lambda b, h: (b, 0, qb + nb + h)),
                  pl.BlockSpec((1, S, Dk), lambda b, h: (b, 0, qb + 2 * nb + h)),
                  pl.BlockSpec((1, S, Dk), lambda b, h: (b, 0, h))],
        out_specs=pl.BlockSpec((1, S, Dk), lambda b, h: (b, 0, h)),
        out_shape=jax.ShapeDtypeStruct((B, S, RET_HEADS * Dk), BF16),
        scratch_shapes=[pltpu.VMEM((S, Dk), F32), pltpu.VMEM((Dk, Dk), F32)],
        compiler_params=_params("parallel", "parallel"),
        name="retention",
    )(decay_logit, proj, proj, proj, gates)


def _merge_kernel(attn_ref, ret_ref, ga_ref, gr_ref, x_ref, wa_ref, wr_ref, wo_ref, gain_ref, wrt_ref,
                  xo_ref, h_ref, logit_ref):
    pa = jnp.dot(attn_ref[...], wa_ref[...], preferred_element_type=F32)
    pr = jnp.dot(ret_ref[...], wr_ref[...], preferred_element_type=F32)
    ga, gr = ga_ref[...], gr_ref[...]
    merged = pa / (1.0 + jnp.exp(-ga)) + pr / (1.0 + jnp.exp(-gr))
    x = x_ref[...] + jnp.dot(merged.astype(BF16), wo_ref[...], preferred_element_type=F32)
    xo_ref[...] = x
    h = _rms(x, gain_ref[...])
    h_ref[...] = h.astype(BF16)
    logit_ref[...] = jnp.dot(h, wrt_ref[...], preferred_element_type=F32, precision=lax.Precision.HIGHEST)


def _merge(attn, ret, gates, x2d, wa, wr, wo, gain, w_router_pad, tm):
    T, D = x2d.shape
    tok = lambda width, col=0: pl.BlockSpec((tm, width), lambda i: (i, col))
    full = lambda a: pl.BlockSpec(a.shape, lambda i: (0, 0))
    return pl.pallas_call(
        _merge_kernel,
        grid=(T // tm,),
        in_specs=[tok(ATTN_WIDTH), tok(D), tok(D, 1), tok(D, 2), tok(D),
                  full(wa), full(wr), full(wo), pl.BlockSpec((1, D), lambda i: (0, 0)), full(w_router_pad)],
        out_specs=[tok(D), tok(D), tok(LANES)],
        out_shape=[jax.ShapeDtypeStruct((T, D), F32), jax.ShapeDtypeStruct((T, D), BF16),
                   jax.ShapeDtypeStruct((T, LANES), F32)],
        compiler_params=_params("parallel"),
        name="merge_out_proj",
    )(attn, ret, gates, gates, x2d, wa, wr, wo, gain.reshape(1, D), w_router_pad)


def _route_kernel(logit_ref, pos_ref, aff_ref, before_ref, *, n_experts, cap):
    S = logit_ref.shape[1]
    rows = 256

    @pl.when(pl.program_id(0) == 0)
    def _():
        def fill(i, carry):
            r0 = pl.multiple_of(i * rows, rows)
            r = lax.broadcasted_iota(jnp.int32, (rows, S), 0) + r0
            c = lax.broadcasted_iota(jnp.int32, (rows, S), 1)
            before_ref[pl.ds(r0, rows), :] = jnp.where(r < c, 1.0, 0.0).astype(BF16)
            return carry
        lax.fori_loop(0, S // rows, fill, 0)

    lane = lax.broadcasted_iota(jnp.int32, (1, LANES), 1)
    lg = jnp.where(lane < n_experts, logit_ref[0], MASKED)
    ex = jnp.exp(lg - jnp.max(lg, axis=-1, keepdims=True))
    aff = (ex / jnp.sum(ex, axis=-1, keepdims=True)).T[:n_experts]

    def search(it, thr_bits):
        cand = thr_bits | jnp.left_shift(jnp.int32(1), 30 - it)
        cnt = jnp.sum((aff >= lax.bitcast_convert_type(cand, F32)).astype(jnp.int32), axis=-1, keepdims=True)
        return jnp.where(cnt >= cap, cand, thr_bits)

    floor = lax.bitcast_convert_type(lax.fori_loop(0, 31, search, jnp.zeros((n_experts, 1), jnp.int32)), F32)
    thr = jnp.min(jnp.where(aff >= floor, aff, jnp.inf), axis=-1, keepdims=True)
    above = aff > thr
    tied = aff == thr
    need = (cap - jnp.sum(above.astype(jnp.int32), axis=-1, keepdims=True)).astype(F32)
    count_before = lambda mask: jnp.dot(jnp.where(mask, 1.0, 0.0).astype(BF16), before_ref[...],
                                        preferred_element_type=F32)
    chosen = above | (tied & (count_before(tied) < need))
    pos_ref[0] = jnp.where(chosen, count_before(chosen), -1.0)
    aff_ref[0] = aff


def _route(logits, cap):
    B, S, _ = logits.shape
    E = N_EXPERTS
    return pl.pallas_call(
        functools.partial(_route_kernel, n_experts=E, cap=cap),
        grid=(B,),
        in_specs=[pl.BlockSpec((1, S, LANES), lambda b: (b, 0, 0))],
        out_specs=[pl.BlockSpec((1, E, S), lambda b: (b, 0, 0)), pl.BlockSpec((1, E, S), lambda b: (b, 0, 0))],
        out_shape=[jax.ShapeDtypeStruct((B, E, S), F32), jax.ShapeDtypeStruct((B, E, S), F32)],
        scratch_shapes=[pltpu.VMEM((S, S), BF16)],
        compiler_params=_params("arbitrary"),
        name="route",
    )(logits)


def _slot_mask(pos_ref, cap):
    pos = pos_ref[0, 0].astype(jnp.int32)
    slot = lax.broadcasted_iota(jnp.int32, (cap, pos.shape[1]), 0)
    return slot == pos


def _gather_kernel(h_ref, pos_ref, aff_ref, x_ref, g_ref, *, cap):
    mask = _slot_mask(pos_ref, cap)
    onehot = jnp.where(mask, 1.0, 0.0).astype(BF16)
    x_ref[0, 0] = jnp.dot(onehot, h_ref[0], preferred_element_type=F32).astype(x_ref.dtype)
    g = jnp.sum(jnp.where(mask, aff_ref[0, 0], 0.0), axis=-1, keepdims=True)
    g_ref[0, 0] = jnp.broadcast_to(g, (cap, LANES))


def _gather(h, pos, aff, cap):
    B, S, D = h.shape
    E = pos.shape[1]
    row = pl.BlockSpec((1, 1, 1, S), lambda b, e: (b, e, 0, 0))
    return pl.pallas_call(
        functools.partial(_gather_kernel, cap=cap),
        grid=(B, E),
        in_specs=[pl.BlockSpec((1, S, D), lambda b, e: (b, 0, 0)), row, row],
        out_specs=[pl.BlockSpec((1, 1, cap, D), lambda b, e: (e, b, 0, 0)),
                   pl.BlockSpec((1, 1, cap, LANES), lambda b, e: (e, b, 0, 0))],
        out_shape=[jax.ShapeDtypeStruct((E, B, cap, D), BF16), jax.ShapeDtypeStruct((E, B, cap, LANES), F32)],
        compiler_params=_params("parallel", "arbitrary"),
        name="expert_gather",
    )(h, pos.reshape(B, E, 1, S), aff.reshape(B, E, 1, S))


def _expert_kernel(x_ref, g_ref, wg_ref, wu_ref, wd_ref, y_ref, acc_ref):
    f = pl.program_id(1)
    x = x_ref[0]
    a = jnp.dot(x, wg_ref[0].astype(BF16), preferred_element_type=F32)
    u = jnp.dot(x, wu_ref[0].astype(BF16), preferred_element_type=F32)
    act = (a / (1.0 + jnp.exp(-a)) * u).astype(BF16)
    part = jnp.dot(act, wd_ref[0].astype(BF16), preferred_element_type=F32)

    @pl.when(f == 0)
    def _():
        acc_ref[...] = part

    @pl.when(f > 0)
    def _():
        acc_ref[...] += part

    @pl.when(f == pl.num_programs(1) - 1)
    def _():
        y_ref[0] = (acc_ref[...] * g_ref[0][:, :1]).astype(y_ref.dtype)


def _experts(xin, g, w_gate, w_up, w_down, tf):
    E, M, D = xin.shape
    FF = w_gate.shape[2]
    return pl.pallas_call(
        _expert_kernel,
        grid=(E, FF // tf),
        in_specs=[pl.BlockSpec((1, M, D), lambda e, f: (e, 0, 0)),
                  pl.BlockSpec((1, M, LANES), lambda e, f: (e, 0, 0)),
                  pl.BlockSpec((1, D, tf), lambda e, f: (e, 0, f)),
                  pl.BlockSpec((1, D, tf), lambda e, f: (e, 0, f)),
                  pl.BlockSpec((1, tf, D), lambda e, f: (e, f, 0))],
        out_specs=pl.BlockSpec((1, M, D), lambda e, f: (e, 0, 0)),
        out_shape=jax.ShapeDtypeStruct((E, M, D), BF16),
        scratch_shapes=[pltpu.VMEM((M, D), F32)],
        compiler_params=_params("parallel", "arbitrary"),
        name="expert_swiglu",
    )(xin, g, w_gate, w_up, w_down)


def _scatter_kernel(x_ref, pos_ref, y_ref, o_ref, *, cap):
    onehot = jnp.where(_slot_mask(pos_ref, cap), 1.0, 0.0).astype(BF16)
    add = lax.dot_general(onehot, y_ref[0, 0], (((0,), (0,)), ((), ())), preferred_element_type=F32)

    @pl.when(pl.program_id(2) == 0)
    def _():
        o_ref[0] = x_ref[0] + add

    @pl.when(pl.program_id(2) > 0)
    def _():
        o_ref[0] += add


def _scatter(x, pos, y, cap, tn):
    B, S, D = x.shape
    E = pos.shape[1]
    return pl.pallas_call(
        functools.partial(_scatter_kernel, cap=cap),
        grid=(B, D // tn, E),
        in_specs=[pl.BlockSpec((1, S, tn), lambda b, n, e: (b, 0, n)),
                  pl.BlockSpec((1, 1, 1, S), lambda b, n, e: (b, e, 0, 0)),
                  pl.BlockSpec((1, 1, cap, tn), lambda b, n, e: (e, b, 0, n))],
        out_specs=pl.BlockSpec((1, S, tn), lambda b, n, e: (b, 0, n)),
        out_shape=jax.ShapeDtypeStruct((B, S, D), F32),
        compiler_params=_params("parallel", "parallel", "arbitrary"),
        name="expert_scatter",
    )(x, pos.reshape(B, E, 1, S), y)


def _final_norm_kernel(x_ref, gain_ref, o_ref):
    o_ref[...] = _rms(x_ref[...], gain_ref[...])


def _final_norm(x2d, gain, tm):
    T, D = x2d.shape
    return pl.pallas_call(
        _final_norm_kernel,
        grid=(T // tm,),
        in_specs=[pl.BlockSpec((tm, D), lambda i: (i, 0)), pl.BlockSpec((1, D), lambda i: (0, 0))],
        out_specs=pl.BlockSpec((tm, D), lambda i: (i, 0)),
        out_shape=jax.ShapeDtypeStruct((T, D), F32),
        compiler_params=_params("parallel"),
        name="final_norm",
    )(x2d, gain.reshape(1, D))


def kernel(x, w_in, w_attn_out, w_ret_out, w_out, ret_decay_logit, norm_mix, norm_ffn, w_router, w_gate, w_up,
           w_down, norm_final):
    B, S, D = x.shape
    T = B * S
    depth = w_in.shape[0]
    W = ATTN_WIDTH
    attn_in = len(ATTN_PATTERNS) * 3 * W
    main_cols = attn_in + 3 * D
    gate_cols = 3 * D
    cap = CAPACITY_FACTOR * S // N_EXPERTS
    x2d = x.reshape(T, D)
    for layer in range(depth):
        w_in_b = w_in[layer].astype(BF16)
        proj = _norm_proj(x2d, norm_mix[layer], w_in_b, 0, main_cols, BF16, tm=2048, tn=1536)
        gates = _norm_proj(x2d, norm_mix[layer], w_in_b, main_cols, gate_cols, F32, tm=1024, tn=1536)
        proj3 = proj.reshape(B, S, main_cols)

        attn = _dilated_attention(proj3)
        ret = _retention(proj3, gates.reshape(B, S, gate_cols), ret_decay_logit[layer], attn_in)

        w_router_pad = jnp.pad(w_router[layer], ((0, 0), (0, LANES - N_EXPERTS)))
        x2d, h2, logits = _merge(attn.reshape(T, W), ret.reshape(T, D), gates, x2d, w_attn_out[layer].astype(BF16),
                                 w_ret_out[layer].astype(BF16), w_out[layer].astype(BF16), norm_ffn[layer],
                                 w_router_pad, tm=512)

        pos, aff = _route(logits.reshape(B, S, LANES), cap)
        xin, g = _gather(h2.reshape(B, S, D), pos, aff, cap)
        y = _experts(xin.reshape(N_EXPERTS, B * cap, D), g.reshape(N_EXPERTS, B * cap, LANES),
                     w_gate[layer], w_up[layer], w_down[layer], tf=256)
        x2d = _scatter(x2d.reshape(B, S, D), pos, y.reshape(N_EXPERTS, B, cap, D), cap, tn=512).reshape(T, D)
    return _final_norm(x2d, norm_final, tm=1024).reshape(B, S, D)
```

```python
import functools

import jax
import jax.numpy as jnp
from jax import lax
from jax.experimental import pallas as pl
from jax.experimental.pallas import tpu as pltpu

EPS = 1e-6
ATTN_PATTERNS = ((128, 1), (512, 4), (2048, 16))
ATTN_HEADS = 8
ATTN_HEAD_DIM = 64
ATTN_WIDTH = ATTN_HEADS * ATTN_HEAD_DIM
RET_HEADS = 4
RET_CHUNK = 128
N_EXPERTS = 16
CAPACITY_FACTOR = 2

LANES = 128
MASKED = -1e30
VMEM_LIMIT_BYTES = 56 * 1024 * 1024

F32 = jnp.float32
BF16 = jnp.bfloat16


def _params(*semantics):
    return pltpu.CompilerParams(dimension_semantics=semantics, vmem_limit_bytes=VMEM_LIMIT_BYTES)


def _rms(x, gain):
    return x * lax.rsqrt(jnp.mean(x * x, axis=-1, keepdims=True) + EPS) * gain


def _norm_proj_kernel(x_ref, gain_ref, w_ref, o_ref, h_ref):
    @pl.when(pl.program_id(1) == 0)
    def _():
        h_ref[...] = _rms(x_ref[...], gain_ref[...]).astype(BF16)

    o_ref[...] = jnp.dot(h_ref[...], w_ref[...], preferred_element_type=F32).astype(o_ref.dtype)


def _norm_proj(x2d, gain, w, col0, ncols, out_dtype, tm, tn):
    T, D = x2d.shape
    assert T % tm == 0 and ncols % tn == 0 and col0 % tn == 0
    return pl.pallas_call(
        _norm_proj_kernel,
        grid=(T // tm, ncols // tn),
        in_specs=[pl.BlockSpec((tm, D), lambda i, j: (i, 0)),
                  pl.BlockSpec((1, D), lambda i, j: (0, 0)),
                  pl.BlockSpec((D, tn), lambda i, j: (0, j + col0 // tn))],
        out_specs=pl.BlockSpec((tm, tn), lambda i, j: (i, j)),
        out_shape=jax.ShapeDtypeStruct((T, ncols), out_dtype),
        scratch_shapes=[pltpu.VMEM((tm, D), BF16)],
        compiler_params=_params("parallel", "arbitrary"),
        name="norm_proj",
    )(x2d, gain.reshape(1, D), w)


ATTN_TQ = 128
ATTN_TILES_IN_FLIGHT = 4


def _attn_kernel(*refs, S, patterns):
    n_groups = len(patterns)
    in_refs, o_ref = refs[:3 * n_groups], refs[3 * n_groups]
    stage_ref, qs_ref, ks_ref, vs_ref, og_ref, lg_ref, bias_ref = refs[3 * n_groups + 1:]
    tq = ATTN_TQ
    hp = pl.program_id(0)
    lane = lax.broadcasted_iota(jnp.int32, (1, LANES), 1)
    first_head = lane < ATTN_HEAD_DIM

    for gi, (window, d) in enumerate(patterns):
        half = window // (2 * d)
        tk = tq + 2 * half
        L = S // d
        nblk = L // tq
        seg = L + 2 * half
        q_in, k_in, v_in = in_refs[3 * gi:3 * gi + 3]

        @pl.when(pl.program_id(1) == 0)
        def _(gi=gi, d=d, half=half, tk=tk):
            r = lax.broadcasted_iota(jnp.int32, (tq, tk), 0)
            c = lax.broadcasted_iota(jnp.int32, (tq, tk), 1)
            rel = jnp.abs(c - half - r)
            dist = (d * rel).astype(F32)
            for hh in range(2):
                pow2 = jnp.full((tq, tk), jnp.left_shift(jnp.int32(2), 2 * hp + hh), jnp.int32).astype(F32)
                band = jnp.where(rel <= half, -(dist / pow2), MASKED)
                for kind in range(4):
                    tile_bias = band
                    if kind & 1:
                        tile_bias = jnp.where(c < half, MASKED, tile_bias)
                    if kind & 2:
                        tile_bias = jnp.where(c >= tq + half, MASKED, tile_bias)
                    bias_ref[gi, kind, hh] = tile_bias

        zeros = jnp.zeros((half, LANES), BF16)
        if d == 1:
            ks_ref[pl.ds(0, half), :] = zeros
            ks_ref[pl.ds(half, S), :] = k_in[0]
            ks_ref[pl.ds(half + S, half), :] = zeros
            vs_ref[pl.ds(0, half), :] = zeros
            vs_ref[pl.ds(half, S), :] = v_in[0]
            vs_ref[pl.ds(half + S, half), :] = zeros
        else:
            stage_ref[0] = q_in[0].astype(F32)
            stage_ref[1] = k_in[0].astype(F32)
            stage_ref[2] = v_in[0].astype(F32)

            def regroup(r, carry, d=d, L=L, seg=seg, half=half):
                rows = pl.ds(r, L, stride=d)
                k0 = pl.multiple_of(r * seg, half)
                qs_ref[pl.ds(pl.multiple_of(r * L, tq), L), :] = stage_ref[0, rows, :].astype(BF16)
                ks_ref[pl.ds(k0, half), :] = zeros
                ks_ref[pl.ds(k0 + half, L), :] = stage_ref[1, rows, :].astype(BF16)
                ks_ref[pl.ds(k0 + half + L, half), :] = zeros
                vs_ref[pl.ds(k0, half), :] = zeros
                vs_ref[pl.ds(k0 + half, L), :] = stage_ref[2, rows, :].astype(BF16)
                vs_ref[pl.ds(k0 + half + L, half), :] = zeros
                return carry

            lax.fori_loop(0, d, regroup, 0)

        def tile(u, carry, gi=gi, d=d, nblk=nblk, tk=tk, q_in=q_in):
            r = u // nblk
            i = u % nblk
            q0 = pl.multiple_of(u * tq, tq)
            k0 = pl.multiple_of((u + r) * tq, tq)
            kind = jnp.where(i == 0, 1, 0) + jnp.where(i == nblk - 1, 2, 0)
            qp = q_in[0, pl.ds(q0, tq), :] if d == 1 else qs_ref[pl.ds(q0, tq), :]
            qp = qp * (ATTN_HEAD_DIM ** -0.5)
            kp = ks_ref[pl.ds(k0, tk), :]
            vp = vs_ref[pl.ds(k0, tk), :]
            outs, lses = [], []
            for hh in range(2):
                keep = first_head if hh == 0 else jnp.logical_not(first_head)
                qh = jnp.where(keep, qp, jnp.zeros_like(qp))
                s = lax.dot_general(qh, kp, (((1,), (1,)), ((), ())), preferred_element_type=F32)
                s = s + bias_ref[gi, kind, hh]
                m = jnp.max(s, axis=-1, keepdims=True)
                p = jnp.exp(s - m)
                den = jnp.sum(p, axis=-1, keepdims=True)
                pv = jnp.dot(p.astype(BF16), vp, preferred_element_type=F32)
                outs.append(pv * (1.0 / den))
                lses.append(jnp.broadcast_to(m + jnp.log(den), (tq, LANES)))
            rows = pl.ds(q0, tq) if d == 1 else pl.ds(i * (tq * d) + r, tq, stride=d)
            og_ref[gi, rows, :] = jnp.where(first_head, outs[0], outs[1])
            lg_ref[gi, rows, :] = jnp.where(first_head, lses[0], lses[1])
            return carry

        lax.fori_loop(0, S // tq, tile, 0, unroll=ATTN_TILES_IN_FLIGHT)

    def merge(j, carry):
        rows = pl.ds(pl.multiple_of(j * tq, tq), tq)
        lse = [lg_ref[g, rows, :] for g in range(n_groups)]
        top = functools.reduce(jnp.maximum, lse)
        w = [jnp.exp(l - top) for l in lse]
        num = sum(w[g] * og_ref[g, rows, :] for g in range(n_groups))
        o_ref[0, rows, :] = (num / sum(w)).astype(o_ref.dtype)
        return carry

    lax.fori_loop(0, S // tq, merge, 0)


def _dilated_attention(proj):
    B, S, _ = proj.shape
    W = ATTN_WIDTH
    n_groups = len(ATTN_PATTERNS)
    halves = {w // (2 * d) for w, d in ATTN_PATTERNS}
    assert len(halves) == 1 and all(S % (d * ATTN_TQ) == 0 for _, d in ATTN_PATTERNS)
    half = halves.pop()
    tk = ATTN_TQ + 2 * half
    pairs = W // LANES
    col = lambda c0: pl.BlockSpec((1, S, LANES), lambda hp, b: (b, 0, c0 // LANES + hp))
    return pl.pallas_call(
        functools.partial(_attn_kernel, S=S, patterns=ATTN_PATTERNS),
        grid=(pairs, B),
        in_specs=[col((3 * g + t) * W) for g in range(n_groups) for t in range(3)],
        out_specs=pl.BlockSpec((1, S, LANES), lambda hp, b: (b, 0, hp)),
        out_shape=jax.ShapeDtypeStruct((B, S, W), BF16),
        scratch_shapes=[pltpu.VMEM((3, S, LANES), F32),
                        pltpu.VMEM((S, LANES), BF16),
                        pltpu.VMEM((2 * S, LANES), BF16),
                        pltpu.VMEM((2 * S, LANES), BF16),
                        pltpu.VMEM((n_groups, S, LANES), F32),
                        pltpu.VMEM((n_groups, S, LANES), F32),
                        pltpu.VMEM((n_groups, 4, 2, ATTN_TQ, tk), F32)],
        compiler_params=_params("arbitrary", "arbitrary"),
        name="dilated_attention",
    )(*([proj] * (3 * n_groups)))


def _log_sigmoid(x):
    return jnp.minimum(x, 0.0) - jnp.log1p(jnp.exp(-jnp.abs(x)))


def _retention_kernel(logit_ref, q_ref, k_ref, v_ref, gate_ref, o_ref, yf_ref, yb_ref, sf_ref, sb_ref):
    C = RET_CHUNK
    S, Dk = q_ref.shape[1], q_ref.shape[2]
    N = S // C
    h = pl.program_id(1)
    lf = _log_sigmoid(jnp.full((C, 1), logit_ref[0, h], F32))
    lb = _log_sigmoid(jnp.full((C, 1), logit_ref[1, h], F32))
    rel = (lax.broadcasted_iota(jnp.int32, (C, C), 0) - lax.broadcasted_iota(jnp.int32, (C, C), 1)).astype(F32)
    decay = jnp.exp(jnp.where(rel >= 0, lf * rel, -lb * rel))
    pos = lax.broadcasted_iota(jnp.int32, (C, 1), 0).astype(F32)
    xi_f = jnp.exp(lf * (pos + 1.0))
    zeta_f = jnp.exp(lf * (C - 1.0 - pos))
    xi_b = jnp.exp(lb * (C - pos))
    zeta_b = jnp.exp(lb * pos)
    chunk_f = jnp.exp(_log_sigmoid(jnp.full((Dk, 1), logit_ref[0, h], F32)) * C)
    chunk_b = jnp.exp(_log_sigmoid(jnp.full((Dk, 1), logit_ref[1, h], F32)) * C)
    scale = Dk ** -0.5

    def load(i):
        r0 = pl.multiple_of(i * C, C)
        return r0, q_ref[0, pl.ds(r0, C), :] * scale, k_ref[0, pl.ds(r0, C), :], v_ref[0, pl.ds(r0, C), :]

    def cross(state_ref, qc, kc, vc, xi, zeta, chunk_decay):
        st = state_ref[...]
        y = jnp.dot(qc, st.astype(BF16), preferred_element_type=F32) * xi
        kz = (kc.astype(F32) * zeta).astype(BF16)
        u = lax.dot_general(kz, vc, (((0,), (0,)), ((), ())), preferred_element_type=F32)
        state_ref[...] = st * chunk_decay + u
        return y

    sf_ref[...] = jnp.zeros_like(sf_ref)
    sb_ref[...] = jnp.zeros_like(sb_ref)

    def step(i, carry):
        r0, qc, kc, vc = load(i)
        s = lax.dot_general(qc, kc, (((1,), (1,)), ((), ())), preferred_element_type=F32) * decay
        y = jnp.dot(s.astype(BF16), vc, preferred_element_type=F32)
        yf_ref[pl.ds(r0, C), :] = y + cross(sf_ref, qc, kc, vc, xi_f, zeta_f, chunk_f)
        r1, qd, kd, vd = load(N - 1 - i)
        yb_ref[pl.ds(r1, C), :] = cross(sb_ref, qd, kd, vd, xi_b, zeta_b, chunk_b)
        return carry

    lax.fori_loop(0, N, step, 0, unroll=2)

    def finish(i, carry):
        rows = pl.ds(pl.multiple_of(i * C, C), C)
        r = yf_ref[rows, :] + yb_ref[rows, :]
        r = r * lax.rsqrt(jnp.mean(r * r, axis=-1, keepdims=True) + EPS)
        g = gate_ref[0, rows, :]
        o_ref[0, rows, :] = (g / (1.0 + jnp.exp(-g)) * r).astype(o_ref.dtype)
        return carry

    lax.fori_loop(0, N, finish, 0)


def _retention(proj, gates, decay_logit, q_col0):
    B, S, _ = proj.shape
    Dk = gates.shape[2] // 3 // RET_HEADS
    qb = q_col0 // Dk
    nb = RET_HEADS
    return pl.pallas_call(
        _retention_kernel,
        grid=(B, RET_HEADS),
        in_specs=[pl.BlockSpec(memory_space=pltpu.SMEM),
                  pl.BlockSpec((1, S, Dk), lambda b, h: (b, 0, qb + h)),
                  pl.BlockSpec((1, S, Dk), lambda b, h: (b, 0, qb + nb + h)),
                  pl.BlockSpec((1, S, Dk), lambda b, h: (b, 0, qb + 2 * nb + h)),
                  pl.BlockSpec((1, S, Dk), lambda b, h: (b, 0, h))],
        out_specs=pl.BlockSpec((1, S, Dk), lambda b, h: (b, 0, h)),
        out_shape=jax.ShapeDtypeStruct((B, S, RET_HEADS * Dk), BF16),
        scratch_shapes=[pltpu.VMEM((S, Dk), F32), pltpu.VMEM((S, Dk), F32),
                        pltpu.VMEM((Dk, Dk), F32), pltpu.VMEM((Dk, Dk), F32)],
        compiler_params=_params("parallel", "parallel"),
        name="retention",
    )(decay_logit, proj, proj, proj, gates)


def _merge_kernel(attn_ref, ret_ref, ga_ref, gr_ref, x_ref, wa_ref, wr_ref, wo_ref, gain_ref, wrt_ref,
                  wrt_lo_ref, xo_ref, h_ref, logit_ref):
    pa = jnp.dot(attn_ref[...], wa_ref[...], preferred_element_type=F32)
    pr = jnp.dot(ret_ref[...], wr_ref[...], preferred_element_type=F32)
    ga, gr = ga_ref[...], gr_ref[...]
    merged = pa / (1.0 + jnp.exp(-ga)) + pr / (1.0 + jnp.exp(-gr))
    x = x_ref[...] + jnp.dot(merged.astype(BF16), wo_ref[...], preferred_element_type=F32)
    xo_ref[...] = x
    h = _rms(x, gain_ref[...])
    h_hi = h.astype(BF16)
    h_ref[...] = h_hi
    h_lo = (h - h_hi.astype(F32)).astype(BF16)
    logit_ref[...] = (jnp.dot(h_hi, wrt_ref[...], preferred_element_type=F32)
                      + jnp.dot(h_lo, wrt_ref[...], preferred_element_type=F32)
                      + jnp.dot(h_hi, wrt_lo_ref[...], preferred_element_type=F32))


def _merge(attn, ret, gates, x2d, wa, wr, wo, gain, w_router, tm):
    T, D = x2d.shape
    w_router_pad = jnp.pad(w_router, ((0, 0), (0, LANES - w_router.shape[1])))
    wrt_hi = w_router_pad.astype(BF16)
    wrt_lo = (w_router_pad - wrt_hi.astype(F32)).astype(BF16)
    tok = lambda width, col=0: pl.BlockSpec((tm, width), lambda i: (i, col))
    full = lambda a: pl.BlockSpec(a.shape, lambda i: (0, 0))
    return pl.pallas_call(
        _merge_kernel,
        grid=(T // tm,),
        in_specs=[tok(ATTN_WIDTH), tok(D), tok(D, 1), tok(D, 2), tok(D),
                  full(wa), full(wr), full(wo), pl.BlockSpec((1, D), lambda i: (0, 0)), full(wrt_hi), full(wrt_lo)],
        out_specs=[tok(D), tok(D), tok(LANES)],
        out_shape=[jax.ShapeDtypeStruct((T, D), F32), jax.ShapeDtypeStruct((T, D), BF16),
                   jax.ShapeDtypeStruct((T, LANES), F32)],
        compiler_params=_params("parallel"),
        name="merge_out_proj",
    )(attn, ret, gates, gates, x2d, wa, wr, wo, gain.reshape(1, D), wrt_hi, wrt_lo)


def _route_kernel(logit_ref, pos_ref, aff_ref, before_ref, *, n_experts, cap):
    S = logit_ref.shape[1]
    rows = 256

    @pl.when(pl.program_id(0) == 0)
    def _():
        def fill(i, carry):
            r0 = pl.multiple_of(i * rows, rows)
            r = lax.broadcasted_iota(jnp.int32, (rows, S), 0) + r0
            c = lax.broadcasted_iota(jnp.int32, (rows, S), 1)
            before_ref[pl.ds(r0, rows), :] = jnp.where(r < c, 1.0, 0.0).astype(BF16)
            return carry
        lax.fori_loop(0, S // rows, fill, 0)

    lane = lax.broadcasted_iota(jnp.int32, (1, LANES), 1)
    lg = jnp.where(lane < n_experts, logit_ref[0], MASKED)
    ex = jnp.exp(lg - jnp.max(lg, axis=-1, keepdims=True))
    aff = (ex / jnp.sum(ex, axis=-1, keepdims=True)).T[:n_experts]

    def search(it, thr_bits):
        cand = thr_bits | jnp.left_shift(jnp.int32(1), 30 - it)
        cnt = jnp.sum((aff >= lax.bitcast_convert_type(cand, F32)).astype(jnp.int32), axis=-1, keepdims=True)
        return jnp.where(cnt >= cap, cand, thr_bits)

    floor = lax.bitcast_convert_type(lax.fori_loop(0, 31, search, jnp.zeros((n_experts, 1), jnp.int32)), F32)
    thr = jnp.min(jnp.where(aff >= floor, aff, jnp.inf), axis=-1, keepdims=True)
    above = aff > thr
    tied = aff == thr
    need = (cap - jnp.sum(above.astype(jnp.int32), axis=-1, keepdims=True)).astype(F32)
    count_before = lambda mask: jnp.dot(jnp.where(mask, 1.0, 0.0).astype(BF16), before_ref[...],
                                        preferred_element_type=F32)
    chosen = above | (tied & (count_before(tied) < need))
    pos_ref[0] = jnp.where(chosen, count_before(chosen), -1.0)
    aff_ref[0] = aff


def _route(logits, cap):
    B, S, _ = logits.shape
    E = N_EXPERTS
    return pl.pallas_call(
        functools.partial(_route_kernel, n_experts=E, cap=cap),
        grid=(B,),
        in_specs=[pl.BlockSpec((1, S, LANES), lambda b: (b, 0, 0))],
        out_specs=[pl.BlockSpec((1, E, S), lambda b: (b, 0, 0)), pl.BlockSpec((1, E, S), lambda b: (b, 0, 0))],
        out_shape=[jax.ShapeDtypeStruct((B, E, S), F32), jax.ShapeDtypeStruct((B, E, S), F32)],
        scratch_shapes=[pltpu.VMEM((S, S), BF16)],
        compiler_params=_params("arbitrary"),
        name="route",
    )(logits)


def _slot_mask(pos_ref, cap):
    pos = pos_ref[0, 0].astype(jnp.int32)
    slot = lax.broadcasted_iota(jnp.int32, (cap, pos.shape[1]), 0)
    return slot == pos


def _gather_kernel(h_ref, pos_ref, aff_ref, x_ref, g_ref, *, cap):
    mask = _slot_mask(pos_ref, cap)
    onehot = jnp.where(mask, 1.0, 0.0).astype(BF16)
    x_ref[0, 0] = jnp.dot(onehot, h_ref[0], preferred_element_type=F32).astype(x_ref.dtype)
    g = jnp.sum(jnp.where(mask, aff_ref[0, 0], 0.0), axis=-1, keepdims=True)
    g_ref[0, 0] = jnp.broadcast_to(g, (cap, LANES))


def _gather(h, pos, aff, cap):
    B, S, D = h.shape
    E = pos.shape[1]
    row = pl.BlockSpec((1, 1, 1, S), lambda b, e: (b, e, 0, 0))
    return pl.pallas_call(
        functools.partial(_gather_kernel, cap=cap),
        grid=(B, E),
        in_specs=[pl.BlockSpec((1, S, D), lambda b, e: (b, 0, 0)), row, row],
        out_specs=[pl.BlockSpec((1, 1, cap, D), lambda b, e: (e, b, 0, 0)),
                   pl.BlockSpec((1, 1, cap, LANES), lambda b, e: (e, b, 0, 0))],
        out_shape=[jax.ShapeDtypeStruct((E, B, cap, D), BF16), jax.ShapeDtypeStruct((E, B, cap, LANES), F32)],
        compiler_params=_params("parallel", "arbitrary"),
        name="expert_gather",
    )(h, pos.reshape(B, E, 1, S), aff.reshape(B, E, 1, S))


def _expert_kernel(x_ref, g_ref, wg_ref, wu_ref, wd_ref, y_ref, acc_ref):
    f = pl.program_id(1)
    x = x_ref[0]
    a = jnp.dot(x, wg_ref[...].astype(BF16), preferred_element_type=F32)
    u = jnp.dot(x, wu_ref[...].astype(BF16), preferred_element_type=F32)
    act = (a / (1.0 + jnp.exp(-a)) * u).astype(BF16)
    part = jnp.dot(act, wd_ref[...].astype(BF16), preferred_element_type=F32)

    @pl.when(f == 0)
    def _():
        acc_ref[...] = part

    @pl.when(f > 0)
    def _():
        acc_ref[...] += part

    @pl.when(f == pl.num_programs(1) - 1)
    def _():
        y_ref[0] = (acc_ref[...] * g_ref[0][:, :1]).astype(y_ref.dtype)


def _experts(xin, g, w_gate, w_up, w_down, layer, tf):
    E, M, D = xin.shape
    FF = w_gate.shape[3]
    return pl.pallas_call(
        _expert_kernel,
        grid=(E, FF // tf),
        in_specs=[pl.BlockSpec((1, M, D), lambda e, f: (e, 0, 0)),
                  pl.BlockSpec((1, M, LANES), lambda e, f: (e, 0, 0)),
                  pl.BlockSpec((None, None, D, tf), lambda e, f: (layer, e, 0, f)),
                  pl.BlockSpec((None, None, D, tf), lambda e, f: (layer, e, 0, f)),
                  pl.BlockSpec((None, None, tf, D), lambda e, f: (layer, e, f, 0))],
        out_specs=pl.BlockSpec((1, M, D), lambda e, f: (e, 0, 0)),
        out_shape=jax.ShapeDtypeStruct((E, M, D), BF16),
        scratch_shapes=[pltpu.VMEM((M, D), F32)],
        compiler_params=_params("parallel", "arbitrary"),
        name="expert_swiglu",
    )(xin, g, w_gate, w_up, w_down)


def _scatter_kernel(x_ref, pos_ref, y_ref, o_ref, *, cap):
    onehot = jnp.where(_slot_mask(pos_ref, cap), 1.0, 0.0).astype(BF16)
    add = lax.dot_general(onehot, y_ref[0, 0], (((0,), (0,)), ((), ())), preferred_element_type=F32)

    @pl.when(pl.program_id(2) == 0)
    def _():
        o_ref[0] = x_ref[0] + add

    @pl.when(pl.program_id(2) > 0)
    def _():
        o_ref[0] += add


def _scatter(x, pos, y, cap, tn):
    B, S, D = x.shape
    E = pos.shape[1]
    return pl.pallas_call(
        functools.partial(_scatter_kernel, cap=cap),
        grid=(B, D // tn, E),
        in_specs=[pl.BlockSpec((1, S, tn), lambda b, n, e: (b, 0, n)),
                  pl.BlockSpec((1, 1, 1, S), lambda b, n, e: (b, e, 0, 0)),
                  pl.BlockSpec((1, 1, cap, tn), lambda b, n, e: (e, b, 0, n))],
        out_specs=pl.BlockSpec((1, S, tn), lambda b, n, e: (b, 0, n)),
        out_shape=jax.ShapeDtypeStruct((B, S, D), F32),
        compiler_params=_params("parallel", "parallel", "arbitrary"),
        name="expert_scatter",
    )(x, pos.reshape(B, E, 1, S), y)


def _final_norm_kernel(x_ref, gain_ref, o_ref):
    o_ref[...] = _rms(x_ref[...], gain_ref[...])


def _final_norm(x2d, gain, tm):
    T, D = x2d.shape
    return pl.pallas_call(
        _final_norm_kernel,
        grid=(T // tm,),
        in_specs=[pl.BlockSpec((tm, D), lambda i: (i, 0)), pl.BlockSpec((1, D), lambda i: (0, 0))],
        out_specs=pl.BlockSpec((tm, D), lambda i: (i, 0)),
        out_shape=jax.ShapeDtypeStruct((T, D), F32),
        compiler_params=_params("parallel"),
        name="final_norm",
    )(x2d, gain.reshape(1, D))


def kernel(x, w_in, w_attn_out, w_ret_out, w_out, ret_decay_logit, norm_mix, norm_ffn, w_router, w_gate, w_up,
           w_down, norm_final):
    B, S, D = x.shape
    T = B * S
    depth = w_in.shape[0]
    W = ATTN_WIDTH
    attn_in = len(ATTN_PATTERNS) * 3 * W
    main_cols = attn_in + 3 * D
    gate_cols = 3 * D
    cap = CAPACITY_FACTOR * S // N_EXPERTS
    x2d = x.reshape(T, D)
    for layer in range(depth):
        w_in_b = w_in[layer].astype(BF16)
        proj = _norm_proj(x2d, norm_mix[layer], w_in_b, 0, main_cols, BF16, tm=2048, tn=1536)
        gates = _norm_proj(x2d, norm_mix[layer], w_in_b, main_cols, gate_cols, F32, tm=1024, tn=1536)
        proj3 = proj.reshape(B, S, main_cols)

        attn = _dilated_attention(proj3)
        ret = _retention(proj3, gates.reshape(B, S, gate_cols), ret_decay_logit[layer], attn_in)

        x2d, h2, logits = _merge(attn.reshape(T, W), ret.reshape(T, D), gates, x2d, w_attn_out[layer].astype(BF16),
                                 w_ret_out[layer].astype(BF16), w_out[layer].astype(BF16), norm_ffn[layer],
                                 w_router[layer], tm=512)

        pos, aff = _route(logits.reshape(B, S, LANES), cap)
        xin, g = _gather(h2.reshape(B, S, D), pos, aff, cap)
        y = _experts(xin.reshape(N_EXPERTS, B * cap, D), g.reshape(N_EXPERTS, B * cap, LANES),
                     w_gate, w_up, w_down, layer, tf=256)
        x2d = _scatter(x2d.reshape(B, S, D), pos, y.reshape(N_EXPERTS, B, cap, D), cap, tn=512).reshape(T, D)
    return _final_norm(x2d, norm_final, tm=1024).reshape(B, S, D)
```

```python
import functools

import jax
import jax.numpy as jnp
from jax import lax
from jax.experimental import pallas as pl
from jax.experimental.pallas import tpu as pltpu

EPS = 1e-6
ATTN_PATTERNS = ((128, 1), (512, 4), (2048, 16))
ATTN_HEADS = 8
ATTN_HEAD_DIM = 64
ATTN_WIDTH = ATTN_HEADS * ATTN_HEAD_DIM
RET_HEADS = 4
RET_CHUNK = 128
N_EXPERTS = 16
CAPACITY_FACTOR = 2

LANES = 128
MASKED = -1e30
VMEM_LIMIT_BYTES = 56 * 1024 * 1024

F32 = jnp.float32
BF16 = jnp.bfloat16


def _params(*semantics):
    return pltpu.CompilerParams(dimension_semantics=semantics, vmem_limit_bytes=VMEM_LIMIT_BYTES)


def _rms(x, gain):
    return x * lax.rsqrt(jnp.mean(x * x, axis=-1, keepdims=True) + EPS) * gain


def _norm_proj_kernel(x_ref, gain_ref, w_ref, o_ref, h_ref):
    @pl.when(pl.program_id(1) == 0)
    def _():
        h_ref[...] = _rms(x_ref[...], gain_ref[...]).astype(BF16)

    o_ref[...] = jnp.dot(h_ref[...], w_ref[...], preferred_element_type=F32).astype(o_ref.dtype)


def _norm_proj(x2d, gain, w, col0, ncols, out_dtype, tm, tn):
    T, D = x2d.shape
    assert T % tm == 0 and ncols % tn == 0 and col0 % tn == 0
    return pl.pallas_call(
        _norm_proj_kernel,
        grid=(T // tm, ncols // tn),
        in_specs=[pl.BlockSpec((tm, D), lambda i, j: (i, 0)),
                  pl.BlockSpec((1, D), lambda i, j: (0, 0)),
                  pl.BlockSpec((D, tn), lambda i, j: (0, j + col0 // tn))],
        out_specs=pl.BlockSpec((tm, tn), lambda i, j: (i, j)),
        out_shape=jax.ShapeDtypeStruct((T, ncols), out_dtype),
        scratch_shapes=[pltpu.VMEM((tm, D), BF16)],
        compiler_params=_params("parallel", "arbitrary"),
        name="norm_proj",
    )(x2d, gain.reshape(1, D), w)


ATTN_TQ = 128
ATTN_TILES_IN_FLIGHT = 4


def _attn_kernel(*refs, S, patterns):
    n_groups = len(patterns)
    in_refs, o_ref = refs[:3 * n_groups], refs[3 * n_groups]
    stage_ref, qs_ref, ks_ref, vs_ref, og_ref, lg_ref, bias_ref = refs[3 * n_groups + 1:]
    tq = ATTN_TQ
    hp = pl.program_id(0)
    lane = lax.broadcasted_iota(jnp.int32, (1, LANES), 1)
    first_head = lane < ATTN_HEAD_DIM

    for gi, (window, d) in enumerate(patterns):
        half = window // (2 * d)
        tk = tq + 2 * half
        L = S // d
        nblk = L // tq
        seg = L + 2 * half
        q_in, k_in, v_in = in_refs[3 * gi:3 * gi + 3]

        @pl.when(pl.program_id(1) == 0)
        def _(gi=gi, d=d, half=half, tk=tk):
            r = lax.broadcasted_iota(jnp.int32, (tq, tk), 0)
            c = lax.broadcasted_iota(jnp.int32, (tq, tk), 1)
            rel = jnp.abs(c - half - r)
            dist = (d * rel).astype(F32)
            for hh in range(2):
                pow2 = jnp.full((tq, tk), jnp.left_shift(jnp.int32(2), 2 * hp + hh), jnp.int32).astype(F32)
                band = jnp.where(rel <= half, -(dist / pow2), MASKED)
                for kind in range(4):
                    tile_bias = band
                    if kind & 1:
                        tile_bias = jnp.where(c < half, MASKED, tile_bias)
                    if kind & 2:
                        tile_bias = jnp.where(c >= tq + half, MASKED, tile_bias)
                    bias_ref[gi, kind, hh] = tile_bias

        zeros = jnp.zeros((half, LANES), BF16)
        if d == 1:
            ks_ref[pl.ds(0, half), :] = zeros
            ks_ref[pl.ds(half, S), :] = k_in[0]
            ks_ref[pl.ds(half + S, half), :] = zeros
            vs_ref[pl.ds(0, half), :] = zeros
            vs_ref[pl.ds(half, S), :] = v_in[0]
            vs_ref[pl.ds(half + S, half), :] = zeros
        else:
            stage_ref[0] = q_in[0].astype(F32)
            stage_ref[1] = k_in[0].astype(F32)
            stage_ref[2] = v_in[0].astype(F32)

            def regroup(r, carry, d=d, L=L, seg=seg, half=half):
                rows = pl.ds(r, L, stride=d)
                k0 = pl.multiple_of(r * seg, half)
                qs_ref[pl.ds(pl.multiple_of(r * L, tq), L), :] = stage_ref[0, rows, :].astype(BF16)
                ks_ref[pl.ds(k0, half), :] = zeros
                ks_ref[pl.ds(k0 + half, L), :] = stage_ref[1, rows, :].astype(BF16)
                ks_ref[pl.ds(k0 + half + L, half), :] = zeros
                vs_ref[pl.ds(k0, half), :] = zeros
                vs_ref[pl.ds(k0 + half, L), :] = stage_ref[2, rows, :].astype(BF16)
                vs_ref[pl.ds(k0 + half + L, half), :] = zeros
                return carry

            lax.fori_loop(0, d, regroup, 0)

        def tile(u, carry, gi=gi, d=d, nblk=nblk, tk=tk, q_in=q_in):
            r = u // nblk
            i = u % nblk
            q0 = pl.multiple_of(u * tq, tq)
            k0 = pl.multiple_of((u + r) * tq, tq)
            kind = jnp.where(i == 0, 1, 0) + jnp.where(i == nblk - 1, 2, 0)
            qp = q_in[0, pl.ds(q0, tq), :] if d == 1 else qs_ref[pl.ds(q0, tq), :]
            qp = qp * (ATTN_HEAD_DIM ** -0.5)
            kp = ks_ref[pl.ds(k0, tk), :]
            vp = vs_ref[pl.ds(k0, tk), :]
            outs, lses = [], []
            for hh in range(2):
                keep = first_head if hh == 0 else jnp.logical_not(first_head)
                qh = jnp.where(keep, qp, jnp.zeros_like(qp))
                s = lax.dot_general(qh, kp, (((1,), (1,)), ((), ())), preferred_element_type=F32)
                s = s + bias_ref[gi, kind, hh]
                m = jnp.max(s, axis=-1, keepdims=True)
                p = jnp.exp(s - m)
                den = jnp.sum(p, axis=-1, keepdims=True)
                pv = jnp.dot(p.astype(BF16), vp, preferred_element_type=F32)
                outs.append(pv * (1.0 / den))
                lses.append(jnp.broadcast_to(m + jnp.log(den), (tq, LANES)))
            rows = pl.ds(q0, tq) if d == 1 else pl.ds(i * (tq * d) + r, tq, stride=d)
            og_ref[gi, rows, :] = jnp.where(first_head, outs[0], outs[1])
            lg_ref[gi, rows, :] = jnp.where(first_head, lses[0], lses[1])
            return carry

        lax.fori_loop(0, S // tq, tile, 0, unroll=ATTN_TILES_IN_FLIGHT)

    def merge(j, carry):
        rows = pl.ds(pl.multiple_of(j * tq, tq), tq)
        lse = [lg_ref[g, rows, :] for g in range(n_groups)]
        top = functools.reduce(jnp.maximum, lse)
        w = [jnp.exp(l - top) for l in lse]
        num = sum(w[g] * og_ref[g, rows, :] for g in range(n_groups))
        o_ref[0, rows, :] = (num / sum(w)).astype(o_ref.dtype)
        return carry

    lax.fori_loop(0, S // tq, merge, 0)


def _dilated_attention(proj):
    B, S, _ = proj.shape
    W = ATTN_WIDTH
    n_groups = len(ATTN_PATTERNS)
    halves = {w // (2 * d) for w, d in ATTN_PATTERNS}
    assert len(halves) == 1 and all(S % (d * ATTN_TQ) == 0 for _, d in ATTN_PATTERNS)
    half = halves.pop()
    tk = ATTN_TQ + 2 * half
    pairs = W // LANES
    col = lambda c0: pl.BlockSpec((1, S, LANES), lambda hp, b: (b, 0, c0 // LANES + hp))
    return pl.pallas_call(
        functools.partial(_attn_kernel, S=S, patterns=ATTN_PATTERNS),
        grid=(pairs, B),
        in_specs=[col((3 * g + t) * W) for g in range(n_groups) for t in range(3)],
        out_specs=pl.BlockSpec((1, S, LANES), lambda hp, b: (b, 0, hp)),
        out_shape=jax.ShapeDtypeStruct((B, S, W), BF16),
        scratch_shapes=[pltpu.VMEM((3, S, LANES), F32),
                        pltpu.VMEM((S, LANES), BF16),
                        pltpu.VMEM((2 * S, LANES), BF16),
                        pltpu.VMEM((2 * S, LANES), BF16),
                        pltpu.VMEM((n_groups, S, LANES), F32),
                        pltpu.VMEM((n_groups, S, LANES), F32),
                        pltpu.VMEM((n_groups, 4, 2, ATTN_TQ, tk), F32)],
        compiler_params=_params("arbitrary", "arbitrary"),
        name="dilated_attention",
    )(*([proj] * (3 * n_groups)))


def _log_sigmoid(x):
    return jnp.minimum(x, 0.0) - jnp.log1p(jnp.exp(-jnp.abs(x)))


def _retention_kernel(logit_ref, q_ref, k_ref, v_ref, gate_ref, o_ref, yf_ref, yb_ref, sf_ref, sb_ref):
    C = RET_CHUNK
    S, Dk = q_ref.shape[1], q_ref.shape[2]
    N = S // C
    h = pl.program_id(1)
    lf = _log_sigmoid(jnp.full((C, 1), logit_ref[0, h], F32))
    lb = _log_sigmoid(jnp.full((C, 1), logit_ref[1, h], F32))
    rel = (lax.broadcasted_iota(jnp.int32, (C, C), 0) - lax.broadcasted_iota(jnp.int32, (C, C), 1)).astype(F32)
    decay = jnp.exp(jnp.where(rel >= 0, lf * rel, -lb * rel))
    pos = lax.broadcasted_iota(jnp.int32, (C, 1), 0).astype(F32)
    xi_f = jnp.exp(lf * (pos + 1.0))
    zeta_f = jnp.exp(lf * (C - 1.0 - pos))
    xi_b = jnp.exp(lb * (C - pos))
    zeta_b = jnp.exp(lb * pos)
    chunk_f = jnp.exp(_log_sigmoid(jnp.full((Dk, 1), logit_ref[0, h], F32)) * C)
    chunk_b = jnp.exp(_log_sigmoid(jnp.full((Dk, 1), logit_ref[1, h], F32)) * C)
    scale = Dk ** -0.5

    def load(i):
        r0 = pl.multiple_of(i * C, C)
        return r0, q_ref[0, pl.ds(r0, C), :] * scale, k_ref[0, pl.ds(r0, C), :], v_ref[0, pl.ds(r0, C), :]

    def cross(state_ref, qc, kc, vc, xi, zeta, chunk_decay):
        st = state_ref[...]
        y = jnp.dot(qc, st.astype(BF16), preferred_element_type=F32) * xi
        kz = (kc.astype(F32) * zeta).astype(BF16)
        u = lax.dot_general(kz, vc, (((0,), (0,)), ((), ())), preferred_element_type=F32)
        state_ref[...] = st * chunk_decay + u
        return y

    sf_ref[...] = jnp.zeros_like(sf_ref)
    sb_ref[...] = jnp.zeros_like(sb_ref)

    def step(i, carry):
        r0, qc, kc, vc = load(i)
        s = lax.dot_general(qc, kc, (((1,), (1,)), ((), ())), preferred_element_type=F32) * decay
        y = jnp.dot(s.astype(BF16), vc, preferred_element_type=F32)
        yf_ref[pl.ds(r0, C), :] = y + cross(sf_ref, qc, kc, vc, xi_f, zeta_f, chunk_f)
        r1, qd, kd, vd = load(N - 1 - i)
        yb_ref[pl.ds(r1, C), :] = cross(sb_ref, qd, kd, vd, xi_b, zeta_b, chunk_b)
        return carry

    lax.fori_loop(0, N, step, 0, unroll=2)

    def finish(i, carry):
        rows = pl.ds(pl.multiple_of(i * C, C), C)
        r = yf_ref[rows, :] + yb_ref[rows, :]
        r = r * lax.rsqrt(jnp.mean(r * r, axis=-1, keepdims=True) + EPS)
        g = gate_ref[0, rows, :]
        o_ref[0, rows, :] = (g / (1.0 + jnp.exp(-g)) * r).astype(o_ref.dtype)
        return carry

    lax.fori_loop(0, N, finish, 0)


def _retention(proj, gates, decay_logit, q_col0):
    B, S, _ = proj.shape
    Dk = gates.shape[2] // 3 // RET_HEADS
    qb = q_col0 // Dk
    nb = RET_HEADS
    return pl.pallas_call(
        _retention_kernel,
        grid=(B, RET_HEADS),
        in_specs=[pl.BlockSpec(memory_space=pltpu.SMEM),
                  pl.BlockSpec((1, S, Dk), lambda b, h: (b, 0, qb + h)),
                  pl.BlockSpec((1, S, Dk), lambda b, h: (b, 0, qb + nb + h)),
                  pl.BlockSpec((1, S, Dk), lambda b, h: (b, 0, qb + 2 * nb + h)),
                  pl.BlockSpec((1, S, Dk), lambda b, h: (b, 0, h))],
        out_specs=pl.BlockSpec((1, S, Dk), lambda b, h: (b, 0, h)),
        out_shape=jax.ShapeDtypeStruct((B, S, RET_HEADS * Dk), BF16),
        scratch_shapes=[pltpu.VMEM((S, Dk), F32), pltpu.VMEM((S, Dk), F32),
                        pltpu.VMEM((Dk, Dk), F32), pltpu.VMEM((Dk, Dk), F32)],
        compiler_params=_params("parallel", "parallel"),
        name="retention",
    )(decay_logit, proj, proj, proj, gates)


def _merge_kernel(attn_ref, ret_ref, ga_ref, gr_ref, x_ref, wa_ref, wr_ref, wo_ref, gain_ref, wrt_ref,
                  wrt_lo_ref, xo_ref, h_ref, logit_ref):
    pa = jnp.dot(attn_ref[...], wa_ref[...], preferred_element_type=F32)
    pr = jnp.dot(ret_ref[...], wr_ref[...], preferred_element_type=F32)
    ga, gr = ga_ref[...], gr_ref[...]
    merged = pa / (1.0 + jnp.exp(-ga)) + pr / (1.0 + jnp.exp(-gr))
    x = x_ref[...] + jnp.dot(merged.astype(BF16), wo_ref[...], preferred_element_type=F32)
    xo_ref[...] = x
    h = _rms(x, gain_ref[...])
    h_hi = h.astype(BF16)
    h_ref[...] = h_hi
    h_lo = (h - h_hi.astype(F32)).astype(BF16)
    logit_ref[...] = (jnp.dot(h_hi, wrt_ref[...], preferred_element_type=F32)
                      + jnp.dot(h_lo, wrt_ref[...], preferred_element_type=F32)
                      + jnp.dot(h_hi, wrt_lo_ref[...], preferred_element_type=F32))


def _merge(attn, ret, gates, x2d, wa, wr, wo, gain, w_router, tm):
    T, D = x2d.shape
    w_router_pad = jnp.pad(w_router, ((0, 0), (0, LANES - w_router.shape[1])))
    wrt_hi = w_router_pad.astype(BF16)
    wrt_lo = (w_router_pad - wrt_hi.astype(F32)).astype(BF16)
    tok = lambda width, col=0: pl.BlockSpec((tm, width), lambda i: (i, col))
    full = lambda a: pl.BlockSpec(a.shape, lambda i: (0, 0))
    return pl.pallas_call(
        _merge_kernel,
        grid=(T // tm,),
        in_specs=[tok(ATTN_WIDTH), tok(D), tok(D, 1), tok(D, 2), tok(D),
                  full(wa), full(wr), full(wo), pl.BlockSpec((1, D), lambda i: (0, 0)), full(wrt_hi), full(wrt_lo)],
        out_specs=[tok(D), tok(D), tok(LANES)],
        out_shape=[jax.ShapeDtypeStruct((T, D), F32), jax.ShapeDtypeStruct((T, D), BF16),
                   jax.ShapeDtypeStruct((T, LANES), F32)],
        compiler_params=_params("parallel"),
        name="merge_out_proj",
    )(attn, ret, gates, gates, x2d, wa, wr, wo, gain.reshape(1, D), wrt_hi, wrt_lo)


def _route_kernel(logit_ref, pos_ref, pos_t_ref, aff_ref, before_ref, *, n_experts, cap):
    S = logit_ref.shape[1]
    rows = 256

    @pl.when(pl.program_id(0) == 0)
    def _():
        def fill(i, carry):
            r0 = pl.multiple_of(i * rows, rows)
            r = lax.broadcasted_iota(jnp.int32, (rows, S), 0) + r0
            c = lax.broadcasted_iota(jnp.int32, (rows, S), 1)
            before_ref[pl.ds(r0, rows), :] = jnp.where(r < c, 1.0, 0.0).astype(BF16)
            return carry
        lax.fori_loop(0, S // rows, fill, 0)

    lane = lax.broadcasted_iota(jnp.int32, (1, LANES), 1)
    lg = jnp.where(lane < n_experts, logit_ref[0], MASKED)
    ex = jnp.exp(lg - jnp.max(lg, axis=-1, keepdims=True))
    aff = (ex / jnp.sum(ex, axis=-1, keepdims=True)).T[:n_experts]

    def search(it, thr_bits):
        cand = thr_bits | jnp.left_shift(jnp.int32(1), 30 - it)
        cnt = jnp.sum((aff >= lax.bitcast_convert_type(cand, F32)).astype(jnp.int32), axis=-1, keepdims=True)
        return jnp.where(cnt >= cap, cand, thr_bits)

    floor = lax.bitcast_convert_type(lax.fori_loop(0, 31, search, jnp.zeros((n_experts, 1), jnp.int32)), F32)
    thr = jnp.min(jnp.where(aff >= floor, aff, jnp.inf), axis=-1, keepdims=True)
    above = aff > thr
    tied = aff == thr
    need = (cap - jnp.sum(above.astype(jnp.int32), axis=-1, keepdims=True)).astype(F32)
    count_before = lambda mask: jnp.dot(jnp.where(mask, 1.0, 0.0).astype(BF16), before_ref[...],
                                        preferred_element_type=F32)
    chosen = above | (tied & (count_before(tied) < need))
    pos = jnp.where(chosen, count_before(chosen), -1.0)
    pos_ref[0] = pos
    pos_t_ref[0] = jnp.concatenate([pos, jnp.full((LANES - n_experts, S), -1.0, F32)], axis=0).T
    aff_ref[0] = aff


def _route(logits, cap):
    B, S, _ = logits.shape
    E = N_EXPERTS
    return pl.pallas_call(
        functools.partial(_route_kernel, n_experts=E, cap=cap),
        grid=(B,),
        in_specs=[pl.BlockSpec((1, S, LANES), lambda b: (b, 0, 0))],
        out_specs=[pl.BlockSpec((1, E, S), lambda b: (b, 0, 0)), pl.BlockSpec((1, S, LANES), lambda b: (b, 0, 0)),
                   pl.BlockSpec((1, E, S), lambda b: (b, 0, 0))],
        out_shape=[jax.ShapeDtypeStruct((B, E, S), F32), jax.ShapeDtypeStruct((B, S, LANES), F32),
                   jax.ShapeDtypeStruct((B, E, S), F32)],
        scratch_shapes=[pltpu.VMEM((S, S), BF16)],
        compiler_params=_params("arbitrary"),
        name="route",
    )(logits)


def _slot_mask(pos_ref, cap):
    pos = pos_ref[0, 0].astype(jnp.int32)
    slot = lax.broadcasted_iota(jnp.int32, (cap, pos.shape[1]), 0)
    return slot == pos


def _gather_kernel(h_ref, pos_ref, aff_ref, x_ref, g_ref, *, cap):
    mask = _slot_mask(pos_ref, cap)
    onehot = jnp.where(mask, 1.0, 0.0).astype(BF16)
    x_ref[0, 0] = jnp.dot(onehot, h_ref[0], preferred_element_type=F32).astype(x_ref.dtype)
    g = jnp.sum(jnp.where(mask, aff_ref[0, 0], 0.0), axis=-1, keepdims=True)
    g_ref[0, 0] = jnp.broadcast_to(g, (cap, LANES))


def _gather(h, pos, aff, cap):
    B, S, D = h.shape
    E = pos.shape[1]
    row = pl.BlockSpec((1, 1, 1, S), lambda b, e: (b, e, 0, 0))
    return pl.pallas_call(
        functools.partial(_gather_kernel, cap=cap),
        grid=(B, E),
        in_specs=[pl.BlockSpec((1, S, D), lambda b, e: (b, 0, 0)), row, row],
        out_specs=[pl.BlockSpec((1, 1, cap, D), lambda b, e: (e, b, 0, 0)),
                   pl.BlockSpec((1, 1, cap, LANES), lambda b, e: (e, b, 0, 0))],
        out_shape=[jax.ShapeDtypeStruct((E, B, cap, D), BF16), jax.ShapeDtypeStruct((E, B, cap, LANES), F32)],
        compiler_params=_params("parallel", "arbitrary"),
        name="expert_gather",
    )(h, pos.reshape(B, E, 1, S), aff.reshape(B, E, 1, S))


def _expert_kernel(x_ref, g_ref, wg_ref, wu_ref, wd_ref, y_ref, act_ref, wd_bf_ref, *, nf, tf):
    f = pl.program_id(1)
    x = x_ref[0]
    a = jnp.dot(x, wg_ref[...].astype(BF16), preferred_element_type=F32)
    u = jnp.dot(x, wu_ref[...].astype(BF16), preferred_element_type=F32)
    act = (a / (1.0 + jnp.exp(-a)) * u).astype(BF16)
    wd_bf_ref[pl.ds(pl.multiple_of(f * tf, tf), tf), :] = wd_ref[...].astype(BF16)
    for k in range(nf):
        @pl.when(f == k)
        def _(k=k):
            act_ref[:, k * tf:(k + 1) * tf] = act

    @pl.when(f == nf - 1)
    def _():
        y = jnp.dot(act_ref[...], wd_bf_ref[...], preferred_element_type=F32)
        y_ref[0] = (y * g_ref[0][:, :1]).astype(y_ref.dtype)


def _experts(xin, g, w_gate, w_up, w_down, layer, tf):
    E, M, D = xin.shape
    FF = w_gate.shape[3]
    return pl.pallas_call(
        functools.partial(_expert_kernel, nf=FF // tf, tf=tf),
        grid=(E, FF // tf),
        in_specs=[pl.BlockSpec((1, M, D), lambda e, f: (e, 0, 0)),
                  pl.BlockSpec((1, M, LANES), lambda e, f: (e, 0, 0)),
                  pl.BlockSpec((None, None, D, tf), lambda e, f: (layer, e, 0, f)),
                  pl.BlockSpec((None, None, D, tf), lambda e, f: (layer, e, 0, f)),
                  pl.BlockSpec((None, None, tf, D), lambda e, f: (layer, e, f, 0))],
        out_specs=pl.BlockSpec((1, M, D), lambda e, f: (e, 0, 0)),
        out_shape=jax.ShapeDtypeStruct((E, M, D), BF16),
        scratch_shapes=[pltpu.VMEM((M, FF), BF16), pltpu.VMEM((FF, D), BF16)],
        compiler_params=_params("parallel", "arbitrary"),
        name="expert_swiglu",
    )(xin, g, w_gate, w_up, w_down)


def _scatter_kernel(x_ref, pos_t_ref, y_ref, o_ref, onehot_ref, *, n_experts, cap):
    @pl.when(pl.program_id(1) == 0)
    def _():
        pos = pos_t_ref[0].astype(jnp.int32)
        slot = lax.broadcasted_iota(jnp.int32, (pos.shape[0], cap), 1)
        for e in range(n_experts):
            onehot_ref[:, e * cap:(e + 1) * cap] = jnp.where(pos[:, e:e + 1] == slot, 1.0, 0.0).astype(BF16)

    y = y_ref[:, 0].reshape(n_experts * cap, y_ref.shape[-1])
    o_ref[0] = x_ref[0] + jnp.dot(onehot_ref[...], y, preferred_element_type=F32)


def _scatter(x, pos_t, y, cap, tn):
    B, S, D = x.shape
    E = y.shape[0]
    return pl.pallas_call(
        functools.partial(_scatter_kernel, n_experts=E, cap=cap),
        grid=(B, D // tn),
        in_specs=[pl.BlockSpec((1, S, tn), lambda b, n: (b, 0, n)),
                  pl.BlockSpec((1, S, LANES), lambda b, n: (b, 0, 0)),
                  pl.BlockSpec((E, 1, cap, tn), lambda b, n: (0, b, 0, n))],
        out_specs=pl.BlockSpec((1, S, tn), lambda b, n: (b, 0, n)),
        out_shape=jax.ShapeDtypeStruct((B, S, D), F32),
        scratch_shapes=[pltpu.VMEM((S, E * cap), BF16)],
        compiler_params=_params("parallel", "arbitrary"),
        name="expert_scatter",
    )(x, pos_t, y)


def _final_norm_kernel(x_ref, gain_ref, o_ref):
    o_ref[...] = _rms(x_ref[...], gain_ref[...])


def _final_norm(x2d, gain, tm):
    T, D = x2d.shape
    return pl.pallas_call(
        _final_norm_kernel,
        grid=(T // tm,),
        in_specs=[pl.BlockSpec((tm, D), lambda i: (i, 0)), pl.BlockSpec((1, D), lambda i: (0, 0))],
        out_specs=pl.BlockSpec((tm, D), lambda i: (i, 0)),
        out_shape=jax.ShapeDtypeStruct((T, D), F32),
        compiler_params=_params("parallel"),
        name="final_norm",
    )(x2d, gain.reshape(1, D))


def kernel(x, w_in, w_attn_out, w_ret_out, w_out, ret_decay_logit, norm_mix, norm_ffn, w_router, w_gate, w_up,
           w_down, norm_final):
    B, S, D = x.shape
    T = B * S
    depth = w_in.shape[0]
    W = ATTN_WIDTH
    attn_in = len(ATTN_PATTERNS) * 3 * W
    main_cols = attn_in + 3 * D
    gate_cols = 3 * D
    cap = CAPACITY_FACTOR * S // N_EXPERTS
    x2d = x.reshape(T, D)
    for layer in range(depth):
        w_in_b = w_in[layer].astype(BF16)
        proj = _norm_proj(x2d, norm_mix[layer], w_in_b, 0, main_cols, BF16, tm=2048, tn=1536)
        gates = _norm_proj(x2d, norm_mix[layer], w_in_b, main_cols, gate_cols, F32, tm=1024, tn=1536)
        proj3 = proj.reshape(B, S, main_cols)

        attn = _dilated_attention(proj3)
        ret = _retention(proj3, gates.reshape(B, S, gate_cols), ret_decay_logit[layer], attn_in)

        x2d, h2, logits = _merge(attn.reshape(T, W), ret.reshape(T, D), gates, x2d, w_attn_out[layer].astype(BF16),
                                 w_ret_out[layer].astype(BF16), w_out[layer].astype(BF16), norm_ffn[layer],
                                 w_router[layer], tm=512)

        pos, pos_t, aff = _route(logits.reshape(B, S, LANES), cap)
        xin, g = _gather(h2.reshape(B, S, D), pos, aff, cap)
        y = _experts(xin.reshape(N_EXPERTS, B * cap, D), g.reshape(N_EXPERTS, B * cap, LANES),
                     w_gate, w_up, w_down, layer, tf=256)
        x2d = _scatter(x2d.reshape(B, S, D), pos_t, y.reshape(N_EXPERTS, B, cap, D), cap, tn=512).reshape(T, D)
    return _final_norm(x2d, norm_final, tm=1024).reshape(B, S, D)
```

```python
import functools

import jax
import jax.numpy as jnp
from jax import lax
from jax.experimental import pallas as pl
from jax.experimental.pallas import tpu as pltpu

EPS = 1e-6
ATTN_PATTERNS = ((128, 1), (512, 4), (2048, 16))
ATTN_HEADS = 8
ATTN_HEAD_DIM = 64
ATTN_WIDTH = ATTN_HEADS * ATTN_HEAD_DIM
RET_HEADS = 4
RET_CHUNK = 256
N_EXPERTS = 16
CAPACITY_FACTOR = 2

LANES = 128
MASKED = -1e30
VMEM_LIMIT_BYTES = 56 * 1024 * 1024

F32 = jnp.float32
BF16 = jnp.bfloat16


def _params(*semantics):
    return pltpu.CompilerParams(dimension_semantics=semantics, vmem_limit_bytes=VMEM_LIMIT_BYTES)


def _sigmoid(x):
    return 0.5 * (1.0 + jnp.tanh(0.5 * x))


def _rms(x, gain):
    return x * lax.rsqrt(jnp.mean(x * x, axis=-1, keepdims=True) + EPS) * gain


def _norm_proj_kernel(x_ref, gain_ref, w_ref, o_ref, h_ref):
    @pl.when(pl.program_id(1) == 0)
    def _():
        h_ref[...] = _rms(x_ref[...], gain_ref[...]).astype(BF16)

    o_ref[...] = jnp.dot(h_ref[...], w_ref[...], preferred_element_type=F32).astype(o_ref.dtype)


def _norm_proj(x2d, gain, w, col0, ncols, out_dtype, tm, tn):
    T, D = x2d.shape
    assert T % tm == 0 and ncols % tn == 0 and col0 % tn == 0
    return pl.pallas_call(
        _norm_proj_kernel,
        grid=(T // tm, ncols // tn),
        in_specs=[pl.BlockSpec((tm, D), lambda i, j: (i, 0)),
                  pl.BlockSpec((1, D), lambda i, j: (0, 0)),
                  pl.BlockSpec((D, tn), lambda i, j: (0, j + col0 // tn))],
        out_specs=pl.BlockSpec((tm, tn), lambda i, j: (i, j)),
        out_shape=jax.ShapeDtypeStruct((T, ncols), out_dtype),
        scratch_shapes=[pltpu.VMEM((tm, D), BF16)],
        compiler_params=_params("parallel", "arbitrary"),
        name="norm_proj",
    )(x2d, gain.reshape(1, D), w)


ATTN_TQ = 128
ATTN_BLOCK = 1
ATTN_RING = 3


def _attn_kernel(*refs, S, patterns):
    n_groups = len(patterns)
    in_refs, o_ref = refs[:3 * n_groups], refs[3 * n_groups]
    stage_ref, qs_ref, ks_ref, vs_ref, acc_ref, den_ref, max_ref, bias_ref, s_ref, m_ref = refs[3 * n_groups + 1:]
    tq = ATTN_TQ
    n_tiles = S // tq
    hp = pl.program_id(0)
    lane = lax.broadcasted_iota(jnp.int32, (1, LANES), 1)
    first_head = lane < ATTN_HEAD_DIM
    geometry = []

    for gi, (window, d) in enumerate(patterns):
        half = window // (2 * d)
        tk = tq + 2 * half
        L = S // d
        seg = L + 2 * half
        geometry.append((d, half, tk, L // tq))
        q_in, k_in, v_in = in_refs[3 * gi:3 * gi + 3]

        @pl.when(pl.program_id(1) == 0)
        def _(gi=gi, d=d, half=half, tk=tk):
            r = lax.broadcasted_iota(jnp.int32, (tq, tk), 0)
            c = lax.broadcasted_iota(jnp.int32, (tq, tk), 1)
            rel = jnp.abs(c - half - r)
            dist = (d * rel).astype(F32)
            for hh in range(2):
                pow2 = jnp.full((tq, tk), jnp.left_shift(jnp.int32(2), 2 * hp + hh), jnp.int32).astype(F32)
                band = jnp.where(rel <= half, -(dist / pow2), MASKED)
                for kind in range(4):
                    tile_bias = band
                    if kind & 1:
                        tile_bias = jnp.where(c < half, MASKED, tile_bias)
                    if kind & 2:
                        tile_bias = jnp.where(c >= tq + half, MASKED, tile_bias)
                    bias_ref[gi, kind, hh * tq:(hh + 1) * tq, :] = tile_bias

        vs_ref[gi, :, LANES:] = jnp.ones((vs_ref.shape[1], LANES), BF16)
        zeros = jnp.zeros((half, LANES), BF16)
        if d == 1:
            ks_ref[gi, pl.ds(0, half), :] = zeros
            ks_ref[gi, pl.ds(half, S), :] = k_in[0]
            ks_ref[gi, pl.ds(half + S, half), :] = zeros
            vs_ref[gi, pl.ds(0, half), :LANES] = zeros
            vs_ref[gi, pl.ds(half, S), :LANES] = v_in[0]
            vs_ref[gi, pl.ds(half + S, half), :LANES] = zeros
        else:
            stage_ref[0] = q_in[0].astype(F32)
            stage_ref[1] = k_in[0].astype(F32)
            stage_ref[2] = v_in[0].astype(F32)

            def regroup(r, carry, gi=gi, d=d, L=L, seg=seg, half=half):
                rows = pl.ds(r, L, stride=d)
                k0 = pl.multiple_of(r * seg, half)
                qs_ref[gi, pl.ds(pl.multiple_of(r * L, tq), L), :] = stage_ref[0, rows, :].astype(BF16)
                ks_ref[gi, pl.ds(k0, half), :] = zeros
                ks_ref[gi, pl.ds(k0 + half, L), :] = stage_ref[1, rows, :].astype(BF16)
                ks_ref[gi, pl.ds(k0 + half + L, half), :] = zeros
                vs_ref[gi, pl.ds(k0, half), :LANES] = zeros
                vs_ref[gi, pl.ds(k0 + half, L), :LANES] = stage_ref[2, rows, :].astype(BF16)
                vs_ref[gi, pl.ds(k0 + half + L, half), :LANES] = zeros
                return carry

            lax.fori_loop(0, d, regroup, 0)

    def key_row(gi, u):
        d, half, tk, nblk = geometry[gi]
        return (u + u // nblk) * tq

    def scores(gi, u, slot):
        d, half, tk, nblk = geometry[gi]
        assert 2 * half == tq
        i = u % nblk
        kind = (1 if i == 0 else 0) + (2 if i == nblk - 1 else 0)
        qp = in_refs[3 * gi][0, u * tq:(u + 1) * tq, :] if d == 1 else qs_ref[gi, u * tq:(u + 1) * tq, :]
        qp = qp * (ATTN_HEAD_DIM ** -0.5)
        none = jnp.zeros_like(qp)
        q2 = jnp.concatenate([jnp.where(first_head, qp, none), jnp.where(first_head, none, qp)], axis=0)
        k0 = key_row(gi, u)
        s = lax.dot_general(q2, ks_ref[gi, k0:k0 + tk, :], (((1,), (1,)), ((), ())), preferred_element_type=F32)
        s_ref[slot] = s + bias_ref[gi, kind]

    def maxima(slot):
        m_ref[slot] = jnp.broadcast_to(jnp.max(s_ref[slot], axis=-1, keepdims=True), (2 * tq, LANES))

    def values(gi, u, slot):
        d, half, tk, nblk = geometry[gi]
        r, i = u // nblk, u % nblk
        m = m_ref[slot]
        p = jnp.concatenate([jnp.exp(s_ref[slot, :, c0:c0 + LANES] - m) for c0 in range(0, tk, LANES)], axis=1)
        k0 = key_row(gi, u)
        pv = jnp.dot(p.astype(BF16), vs_ref[gi, k0:k0 + tk, :], preferred_element_type=F32)
        rows = pl.ds(u * tq, tq) if d == 1 else pl.ds(i * (tq * d) + r, tq, stride=d)
        acc_ref[gi, rows, :] = jnp.where(first_head, pv[:tq, :LANES], pv[tq:, :LANES])
        den_ref[gi, rows, :] = jnp.where(first_head, pv[:tq, LANES:], pv[tq:, LANES:])
        max_ref[gi, rows, :] = jnp.where(first_head, m[:tq], m[tq:])

    blocks = [(gi, j) for gi in range(n_groups) for j in range(n_tiles // ATTN_BLOCK)]
    for step in range(len(blocks) + 2):
        for stage, fn in enumerate((scores, maxima, values)):
            t = step - stage
            if 0 <= t < len(blocks):
                gi, j = blocks[t]
                for k in range(ATTN_BLOCK):
                    slot = (t % ATTN_RING) * ATTN_BLOCK + k
                    if fn is maxima:
                        fn(slot)
                    else:
                        fn(gi, j * ATTN_BLOCK + k, slot)

    def merge(j, carry):
        rows = pl.ds(pl.multiple_of(j * tq, tq), tq)
        top = functools.reduce(jnp.maximum, [max_ref[g, rows, :] for g in range(n_groups)])
        w = [jnp.exp(max_ref[g, rows, :] - top) for g in range(n_groups)]
        num = sum(w[g] * acc_ref[g, rows, :] for g in range(n_groups))
        den = sum(w[g] * den_ref[g, rows, :] for g in range(n_groups))
        o_ref[0, rows, :] = (num / den).astype(o_ref.dtype)
        return carry

    lax.fori_loop(0, n_tiles, merge, 0)


def _dilated_attention(proj):
    B, S, _ = proj.shape
    W = ATTN_WIDTH
    n_groups = len(ATTN_PATTERNS)
    halves = {w // (2 * d) for w, d in ATTN_PATTERNS}
    assert len(halves) == 1 and all(S % (d * ATTN_TQ) == 0 for _, d in ATTN_PATTERNS)
    half = halves.pop()
    tk = ATTN_TQ + 2 * half
    assert tk % LANES == 0 and (S // ATTN_TQ) % ATTN_BLOCK == 0
    pairs = W // LANES
    col = lambda c0: pl.BlockSpec((1, S, LANES), lambda hp, b: (b, 0, c0 // LANES + hp))
    return pl.pallas_call(
        functools.partial(_attn_kernel, S=S, patterns=ATTN_PATTERNS),
        grid=(pairs, B),
        in_specs=[col((3 * g + t) * W) for g in range(n_groups) for t in range(3)],
        out_specs=pl.BlockSpec((1, S, LANES), lambda hp, b: (b, 0, hp)),
        out_shape=jax.ShapeDtypeStruct((B, S, W), BF16),
        scratch_shapes=[pltpu.VMEM((3, S, LANES), F32),
                        pltpu.VMEM((n_groups, S, LANES), BF16),
                        pltpu.VMEM((n_groups, 2 * S, LANES), BF16),
                        pltpu.VMEM((n_groups, 2 * S, 2 * LANES), BF16),
                        pltpu.VMEM((n_groups, S, LANES), F32),
                        pltpu.VMEM((n_groups, S, LANES), F32),
                        pltpu.VMEM((n_groups, S, LANES), F32),
                        pltpu.VMEM((n_groups, 4, 2 * ATTN_TQ, tk), F32),
                        pltpu.VMEM((ATTN_RING * ATTN_BLOCK, 2 * ATTN_TQ, tk), F32),
                        pltpu.VMEM((ATTN_RING * ATTN_BLOCK, 2 * ATTN_TQ, LANES), F32)],
        compiler_params=_params("arbitrary", "arbitrary"),
        name="dilated_attention",
    )(*([proj] * (3 * n_groups)))


def _log_sigmoid(x):
    return jnp.minimum(x, 0.0) - jnp.log1p(jnp.exp(-jnp.abs(x)))


def _retention_kernel(logit_ref, q_ref, k_ref, v_ref, gate_ref, o_ref, yf_ref, yb_ref, sf_ref, sb_ref):
    C = RET_CHUNK
    S, Dk = q_ref.shape[1], q_ref.shape[2]
    N = S // C
    h = pl.program_id(1)
    lf = _log_sigmoid(jnp.full((C, 1), logit_ref[0, h], F32))
    lb = _log_sigmoid(jnp.full((C, 1), logit_ref[1, h], F32))
    rel = (lax.broadcasted_iota(jnp.int32, (C, C), 0) - lax.broadcasted_iota(jnp.int32, (C, C), 1)).astype(F32)
    decay = jnp.exp(jnp.where(rel >= 0, lf * rel, -lb * rel))
    pos = lax.broadcasted_iota(jnp.int32, (C, 1), 0).astype(F32)
    xi_f = jnp.exp(lf * (pos + 1.0))
    zeta_f = jnp.exp(lf * (C - 1.0 - pos))
    xi_b = jnp.exp(lb * (C - pos))
    zeta_b = jnp.exp(lb * pos)
    chunk_f = jnp.exp(_log_sigmoid(jnp.full((Dk, 1), logit_ref[0, h], F32)) * C)
    chunk_b = jnp.exp(_log_sigmoid(jnp.full((Dk, 1), logit_ref[1, h], F32)) * C)
    scale = Dk ** -0.5

    def load(i):
        r0 = pl.multiple_of(i * C, C)
        return r0, q_ref[0, pl.ds(r0, C), :] * scale, k_ref[0, pl.ds(r0, C), :], v_ref[0, pl.ds(r0, C), :]

    def cross(state_ref, qc, kc, vc, xi, zeta, chunk_decay):
        st = state_ref[...]
        y = jnp.dot(qc, st.astype(BF16), preferred_element_type=F32) * xi
        kz = (kc.astype(F32) * zeta).astype(BF16)
        u = lax.dot_general(kz, vc, (((0,), (0,)), ((), ())), preferred_element_type=F32)
        state_ref[...] = st * chunk_decay + u
        return y

    sf_ref[...] = jnp.zeros_like(sf_ref)
    sb_ref[...] = jnp.zeros_like(sb_ref)

    def step(i, carry):
        r0, qc, kc, vc = load(i)
        s = lax.dot_general(qc, kc, (((1,), (1,)), ((), ())), preferred_element_type=F32) * decay
        y = jnp.dot(s.astype(BF16), vc, preferred_element_type=F32)
        yf_ref[pl.ds(r0, C), :] = y + cross(sf_ref, qc, kc, vc, xi_f, zeta_f, chunk_f)
        r1, qd, kd, vd = load(N - 1 - i)
        yb_ref[pl.ds(r1, C), :] = cross(sb_ref, qd, kd, vd, xi_b, zeta_b, chunk_b)
        return carry

    lax.fori_loop(0, N, step, 0, unroll=2)

    def finish(i, carry):
        rows = pl.ds(pl.multiple_of(i * C, C), C)
        r = yf_ref[rows, :] + yb_ref[rows, :]
        r = r * lax.rsqrt(jnp.mean(r * r, axis=-1, keepdims=True) + EPS)
        g = gate_ref[0, rows, :]
        o_ref[0, rows, :] = (g * _sigmoid(g) * r).astype(o_ref.dtype)
        return carry

    lax.fori_loop(0, N, finish, 0)


def _retention(proj, gates, decay_logit, q_col0):
    B, S, _ = proj.shape
    Dk = gates.shape[2] // 3 // RET_HEADS
    qb = q_col0 // Dk
    nb = RET_HEADS
    return pl.pallas_call(
        _retention_kernel,
        grid=(B, RET_HEADS),
        in_specs=[pl.BlockSpec(memory_space=pltpu.SMEM),
                  pl.BlockSpec((1, S, Dk), lambda b, h: (b, 0, qb + h)),
                  pl.BlockSpec((1, S, Dk), lambda b, h: (b, 0, qb + nb + h)),
                  pl.BlockSpec((1, S, Dk), lambda b, h: (b, 0, qb + 2 * nb + h)),
                  pl.BlockSpec((1, S, Dk), lambda b, h: (b, 0, h))],
        out_specs=pl.BlockSpec((1, S, Dk), lambda b, h: (b, 0, h)),
        out_shape=jax.ShapeDtypeStruct((B, S, RET_HEADS * Dk), BF16),
        scratch_shapes=[pltpu.VMEM((S, Dk), F32), pltpu.VMEM((S, Dk), F32),
                        pltpu.VMEM((Dk, Dk), F32), pltpu.VMEM((Dk, Dk), F32)],
        compiler_params=_params("parallel", "parallel"),
        name="retention",
    )(decay_logit, proj, proj, proj, gates)


def _merge_kernel(attn_ref, ret_ref, ga_ref, gr_ref, x_ref, wa_ref, wr_ref, wo_ref, gain_ref, wrt_ref,
                  wrt_lo_ref, xo_ref, h_ref, logit_ref):
    pa = jnp.dot(attn_ref[...], wa_ref[...], preferred_element_type=F32)
    pr = jnp.dot(ret_ref[...], wr_ref[...], preferred_element_type=F32)
    ga, gr = ga_ref[...], gr_ref[...]
    merged = pa * _sigmoid(ga) + pr * _sigmoid(gr)
    x = x_ref[...] + jnp.dot(merged.astype(BF16), wo_ref[...], preferred_element_type=F32)
    xo_ref[...] = x
    h = _rms(x, gain_ref[...])
    h_hi = h.astype(BF16)
    h_ref[...] = h_hi
    h_lo = (h - h_hi.astype(F32)).astype(BF16)
    logit_ref[...] = (jnp.dot(h_hi, wrt_ref[...], preferred_element_type=F32)
                      + jnp.dot(h_lo, wrt_ref[...], preferred_element_type=F32)
                      + jnp.dot(h_hi, wrt_lo_ref[...], preferred_element_type=F32))


def _merge(attn, ret, gates, x2d, wa, wr, wo, gain, w_router, tm):
    T, D = x2d.shape
    w_router_pad = jnp.pad(w_router, ((0, 0), (0, LANES - w_router.shape[1])))
    wrt_hi = w_router_pad.astype(BF16)
    wrt_lo = (w_router_pad - wrt_hi.astype(F32)).astype(BF16)
    tok = lambda width, col=0: pl.BlockSpec((tm, width), lambda i: (i, col))
    full = lambda a: pl.BlockSpec(a.shape, lambda i: (0, 0))
    return pl.pallas_call(
        _merge_kernel,
        grid=(T // tm,),
        in_specs=[tok(ATTN_WIDTH), tok(D), tok(D, 1), tok(D, 2), tok(D),
                  full(wa), full(wr), full(wo), pl.BlockSpec((1, D), lambda i: (0, 0)), full(wrt_hi), full(wrt_lo)],
        out_specs=[tok(D), tok(D), tok(LANES)],
        out_shape=[jax.ShapeDtypeStruct((T, D), F32), jax.ShapeDtypeStruct((T, D), BF16),
                   jax.ShapeDtypeStruct((T, LANES), F32)],
        compiler_params=_params("parallel"),
        name="merge_out_proj",
    )(attn, ret, gates, gates, x2d, wa, wr, wo, gain.reshape(1, D), wrt_hi, wrt_lo)


def _route_kernel(logit_ref, pos_ref, pos_t_ref, aff_ref, before_ref, *, n_experts, cap):
    S = logit_ref.shape[1]
    rows = 256

    @pl.when(pl.program_id(0) == 0)
    def _():
        def fill(i, carry):
            r0 = pl.multiple_of(i * rows, rows)
            r = lax.broadcasted_iota(jnp.int32, (rows, S), 0) + r0
            c = lax.broadcasted_iota(jnp.int32, (rows, S), 1)
            before_ref[pl.ds(r0, rows), :] = jnp.where(r < c, 1.0, 0.0).astype(BF16)
            return carry
        lax.fori_loop(0, S // rows, fill, 0)

    lane = lax.broadcasted_iota(jnp.int32, (1, LANES), 1)
    lg = jnp.where(lane < n_experts, logit_ref[0], MASKED)
    ex = jnp.exp(lg - jnp.max(lg, axis=-1, keepdims=True))
    aff = (ex / jnp.sum(ex, axis=-1, keepdims=True)).T[:n_experts]

    def search(it, thr_bits):
        cand = thr_bits | jnp.left_shift(jnp.int32(1), 30 - it)
        cnt = jnp.sum((aff >= lax.bitcast_convert_type(cand, F32)).astype(jnp.int32), axis=-1, keepdims=True)
        return jnp.where(cnt >= cap, cand, thr_bits)

    floor = lax.bitcast_convert_type(lax.fori_loop(0, 31, search, jnp.zeros((n_experts, 1), jnp.int32)), F32)
    thr = jnp.min(jnp.where(aff >= floor, aff, jnp.inf), axis=-1, keepdims=True)
    above = aff > thr
    tied = aff == thr
    need = (cap - jnp.sum(above.astype(jnp.int32), axis=-1, keepdims=True)).astype(F32)
    count_before = lambda mask: jnp.dot(jnp.where(mask, 1.0, 0.0).astype(BF16), before_ref[...],
                                        preferred_element_type=F32)
    chosen = above | (tied & (count_before(tied) < need))
    pos = jnp.where(chosen, count_before(chosen), -1.0)
    pos_ref[0] = pos
    pos_t_ref[0] = jnp.concatenate([pos, jnp.full((LANES - n_experts, S), -1.0, F32)], axis=0).T
    aff_ref[0] = aff


def _route(logits, cap):
    B, S, _ = logits.shape
    E = N_EXPERTS
    return pl.pallas_call(
        functools.partial(_route_kernel, n_experts=E, cap=cap),
        grid=(B,),
        in_specs=[pl.BlockSpec((1, S, LANES), lambda b: (b, 0, 0))],
        out_specs=[pl.BlockSpec((1, E, S), lambda b: (b, 0, 0)), pl.BlockSpec((1, S, LANES), lambda b: (b, 0, 0)),
                   pl.BlockSpec((1, E, S), lambda b: (b, 0, 0))],
        out_shape=[jax.ShapeDtypeStruct((B, E, S), F32), jax.ShapeDtypeStruct((B, S, LANES), F32),
                   jax.ShapeDtypeStruct((B, E, S), F32)],
        scratch_shapes=[pltpu.VMEM((S, S), BF16)],
        compiler_params=_params("arbitrary"),
        name="route",
    )(logits)


def _slot_mask(pos_ref, cap):
    pos = pos_ref[0, 0].astype(jnp.int32)
    slot = lax.broadcasted_iota(jnp.int32, (cap, pos.shape[1]), 0)
    return slot == pos


def _gather_kernel(h_ref, pos_ref, aff_ref, x_ref, g_ref, *, cap):
    mask = _slot_mask(pos_ref, cap)
    onehot = jnp.where(mask, 1.0, 0.0).astype(BF16)
    x_ref[0, 0] = jnp.dot(onehot, h_ref[0], preferred_element_type=F32).astype(x_ref.dtype)
    g = jnp.sum(jnp.where(mask, aff_ref[0, 0], 0.0), axis=-1, keepdims=True)
    g_ref[0, 0] = jnp.broadcast_to(g, (cap, LANES))


def _gather(h, pos, aff, cap):
    B, S, D = h.shape
    E = pos.shape[1]
    row = pl.BlockSpec((1, 1, 1, S), lambda b, e: (b, e, 0, 0))
    return pl.pallas_call(
        functools.partial(_gather_kernel, cap=cap),
        grid=(B, E),
        in_specs=[pl.BlockSpec((1, S, D), lambda b, e: (b, 0, 0)), row, row],
        out_specs=[pl.BlockSpec((1, 1, cap, D), lambda b, e: (e, b, 0, 0)),
                   pl.BlockSpec((1, 1, cap, LANES), lambda b, e: (e, b, 0, 0))],
        out_shape=[jax.ShapeDtypeStruct((E, B, cap, D), BF16), jax.ShapeDtypeStruct((E, B, cap, LANES), F32)],
        compiler_params=_params("parallel", "arbitrary"),
        name="expert_gather",
    )(h, pos.reshape(B, E, 1, S), aff.reshape(B, E, 1, S))


def _expert_kernel(x_ref, g_ref, wg_ref, wu_ref, wd_ref, y_ref, act_ref, wd_bf_ref, *, nf, tf):
    f = pl.program_id(1)
    x = x_ref[0]
    a = jnp.dot(x, wg_ref[...].astype(BF16), preferred_element_type=F32)
    u = jnp.dot(x, wu_ref[...].astype(BF16), preferred_element_type=F32)
    act = (a * _sigmoid(a) * u).astype(BF16)
    wd_bf_ref[pl.ds(pl.multiple_of(f * tf, tf), tf), :] = wd_ref[...].astype(BF16)
    for k in range(nf):
        @pl.when(f == k)
        def _(k=k):
            act_ref[:, k * tf:(k + 1) * tf] = act

    @pl.when(f == nf - 1)
    def _():
        y = jnp.dot(act_ref[...], wd_bf_ref[...], preferred_element_type=F32)
        y_ref[0] = (y * g_ref[0][:, :1]).astype(y_ref.dtype)


def _experts(xin, g, w_gate, w_up, w_down, layer, tf):
    E, M, D = xin.shape
    FF = w_gate.shape[3]
    return pl.pallas_call(
        functools.partial(_expert_kernel, nf=FF // tf, tf=tf),
        grid=(E, FF // tf),
        in_specs=[pl.BlockSpec((1, M, D), lambda e, f: (e, 0, 0)),
                  pl.BlockSpec((1, M, LANES), lambda e, f: (e, 0, 0)),
                  pl.BlockSpec((None, None, D, tf), lambda e, f: (layer, e, 0, f)),
                  pl.BlockSpec((None, None, D, tf), lambda e, f: (layer, e, 0, f)),
                  pl.BlockSpec((None, None, tf, D), lambda e, f: (layer, e, f, 0))],
        out_specs=pl.BlockSpec((1, M, D), lambda e, f: (e, 0, 0)),
        out_shape=jax.ShapeDtypeStruct((E, M, D), BF16),
        scratch_shapes=[pltpu.VMEM((M, FF), BF16), pltpu.VMEM((FF, D), BF16)],
        compiler_params=_params("parallel", "arbitrary"),
        name="expert_swiglu",
    )(xin, g, w_gate, w_up, w_down)


def _scatter_kernel(x_ref, pos_t_ref, y_ref, o_ref, onehot_ref, *, n_experts, cap):
    @pl.when(pl.program_id(1) == 0)
    def _():
        pos = pos_t_ref[0].astype(jnp.int32)
        slot = lax.broadcasted_iota(jnp.int32, (pos.shape[0], cap), 1)
        for e in range(n_experts):
            onehot_ref[:, e * cap:(e + 1) * cap] = jnp.where(pos[:, e:e + 1] == slot, 1.0, 0.0).astype(BF16)

    y = y_ref[:, 0].reshape(n_experts * cap, y_ref.shape[-1])
    o_ref[0] = x_ref[0] + jnp.dot(onehot_ref[...], y, preferred_element_type=F32)


def _scatter(x, pos_t, y, cap, tn):
    B, S, D = x.shape
    E = y.shape[0]
    return pl.pallas_call(
        functools.partial(_scatter_kernel, n_experts=E, cap=cap),
        grid=(B, D // tn),
        in_specs=[pl.BlockSpec((1, S, tn), lambda b, n: (b, 0, n)),
                  pl.BlockSpec((1, S, LANES), lambda b, n: (b, 0, 0)),
                  pl.BlockSpec((E, 1, cap, tn), lambda b, n: (0, b, 0, n))],
        out_specs=pl.BlockSpec((1, S, tn), lambda b, n: (b, 0, n)),
        out_shape=jax.ShapeDtypeStruct((B, S, D), F32),
        scratch_shapes=[pltpu.VMEM((S, E * cap), BF16)],
        compiler_params=_params("parallel", "arbitrary"),
        name="expert_scatter",
    )(x, pos_t, y)


def _final_norm_kernel(x_ref, gain_ref, o_ref):
    o_ref[...] = _rms(x_ref[...], gain_ref[...])


def _final_norm(x2d, gain, tm):
    T, D = x2d.shape
    return pl.pallas_call(
        _final_norm_kernel,
        grid=(T // tm,),
        in_specs=[pl.BlockSpec((tm, D), lambda i: (i, 0)), pl.BlockSpec((1, D), lambda i: (0, 0))],
        out_specs=pl.BlockSpec((tm, D), lambda i: (i, 0)),
        out_shape=jax.ShapeDtypeStruct((T, D), F32),
        compiler_params=_params("parallel"),
        name="final_norm",
    )(x2d, gain.reshape(1, D))


def kernel(x, w_in, w_attn_out, w_ret_out, w_out, ret_decay_logit, norm_mix, norm_ffn, w_router, w_gate, w_up,
           w_down, norm_final):
    B, S, D = x.shape
    T = B * S
    depth = w_in.shape[0]
    W = ATTN_WIDTH
    attn_in = len(ATTN_PATTERNS) * 3 * W
    main_cols = attn_in + 3 * D
    gate_cols = 3 * D
    cap = CAPACITY_FACTOR * S // N_EXPERTS
    x2d = x.reshape(T, D)
    for layer in range(depth):
        w_in_b = w_in[layer].astype(BF16)
        proj = _norm_proj(x2d, norm_mix[layer], w_in_b, 0, main_cols, BF16, tm=2048, tn=1536)
        gates = _norm_proj(x2d, norm_mix[layer], w_in_b, main_cols, gate_cols, F32, tm=1024, tn=1536)
        proj3 = proj.reshape(B, S, main_cols)

        attn = _dilated_attention(proj3)
        ret = _retention(proj3, gates.reshape(B, S, gate_cols), ret_decay_logit[layer], attn_in)

        x2d, h2, logits = _merge(attn.reshape(T, W), ret.reshape(T, D), gates, x2d, w_attn_out[layer].astype(BF16),
                                 w_ret_out[layer].astype(BF16), w_out[layer].astype(BF16), norm_ffn[layer],
                                 w_router[layer], tm=512)

        pos, pos_t, aff = _route(logits.reshape(B, S, LANES), cap)
        xin, g = _gather(h2.reshape(B, S, D), pos, aff, cap)
        y = _experts(xin.reshape(N_EXPERTS, B * cap, D), g.reshape(N_EXPERTS, B * cap, LANES),
                     w_gate, w_up, w_down, layer, tf=256)
        x2d = _scatter(x2d.reshape(B, S, D), pos_t, y.reshape(N_EXPERTS, B, cap, D), cap, tn=512).reshape(T, D)
    return _final_norm(x2d, norm_final, tm=1024).reshape(B, S, D)
```

```python
import functools

import jax
import jax.numpy as jnp
from jax import lax
from jax.experimental import pallas as pl
from jax.experimental.pallas import tpu as pltpu

EPS = 1e-6
ATTN_PATTERNS = ((128, 1), (512, 4), (2048, 16))
ATTN_HEADS = 8
ATTN_HEAD_DIM = 64
ATTN_WIDTH = ATTN_HEADS * ATTN_HEAD_DIM
RET_HEADS = 4
RET_CHUNK = 256
N_EXPERTS = 16
CAPACITY_FACTOR = 2

LANES = 128
MASKED = -1e30
VMEM_LIMIT_BYTES = 56 * 1024 * 1024

F32 = jnp.float32
BF16 = jnp.bfloat16


def _params(*semantics):
    return pltpu.CompilerParams(dimension_semantics=semantics, vmem_limit_bytes=VMEM_LIMIT_BYTES)


def _sigmoid(x):
    return 0.5 * (1.0 + jnp.tanh(0.5 * x))


def _rms(x, gain):
    return x * lax.rsqrt(jnp.mean(x * x, axis=-1, keepdims=True) + EPS) * gain


def _norm_proj_kernel(x_ref, gain_ref, w_ref, o_ref, h_ref):
    @pl.when(pl.program_id(1) == 0)
    def _():
        h_ref[...] = _rms(x_ref[...], gain_ref[...]).astype(BF16)

    o_ref[...] = jnp.dot(h_ref[...], w_ref[...], preferred_element_type=F32).astype(o_ref.dtype)


def _norm_proj(x2d, gain, w, col0, ncols, out_dtype, tm, tn):
    T, D = x2d.shape
    assert T % tm == 0 and ncols % tn == 0 and col0 % tn == 0
    return pl.pallas_call(
        _norm_proj_kernel,
        grid=(T // tm, ncols // tn),
        in_specs=[pl.BlockSpec((tm, D), lambda i, j: (i, 0)),
                  pl.BlockSpec((1, D), lambda i, j: (0, 0)),
                  pl.BlockSpec((D, tn), lambda i, j: (0, j + col0 // tn))],
        out_specs=pl.BlockSpec((tm, tn), lambda i, j: (i, j)),
        out_shape=jax.ShapeDtypeStruct((T, ncols), out_dtype),
        scratch_shapes=[pltpu.VMEM((tm, D), BF16)],
        compiler_params=_params("parallel", "arbitrary"),
        name="norm_proj",
    )(x2d, gain.reshape(1, D), w)


ATTN_TQ = 128
ATTN_BLOCK = 1
ATTN_RING = 3


def _attn_kernel(*refs, S, patterns):
    n_groups = len(patterns)
    in_refs, o_ref = refs[:3 * n_groups], refs[3 * n_groups]
    stage_ref, qs_ref, ks_ref, vs_ref, acc_ref, den_ref, max_ref, bias_ref, s_ref, m_ref = refs[3 * n_groups + 1:]
    tq = ATTN_TQ
    n_tiles = S // tq
    hp = pl.program_id(0)
    lane = lax.broadcasted_iota(jnp.int32, (1, LANES), 1)
    first_head = lane < ATTN_HEAD_DIM
    geometry = []

    for gi, (window, d) in enumerate(patterns):
        half = window // (2 * d)
        tk = tq + 2 * half
        L = S // d
        seg = L + 2 * half
        geometry.append((d, half, tk, L // tq))
        q_in, k_in, v_in = in_refs[3 * gi:3 * gi + 3]

        @pl.when(pl.program_id(1) == 0)
        def _(gi=gi, d=d, half=half, tk=tk):
            r = lax.broadcasted_iota(jnp.int32, (tq, tk), 0)
            c = lax.broadcasted_iota(jnp.int32, (tq, tk), 1)
            rel = jnp.abs(c - half - r)
            dist = (d * rel).astype(F32)
            for hh in range(2):
                pow2 = jnp.full((tq, tk), jnp.left_shift(jnp.int32(2), 2 * hp + hh), jnp.int32).astype(F32)
                band = jnp.where(rel <= half, -(dist / pow2), MASKED)
                for kind in range(4):
                    tile_bias = band
                    if kind & 1:
                        tile_bias = jnp.where(c < half, MASKED, tile_bias)
                    if kind & 2:
                        tile_bias = jnp.where(c >= tq + half, MASKED, tile_bias)
                    bias_ref[gi, kind, hh * tq:(hh + 1) * tq, :] = tile_bias

        vs_ref[gi, :, LANES:] = jnp.ones((vs_ref.shape[1], LANES), BF16)
        zeros = jnp.zeros((half, LANES), BF16)
        if d == 1:
            ks_ref[gi, pl.ds(0, half), :] = zeros
            ks_ref[gi, pl.ds(half, S), :] = k_in[0]
            ks_ref[gi, pl.ds(half + S, half), :] = zeros
            vs_ref[gi, pl.ds(0, half), :LANES] = zeros
            vs_ref[gi, pl.ds(half, S), :LANES] = v_in[0]
            vs_ref[gi, pl.ds(half + S, half), :LANES] = zeros
        else:
            stage_ref[0] = q_in[0].astype(F32)
            stage_ref[1] = k_in[0].astype(F32)
            stage_ref[2] = v_in[0].astype(F32)

            def regroup(r, carry, gi=gi, d=d, L=L, seg=seg, half=half):
                rows = pl.ds(r, L, stride=d)
                k0 = pl.multiple_of(r * seg, half)
                qs_ref[gi, pl.ds(pl.multiple_of(r * L, tq), L), :] = stage_ref[0, rows, :].astype(BF16)
                ks_ref[gi, pl.ds(k0, half), :] = zeros
                ks_ref[gi, pl.ds(k0 + half, L), :] = stage_ref[1, rows, :].astype(BF16)
                ks_ref[gi, pl.ds(k0 + half + L, half), :] = zeros
                vs_ref[gi, pl.ds(k0, half), :LANES] = zeros
                vs_ref[gi, pl.ds(k0 + half, L), :LANES] = stage_ref[2, rows, :].astype(BF16)
                vs_ref[gi, pl.ds(k0 + half + L, half), :LANES] = zeros
                return carry

            lax.fori_loop(0, d, regroup, 0)

    def key_row(gi, u):
        d, half, tk, nblk = geometry[gi]
        return (u + u // nblk) * tq

    def scores(gi, u, slot):
        d, half, tk, nblk = geometry[gi]
        assert 2 * half == tq
        i = u % nblk
        kind = (1 if i == 0 else 0) + (2 if i == nblk - 1 else 0)
        qp = in_refs[3 * gi][0, u * tq:(u + 1) * tq, :] if d == 1 else qs_ref[gi, u * tq:(u + 1) * tq, :]
        qp = qp * (ATTN_HEAD_DIM ** -0.5)
        none = jnp.zeros_like(qp)
        q2 = jnp.concatenate([jnp.where(first_head, qp, none), jnp.where(first_head, none, qp)], axis=0)
        k0 = key_row(gi, u)
        s = lax.dot_general(q2, ks_ref[gi, k0:k0 + tk, :], (((1,), (1,)), ((), ())), preferred_element_type=F32)
        s_ref[slot] = s + bias_ref[gi, kind]

    def maxima(slot):
        m_ref[slot] = jnp.broadcast_to(jnp.max(s_ref[slot], axis=-1, keepdims=True), (2 * tq, LANES))

    def values(gi, u, slot):
        d, half, tk, nblk = geometry[gi]
        r, i = u // nblk, u % nblk
        m = m_ref[slot]
        p = jnp.concatenate([jnp.exp(s_ref[slot, :, c0:c0 + LANES] - m) for c0 in range(0, tk, LANES)], axis=1)
        k0 = key_row(gi, u)
        pv = jnp.dot(p.astype(BF16), vs_ref[gi, k0:k0 + tk, :], preferred_element_type=F32)
        rows = pl.ds(u * tq, tq) if d == 1 else pl.ds(i * (tq * d) + r, tq, stride=d)
        acc_ref[gi, rows, :] = jnp.where(first_head, pv[:tq, :LANES], pv[tq:, :LANES])
        den_ref[gi, rows, :] = jnp.where(first_head, pv[:tq, LANES:], pv[tq:, LANES:])
        max_ref[gi, rows, :] = jnp.where(first_head, m[:tq], m[tq:])

    blocks = [(gi, j) for gi in range(n_groups) for j in range(n_tiles // ATTN_BLOCK)]
    for step in range(len(blocks) + 2):
        for stage, fn in enumerate((scores, maxima, values)):
            t = step - stage
            if 0 <= t < len(blocks):
                gi, j = blocks[t]
                for k in range(ATTN_BLOCK):
                    slot = (t % ATTN_RING) * ATTN_BLOCK + k
                    if fn is maxima:
                        fn(slot)
                    else:
                        fn(gi, j * ATTN_BLOCK + k, slot)

    def merge(j, carry):
        rows = pl.ds(pl.multiple_of(j * tq, tq), tq)
        top = functools.reduce(jnp.maximum, [max_ref[g, rows, :] for g in range(n_groups)])
        w = [jnp.exp(max_ref[g, rows, :] - top) for g in range(n_groups)]
        num = sum(w[g] * acc_ref[g, rows, :] for g in range(n_groups))
        den = sum(w[g] * den_ref[g, rows, :] for g in range(n_groups))
        o_ref[0, rows, :] = (num / den).astype(o_ref.dtype)
        return carry

    lax.fori_loop(0, n_tiles, merge, 0)


def _dilated_attention(proj, col0):
    B, S, _ = proj.shape
    W = ATTN_WIDTH
    n_groups = len(ATTN_PATTERNS)
    halves = {w // (2 * d) for w, d in ATTN_PATTERNS}
    assert len(halves) == 1 and all(S % (d * ATTN_TQ) == 0 for _, d in ATTN_PATTERNS)
    half = halves.pop()
    tk = ATTN_TQ + 2 * half
    assert tk % LANES == 0 and (S // ATTN_TQ) % ATTN_BLOCK == 0
    pairs = W // LANES
    col = lambda c0: pl.BlockSpec((1, S, LANES), lambda hp, b: (b, 0, c0 // LANES + hp))
    return pl.pallas_call(
        functools.partial(_attn_kernel, S=S, patterns=ATTN_PATTERNS),
        grid=(pairs, B),
        in_specs=[col(col0 + (3 * g + t) * W) for g in range(n_groups) for t in range(3)],
        out_specs=pl.BlockSpec((1, S, LANES), lambda hp, b: (b, 0, hp)),
        out_shape=jax.ShapeDtypeStruct((B, S, W), BF16),
        scratch_shapes=[pltpu.VMEM((3, S, LANES), F32),
                        pltpu.VMEM((n_groups, S, LANES), BF16),
                        pltpu.VMEM((n_groups, 2 * S, LANES), BF16),
                        pltpu.VMEM((n_groups, 2 * S, 2 * LANES), BF16),
                        pltpu.VMEM((n_groups, S, LANES), F32),
                        pltpu.VMEM((n_groups, S, LANES), F32),
                        pltpu.VMEM((n_groups, S, LANES), F32),
                        pltpu.VMEM((n_groups, 4, 2 * ATTN_TQ, tk), F32),
                        pltpu.VMEM((ATTN_RING * ATTN_BLOCK, 2 * ATTN_TQ, tk), F32),
                        pltpu.VMEM((ATTN_RING * ATTN_BLOCK, 2 * ATTN_TQ, LANES), F32)],
        compiler_params=_params("arbitrary", "arbitrary"),
        name="dilated_attention",
    )(*([proj] * (3 * n_groups)))


def _log_sigmoid(x):
    return jnp.minimum(x, 0.0) - jnp.log1p(jnp.exp(-jnp.abs(x)))


def _retention_kernel(logit_ref, q_ref, k_ref, v_ref, gate_ref, o_ref, yf_ref, yb_ref, sf_ref, sb_ref):
    C = RET_CHUNK
    S, Dk = q_ref.shape[1], q_ref.shape[2]
    N = S // C
    h = pl.program_id(1)
    lf = _log_sigmoid(jnp.full((C, 1), logit_ref[0, h], F32))
    lb = _log_sigmoid(jnp.full((C, 1), logit_ref[1, h], F32))
    rel = (lax.broadcasted_iota(jnp.int32, (C, C), 0) - lax.broadcasted_iota(jnp.int32, (C, C), 1)).astype(F32)
    decay = jnp.exp(jnp.where(rel >= 0, lf * rel, -lb * rel))
    pos = lax.broadcasted_iota(jnp.int32, (C, 1), 0).astype(F32)
    xi_f = jnp.exp(lf * (pos + 1.0))
    zeta_f = jnp.exp(lf * (C - 1.0 - pos))
    xi_b = jnp.exp(lb * (C - pos))
    zeta_b = jnp.exp(lb * pos)
    chunk_f = jnp.exp(_log_sigmoid(jnp.full((Dk, 1), logit_ref[0, h], F32)) * C)
    chunk_b = jnp.exp(_log_sigmoid(jnp.full((Dk, 1), logit_ref[1, h], F32)) * C)
    scale = Dk ** -0.5

    def load(i):
        r0 = pl.multiple_of(i * C, C)
        return r0, q_ref[0, pl.ds(r0, C), :] * scale, k_ref[0, pl.ds(r0, C), :], v_ref[0, pl.ds(r0, C), :]

    def cross(state_ref, qc, kc, vc, xi, zeta, chunk_decay):
        st = state_ref[...]
        y = jnp.dot(qc, st.astype(BF16), preferred_element_type=F32) * xi
        kz = (kc.astype(F32) * zeta).astype(BF16)
        u = lax.dot_general(kz, vc, (((0,), (0,)), ((), ())), preferred_element_type=F32)
        state_ref[...] = st * chunk_decay + u
        return y

    sf_ref[...] = jnp.zeros_like(sf_ref)
    sb_ref[...] = jnp.zeros_like(sb_ref)

    def step(i, carry):
        r0, qc, kc, vc = load(i)
        s = lax.dot_general(qc, kc, (((1,), (1,)), ((), ())), preferred_element_type=F32) * decay
        y = jnp.dot(s.astype(BF16), vc, preferred_element_type=F32)
        yf_ref[pl.ds(r0, C), :] = y + cross(sf_ref, qc, kc, vc, xi_f, zeta_f, chunk_f)
        r1, qd, kd, vd = load(N - 1 - i)
        yb_ref[pl.ds(r1, C), :] = cross(sb_ref, qd, kd, vd, xi_b, zeta_b, chunk_b)
        return carry

    lax.fori_loop(0, N, step, 0, unroll=2)

    def finish(i, carry):
        rows = pl.ds(pl.multiple_of(i * C, C), C)
        r = yf_ref[rows, :] + yb_ref[rows, :]
        r = r * lax.rsqrt(jnp.mean(r * r, axis=-1, keepdims=True) + EPS)
        g = gate_ref[0, rows, :].astype(F32)
        o_ref[0, rows, :] = (g * _sigmoid(g) * r).astype(o_ref.dtype)
        return carry

    lax.fori_loop(0, N, finish, 0)


def _retention(proj, decay_logit, q_col0, gate_col0, width):
    B, S, _ = proj.shape
    Dk = width // RET_HEADS
    qb = q_col0 // Dk
    gb = gate_col0 // Dk
    nb = RET_HEADS
    return pl.pallas_call(
        _retention_kernel,
        grid=(B, RET_HEADS),
        in_specs=[pl.BlockSpec(memory_space=pltpu.SMEM),
                  pl.BlockSpec((1, S, Dk), lambda b, h: (b, 0, qb + h)),
                  pl.BlockSpec((1, S, Dk), lambda b, h: (b, 0, qb + nb + h)),
                  pl.BlockSpec((1, S, Dk), lambda b, h: (b, 0, qb + 2 * nb + h)),
                  pl.BlockSpec((1, S, Dk), lambda b, h: (b, 0, gb + h))],
        out_specs=pl.BlockSpec((1, S, Dk), lambda b, h: (b, 0, h)),
        out_shape=jax.ShapeDtypeStruct((B, S, RET_HEADS * Dk), BF16),
        scratch_shapes=[pltpu.VMEM((S, Dk), F32), pltpu.VMEM((S, Dk), F32),
                        pltpu.VMEM((Dk, Dk), F32), pltpu.VMEM((Dk, Dk), F32)],
        compiler_params=_params("parallel", "parallel"),
        name="retention",
    )(decay_logit, proj, proj, proj, proj)


def _merge_kernel(attn_ref, ret_ref, ga_ref, gr_ref, x_ref, wa_ref, wr_ref, wo_ref, gain_ref, wrt_ref,
                  wrt_lo_ref, xo_ref, h_ref, logit_ref):
    pa = jnp.dot(attn_ref[...], wa_ref[...], preferred_element_type=F32)
    pr = jnp.dot(ret_ref[...], wr_ref[...], preferred_element_type=F32)
    merged = pa * _sigmoid(ga_ref[...].astype(F32)) + pr * _sigmoid(gr_ref[...].astype(F32))
    x = x_ref[...] + jnp.dot(merged.astype(BF16), wo_ref[...], preferred_element_type=F32)
    xo_ref[...] = x
    h = _rms(x, gain_ref[...])
    h_hi = h.astype(BF16)
    h_ref[...] = h_hi
    h_lo = (h - h_hi.astype(F32)).astype(BF16)
    logit_ref[...] = (jnp.dot(h_hi, wrt_ref[...], preferred_element_type=F32)
                      + jnp.dot(h_lo, wrt_ref[...], preferred_element_type=F32)
                      + jnp.dot(h_hi, wrt_lo_ref[...], preferred_element_type=F32))


def _merge(attn, ret, gates, x2d, wa, wr, wo, gain, w_router, tm):
    T, D = x2d.shape
    w_router_pad = jnp.pad(w_router, ((0, 0), (0, LANES - w_router.shape[1])))
    wrt_hi = w_router_pad.astype(BF16)
    wrt_lo = (w_router_pad - wrt_hi.astype(F32)).astype(BF16)
    tok = lambda width, col=0: pl.BlockSpec((tm, width), lambda i: (i, col))
    full = lambda a: pl.BlockSpec(a.shape, lambda i: (0, 0))
    return pl.pallas_call(
        _merge_kernel,
        grid=(T // tm,),
        in_specs=[tok(ATTN_WIDTH), tok(D), tok(D, 0), tok(D, 1), tok(D),
                  full(wa), full(wr), full(wo), pl.BlockSpec((1, D), lambda i: (0, 0)), full(wrt_hi), full(wrt_lo)],
        out_specs=[tok(D), tok(D), tok(LANES)],
        out_shape=[jax.ShapeDtypeStruct((T, D), F32), jax.ShapeDtypeStruct((T, D), BF16),
                   jax.ShapeDtypeStruct((T, LANES), F32)],
        compiler_params=_params("parallel"),
        name="merge_out_proj",
    )(attn, ret, gates, gates, x2d, wa, wr, wo, gain.reshape(1, D), wrt_hi, wrt_lo)


def _route_kernel(logit_ref, pos_ref, pos_t_ref, aff_ref, before_ref, *, n_experts, cap):
    S = logit_ref.shape[1]
    rows = 256

    @pl.when(pl.program_id(0) == 0)
    def _():
        def fill(i, carry):
            r0 = pl.multiple_of(i * rows, rows)
            r = lax.broadcasted_iota(jnp.int32, (rows, S), 0) + r0
            c = lax.broadcasted_iota(jnp.int32, (rows, S), 1)
            before_ref[pl.ds(r0, rows), :] = jnp.where(r < c, 1.0, 0.0).astype(BF16)
            return carry
        lax.fori_loop(0, S // rows, fill, 0)

    lane = lax.broadcasted_iota(jnp.int32, (1, LANES), 1)
    lg = jnp.where(lane < n_experts, logit_ref[0], MASKED)
    ex = jnp.exp(lg - jnp.max(lg, axis=-1, keepdims=True))
    aff = (ex / jnp.sum(ex, axis=-1, keepdims=True)).T[:n_experts]

    def search(it, thr_bits):
        cand = thr_bits | jnp.left_shift(jnp.int32(1), 30 - it)
        cnt = jnp.sum((aff >= lax.bitcast_convert_type(cand, F32)).astype(jnp.int32), axis=-1, keepdims=True)
        return jnp.where(cnt >= cap, cand, thr_bits)

    floor = lax.bitcast_convert_type(lax.fori_loop(0, 31, search, jnp.zeros((n_experts, 1), jnp.int32)), F32)
    thr = jnp.min(jnp.where(aff >= floor, aff, jnp.inf), axis=-1, keepdims=True)
    above = aff > thr
    tied = aff == thr
    need = (cap - jnp.sum(above.astype(jnp.int32), axis=-1, keepdims=True)).astype(F32)
    count_before = lambda mask: jnp.dot(jnp.where(mask, 1.0, 0.0).astype(BF16), before_ref[...],
                                        preferred_element_type=F32)
    chosen = above | (tied & (count_before(tied) < need))
    pos = jnp.where(chosen, count_before(chosen), -1.0)
    pos_ref[0] = pos
    pos_t_ref[0] = jnp.concatenate([pos, jnp.full((LANES - n_experts, S), -1.0, F32)], axis=0).T
    aff_ref[0] = aff


def _route(logits, cap):
    B, S, _ = logits.shape
    E = N_EXPERTS
    return pl.pallas_call(
        functools.partial(_route_kernel, n_experts=E, cap=cap),
        grid=(B,),
        in_specs=[pl.BlockSpec((1, S, LANES), lambda b: (b, 0, 0))],
        out_specs=[pl.BlockSpec((1, E, S), lambda b: (b, 0, 0)), pl.BlockSpec((1, S, LANES), lambda b: (b, 0, 0)),
                   pl.BlockSpec((1, E, S), lambda b: (b, 0, 0))],
        out_shape=[jax.ShapeDtypeStruct((B, E, S), F32), jax.ShapeDtypeStruct((B, S, LANES), F32),
                   jax.ShapeDtypeStruct((B, E, S), F32)],
        scratch_shapes=[pltpu.VMEM((S, S), BF16)],
        compiler_params=_params("arbitrary"),
        name="route",
    )(logits)


def _slot_mask(pos_ref, cap):
    pos = pos_ref[0, 0].astype(jnp.int32)
    slot = lax.broadcasted_iota(jnp.int32, (cap, pos.shape[1]), 0)
    return slot == pos


def _gather_kernel(h_ref, pos_ref, aff_ref, x_ref, g_ref, *, cap):
    mask = _slot_mask(pos_ref, cap)
    onehot = jnp.where(mask, 1.0, 0.0).astype(BF16)
    x_ref[0, 0] = jnp.dot(onehot, h_ref[0], preferred_element_type=F32).astype(x_ref.dtype)
    g = jnp.sum(jnp.where(mask, aff_ref[0, 0], 0.0), axis=-1, keepdims=True)
    g_ref[0, 0] = jnp.broadcast_to(g, (cap, LANES))


def _gather(h, pos, aff, cap):
    B, S, D = h.shape
    E = pos.shape[1]
    row = pl.BlockSpec((1, 1, 1, S), lambda b, e: (b, e, 0, 0))
    return pl.pallas_call(
        functools.partial(_gather_kernel, cap=cap),
        grid=(B, E),
        in_specs=[pl.BlockSpec((1, S, D), lambda b, e: (b, 0, 0)), row, row],
        out_specs=[pl.BlockSpec((1, 1, cap, D), lambda b, e: (e, b, 0, 0)),
                   pl.BlockSpec((1, 1, cap, LANES), lambda b, e: (e, b, 0, 0))],
        out_shape=[jax.ShapeDtypeStruct((E, B, cap, D), BF16), jax.ShapeDtypeStruct((E, B, cap, LANES), F32)],
        compiler_params=_params("parallel", "arbitrary"),
        name="expert_gather",
    )(h, pos.reshape(B, E, 1, S), aff.reshape(B, E, 1, S))


def _expert_kernel(x_ref, g_ref, wg_ref, wu_ref, wd_ref, y_ref, act_ref, wd_bf_ref, *, nf, tf):
    f = pl.program_id(1)
    x = x_ref[0]
    a = jnp.dot(x, wg_ref[...].astype(BF16), preferred_element_type=F32)
    u = jnp.dot(x, wu_ref[...].astype(BF16), preferred_element_type=F32)
    act = (a * _sigmoid(a) * u).astype(BF16)
    wd_bf_ref[pl.ds(pl.multiple_of(f * tf, tf), tf), :] = wd_ref[...].astype(BF16)
    for k in range(nf):
        @pl.when(f == k)
        def _(k=k):
            act_ref[:, k * tf:(k + 1) * tf] = act

    @pl.when(f == nf - 1)
    def _():
        y = jnp.dot(act_ref[...], wd_bf_ref[...], preferred_element_type=F32)
        y_ref[0] = (y * g_ref[0][:, :1]).astype(y_ref.dtype)


def _experts(xin, g, w_gate, w_up, w_down, layer, tf):
    E, M, D = xin.shape
    FF = w_gate.shape[3]
    return pl.pallas_call(
        functools.partial(_expert_kernel, nf=FF // tf, tf=tf),
        grid=(E, FF // tf),
        in_specs=[pl.BlockSpec((1, M, D), lambda e, f: (e, 0, 0)),
                  pl.BlockSpec((1, M, LANES), lambda e, f: (e, 0, 0)),
                  pl.BlockSpec((None, None, D, tf), lambda e, f: (layer, e, 0, f)),
                  pl.BlockSpec((None, None, D, tf), lambda e, f: (layer, e, 0, f)),
                  pl.BlockSpec((None, None, tf, D), lambda e, f: (layer, e, f, 0))],
        out_specs=pl.BlockSpec((1, M, D), lambda e, f: (e, 0, 0)),
        out_shape=jax.ShapeDtypeStruct((E, M, D), BF16),
        scratch_shapes=[pltpu.VMEM((M, FF), BF16), pltpu.VMEM((FF, D), BF16)],
        compiler_params=_params("parallel", "arbitrary"),
        name="expert_swiglu",
    )(xin, g, w_gate, w_up, w_down)


def _scatter_kernel(x_ref, pos_t_ref, y_ref, o_ref, onehot_ref, *, n_experts, cap):
    @pl.when(pl.program_id(1) == 0)
    def _():
        pos = pos_t_ref[0].astype(jnp.int32)
        slot = lax.broadcasted_iota(jnp.int32, (pos.shape[0], cap), 1)
        for e in range(n_experts):
            onehot_ref[:, e * cap:(e + 1) * cap] = jnp.where(pos[:, e:e + 1] == slot, 1.0, 0.0).astype(BF16)

    y = y_ref[:, 0].reshape(n_experts * cap, y_ref.shape[-1])
    o_ref[0] = x_ref[0] + jnp.dot(onehot_ref[...], y, preferred_element_type=F32)


def _scatter(x, pos_t, y, cap, tn):
    B, S, D = x.shape
    E = y.shape[0]
    return pl.pallas_call(
        functools.partial(_scatter_kernel, n_experts=E, cap=cap),
        grid=(B, D // tn),
        in_specs=[pl.BlockSpec((1, S, tn), lambda b, n: (b, 0, n)),
                  pl.BlockSpec((1, S, LANES), lambda b, n: (b, 0, 0)),
                  pl.BlockSpec((E, 1, cap, tn), lambda b, n: (0, b, 0, n))],
        out_specs=pl.BlockSpec((1, S, tn), lambda b, n: (b, 0, n)),
        out_shape=jax.ShapeDtypeStruct((B, S, D), F32),
        scratch_shapes=[pltpu.VMEM((S, E * cap), BF16)],
        compiler_params=_params("parallel", "arbitrary"),
        name="expert_scatter",
    )(x, pos_t, y)


def _final_norm_kernel(x_ref, gain_ref, o_ref):
    o_ref[...] = _rms(x_ref[...], gain_ref[...])


def _final_norm(x2d, gain, tm):
    T, D = x2d.shape
    return pl.pallas_call(
        _final_norm_kernel,
        grid=(T // tm,),
        in_specs=[pl.BlockSpec((tm, D), lambda i: (i, 0)), pl.BlockSpec((1, D), lambda i: (0, 0))],
        out_specs=pl.BlockSpec((tm, D), lambda i: (i, 0)),
        out_shape=jax.ShapeDtypeStruct((T, D), F32),
        compiler_params=_params("parallel"),
        name="final_norm",
    )(x2d, gain.reshape(1, D))


def kernel(x, w_in, w_attn_out, w_ret_out, w_out, ret_decay_logit, norm_mix, norm_ffn, w_router, w_gate, w_up,
           w_down, norm_final):
    B, S, D = x.shape
    T = B * S
    depth = w_in.shape[0]
    W = ATTN_WIDTH
    attn_in = len(ATTN_PATTERNS) * 3 * W
    n_in = w_in.shape[2]
    gates0, swish0 = attn_in + 4 * D, attn_in + 3 * D
    attn0, ret0 = 3 * D, 3 * D + attn_in
    cap = CAPACITY_FACTOR * S // N_EXPERTS
    x2d = x.reshape(T, D)
    for layer in range(depth):
        w = w_in[layer]
        w_in_b = jnp.concatenate([w[:, gates0:], w[:, swish0:gates0], w[:, :swish0]], axis=1).astype(BF16)
        proj = _norm_proj(x2d, norm_mix[layer], w_in_b, 0, n_in, BF16, tm=2048, tn=1536)
        proj3 = proj.reshape(B, S, n_in)

        attn = _dilated_attention(proj3, attn0)
        ret = _retention(proj3, ret_decay_logit[layer], ret0, 2 * D, D)

        x2d, h2, logits = _merge(attn.reshape(T, W), ret.reshape(T, D), proj, x2d, w_attn_out[layer].astype(BF16),
                                 w_ret_out[layer].astype(BF16), w_out[layer].astype(BF16), norm_ffn[layer],
                                 w_router[layer], tm=512)

        pos, pos_t, aff = _route(logits.reshape(B, S, LANES), cap)
        xin, g = _gather(h2.reshape(B, S, D), pos, aff, cap)
        y = _experts(xin.reshape(N_EXPERTS, B * cap, D), g.reshape(N_EXPERTS, B * cap, LANES),
                     w_gate, w_up, w_down, layer, tf=256)
        x2d = _scatter(x2d.reshape(B, S, D), pos_t, y.reshape(N_EXPERTS, B, cap, D), cap, tn=512).reshape(T, D)
    return _final_norm(x2d, norm_final, tm=1024).reshape(B, S, D)
```

```python
import functools

import jax
import jax.numpy as jnp
from jax import lax
from jax.experimental import pallas as pl
from jax.experimental.pallas import tpu as pltpu

EPS = 1e-6
ATTN_PATTERNS = ((128, 1), (512, 4), (2048, 16))
ATTN_HEADS = 8
ATTN_HEAD_DIM = 64
ATTN_WIDTH = ATTN_HEADS * ATTN_HEAD_DIM
RET_HEADS = 4
RET_CHUNK = 256
N_EXPERTS = 16
CAPACITY_FACTOR = 2

LANES = 128
MASKED = -1e30
VMEM_LIMIT_BYTES = 56 * 1024 * 1024

F32 = jnp.float32
BF16 = jnp.bfloat16


def _params(*semantics):
    return pltpu.CompilerParams(dimension_semantics=semantics, vmem_limit_bytes=VMEM_LIMIT_BYTES)


def _sigmoid(x):
    return 0.5 * (1.0 + jnp.tanh(0.5 * x))


def _rms(x, gain):
    return x * lax.rsqrt(jnp.mean(x * x, axis=-1, keepdims=True) + EPS) * gain


def _norm_proj_kernel(x_ref, gain_ref, w_ref, o_ref, h_ref):
    @pl.when(pl.program_id(1) == 0)
    def _():
        h_ref[...] = _rms(x_ref[...], gain_ref[...]).astype(BF16)

    o_ref[...] = jnp.dot(h_ref[...], w_ref[...], preferred_element_type=F32).astype(o_ref.dtype)


def _norm_proj(x2d, gain, w, col0, ncols, out_dtype, tm, tn):
    T, D = x2d.shape
    assert T % tm == 0 and ncols % tn == 0 and col0 % tn == 0
    return pl.pallas_call(
        _norm_proj_kernel,
        grid=(T // tm, ncols // tn),
        in_specs=[pl.BlockSpec((tm, D), lambda i, j: (i, 0)),
                  pl.BlockSpec((1, D), lambda i, j: (0, 0)),
                  pl.BlockSpec((D, tn), lambda i, j: (0, j + col0 // tn))],
        out_specs=pl.BlockSpec((tm, tn), lambda i, j: (i, j)),
        out_shape=jax.ShapeDtypeStruct((T, ncols), out_dtype),
        scratch_shapes=[pltpu.VMEM((tm, D), BF16)],
        compiler_params=_params("parallel", "arbitrary"),
        name="norm_proj",
    )(x2d, gain.reshape(1, D), w)


ATTN_TQ = 128
ATTN_BLOCK = 1
ATTN_RING = 3


def _attn_kernel(*refs, S, patterns):
    n_groups = len(patterns)
    in_refs, o_ref = refs[:3 * n_groups], refs[3 * n_groups]
    stage_ref, qs_ref, ks_ref, vs_ref, acc_ref, den_ref, max_ref, bias_ref, s_ref, m_ref = refs[3 * n_groups + 1:]
    tq = ATTN_TQ
    n_tiles = S // tq
    hp = pl.program_id(0)
    lane = lax.broadcasted_iota(jnp.int32, (1, LANES), 1)
    first_head = lane < ATTN_HEAD_DIM
    geometry = []

    for gi, (window, d) in enumerate(patterns):
        half = window // (2 * d)
        tk = tq + 2 * half
        L = S // d
        seg = L + 2 * half
        geometry.append((d, half, tk, L // tq))
        q_in, k_in, v_in = in_refs[3 * gi:3 * gi + 3]

        @pl.when(pl.program_id(1) == 0)
        def _(gi=gi, d=d, half=half, tk=tk):
            r = lax.broadcasted_iota(jnp.int32, (tq, tk), 0)
            c = lax.broadcasted_iota(jnp.int32, (tq, tk), 1)
            rel = jnp.abs(c - half - r)
            dist = (d * rel).astype(F32)
            for hh in range(2):
                pow2 = jnp.full((tq, tk), jnp.left_shift(jnp.int32(2), 2 * hp + hh), jnp.int32).astype(F32)
                band = jnp.where(rel <= half, -(dist / pow2), MASKED)
                for kind in range(4):
                    tile_bias = band
                    if kind & 1:
                        tile_bias = jnp.where(c < half, MASKED, tile_bias)
                    if kind & 2:
                        tile_bias = jnp.where(c >= tq + half, MASKED, tile_bias)
                    bias_ref[gi, kind, hh * tq:(hh + 1) * tq, :] = tile_bias

        vs_ref[gi, :, LANES:] = jnp.ones((vs_ref.shape[1], LANES), BF16)
        zeros = jnp.zeros((half, LANES), BF16)
        if d == 1:
            ks_ref[gi, pl.ds(0, half), :] = zeros
            ks_ref[gi, pl.ds(half, S), :] = k_in[0]
            ks_ref[gi, pl.ds(half + S, half), :] = zeros
            vs_ref[gi, pl.ds(0, half), :LANES] = zeros
            vs_ref[gi, pl.ds(half, S), :LANES] = v_in[0]
            vs_ref[gi, pl.ds(half + S, half), :LANES] = zeros
        else:
            stage_ref[0] = q_in[0].astype(F32)
            stage_ref[1] = k_in[0].astype(F32)
            stage_ref[2] = v_in[0].astype(F32)

            def regroup(r, carry, gi=gi, d=d, L=L, seg=seg, half=half):
                rows = pl.ds(r, L, stride=d)
                k0 = pl.multiple_of(r * seg, half)
                qs_ref[gi, pl.ds(pl.multiple_of(r * L, tq), L), :] = stage_ref[0, rows, :].astype(BF16)
                ks_ref[gi, pl.ds(k0, half), :] = zeros
                ks_ref[gi, pl.ds(k0 + half, L), :] = stage_ref[1, rows, :].astype(BF16)
                ks_ref[gi, pl.ds(k0 + half + L, half), :] = zeros
                vs_ref[gi, pl.ds(k0, half), :LANES] = zeros
                vs_ref[gi, pl.ds(k0 + half, L), :LANES] = stage_ref[2, rows, :].astype(BF16)
                vs_ref[gi, pl.ds(k0 + half + L, half), :LANES] = zeros
                return carry

            lax.fori_loop(0, d, regroup, 0)

    def key_row(gi, u):
        d, half, tk, nblk = geometry[gi]
        return (u + u // nblk) * tq

    def scores(gi, u, slot):
        d, half, tk, nblk = geometry[gi]
        assert 2 * half == tq
        i = u % nblk
        kind = (1 if i == 0 else 0) + (2 if i == nblk - 1 else 0)
        qp = in_refs[3 * gi][0, u * tq:(u + 1) * tq, :] if d == 1 else qs_ref[gi, u * tq:(u + 1) * tq, :]
        qp = qp * (ATTN_HEAD_DIM ** -0.5)
        none = jnp.zeros_like(qp)
        q2 = jnp.concatenate([jnp.where(first_head, qp, none), jnp.where(first_head, none, qp)], axis=0)
        k0 = key_row(gi, u)
        s = lax.dot_general(q2, ks_ref[gi, k0:k0 + tk, :], (((1,), (1,)), ((), ())), preferred_element_type=F32)
        s_ref[slot] = s + bias_ref[gi, kind]

    def maxima(slot):
        m_ref[slot] = jnp.broadcast_to(jnp.max(s_ref[slot], axis=-1, keepdims=True), (2 * tq, LANES))

    def values(gi, u, slot):
        d, half, tk, nblk = geometry[gi]
        r, i = u // nblk, u % nblk
        m = m_ref[slot]
        p = jnp.concatenate([jnp.exp(s_ref[slot, :, c0:c0 + LANES] - m) for c0 in range(0, tk, LANES)], axis=1)
        k0 = key_row(gi, u)
        pv = jnp.dot(p.astype(BF16), vs_ref[gi, k0:k0 + tk, :], preferred_element_type=F32)
        rows = pl.ds(u * tq, tq) if d == 1 else pl.ds(i * (tq * d) + r, tq, stride=d)
        acc_ref[gi, rows, :] = jnp.where(first_head, pv[:tq, :LANES], pv[tq:, :LANES])
        den_ref[gi, rows, :] = jnp.where(first_head, pv[:tq, LANES:], pv[tq:, LANES:])
        max_ref[gi, rows, :] = jnp.where(first_head, m[:tq], m[tq:])

    blocks = [(gi, j) for gi in range(n_groups) for j in range(n_tiles // ATTN_BLOCK)]
    for step in range(len(blocks) + 2):
        for stage, fn in enumerate((scores, maxima, values)):
            t = step - stage
            if 0 <= t < len(blocks):
                gi, j = blocks[t]
                for k in range(ATTN_BLOCK):
                    slot = (t % ATTN_RING) * ATTN_BLOCK + k
                    if fn is maxima:
                        fn(slot)
                    else:
                        fn(gi, j * ATTN_BLOCK + k, slot)

    def merge(j, carry):
        rows = pl.ds(pl.multiple_of(j * tq, tq), tq)
        top = functools.reduce(jnp.maximum, [max_ref[g, rows, :] for g in range(n_groups)])
        w = [jnp.exp(max_ref[g, rows, :] - top) for g in range(n_groups)]
        num = sum(w[g] * acc_ref[g, rows, :] for g in range(n_groups))
        den = sum(w[g] * den_ref[g, rows, :] for g in range(n_groups))
        o_ref[0, rows, :] = (num / den).astype(o_ref.dtype)
        return carry

    lax.fori_loop(0, n_tiles, merge, 0)


def _dilated_attention(proj, col0):
    B, S, _ = proj.shape
    W = ATTN_WIDTH
    n_groups = len(ATTN_PATTERNS)
    halves = {w // (2 * d) for w, d in ATTN_PATTERNS}
    assert len(halves) == 1 and all(S % (d * ATTN_TQ) == 0 for _, d in ATTN_PATTERNS)
    half = halves.pop()
    tk = ATTN_TQ + 2 * half
    assert tk % LANES == 0 and (S // ATTN_TQ) % ATTN_BLOCK == 0
    pairs = W // LANES
    col = lambda c0: pl.BlockSpec((1, S, LANES), lambda hp, b: (b, 0, c0 // LANES + hp))
    return pl.pallas_call(
        functools.partial(_attn_kernel, S=S, patterns=ATTN_PATTERNS),
        grid=(pairs, B),
        in_specs=[col(col0 + (3 * g + t) * W) for g in range(n_groups) for t in range(3)],
        out_specs=pl.BlockSpec((1, S, LANES), lambda hp, b: (b, 0, hp)),
        out_shape=jax.ShapeDtypeStruct((B, S, W), BF16),
        scratch_shapes=[pltpu.VMEM((3, S, LANES), F32),
                        pltpu.VMEM((n_groups, S, LANES), BF16),
                        pltpu.VMEM((n_groups, 2 * S, LANES), BF16),
                        pltpu.VMEM((n_groups, 2 * S, 2 * LANES), BF16),
                        pltpu.VMEM((n_groups, S, LANES), F32),
                        pltpu.VMEM((n_groups, S, LANES), F32),
                        pltpu.VMEM((n_groups, S, LANES), F32),
                        pltpu.VMEM((n_groups, 4, 2 * ATTN_TQ, tk), F32),
                        pltpu.VMEM((ATTN_RING * ATTN_BLOCK, 2 * ATTN_TQ, tk), F32),
                        pltpu.VMEM((ATTN_RING * ATTN_BLOCK, 2 * ATTN_TQ, LANES), F32)],
        compiler_params=_params("arbitrary", "arbitrary"),
        name="dilated_attention",
    )(*([proj] * (3 * n_groups)))


def _log_sigmoid(x):
    return jnp.minimum(x, 0.0) - jnp.log1p(jnp.exp(-jnp.abs(x)))


def _retention_kernel(logit_ref, q_ref, k_ref, v_ref, gate_ref, o_ref, yf_ref, yb_ref, sf_ref, sb_ref):
    C = RET_CHUNK
    S, Dk = q_ref.shape[1], q_ref.shape[2]
    N = S // C
    h = pl.program_id(1)
    lf = _log_sigmoid(jnp.full((C, 1), logit_ref[0, h], F32))
    lb = _log_sigmoid(jnp.full((C, 1), logit_ref[1, h], F32))
    rel = (lax.broadcasted_iota(jnp.int32, (C, C), 0) - lax.broadcasted_iota(jnp.int32, (C, C), 1)).astype(F32)
    decay = jnp.exp(jnp.where(rel >= 0, lf * rel, -lb * rel))
    pos = lax.broadcasted_iota(jnp.int32, (C, 1), 0).astype(F32)
    xi_f = jnp.exp(lf * (pos + 1.0))
    zeta_f = jnp.exp(lf * (C - 1.0 - pos))
    xi_b = jnp.exp(lb * (C - pos))
    zeta_b = jnp.exp(lb * pos)
    chunk_f = jnp.exp(_log_sigmoid(jnp.full((Dk, 1), logit_ref[0, h], F32)) * C)
    chunk_b = jnp.exp(_log_sigmoid(jnp.full((Dk, 1), logit_ref[1, h], F32)) * C)
    scale = Dk ** -0.5

    def load(i):
        r0 = pl.multiple_of(i * C, C)
        return r0, q_ref[0, pl.ds(r0, C), :] * scale, k_ref[0, pl.ds(r0, C), :], v_ref[0, pl.ds(r0, C), :]

    def cross(state_ref, qc, kc, vc, xi, zeta, chunk_decay):
        st = state_ref[...]
        y = jnp.dot(qc, st.astype(BF16), preferred_element_type=F32) * xi
        kz = (kc.astype(F32) * zeta).astype(BF16)
        u = lax.dot_general(kz, vc, (((0,), (0,)), ((), ())), preferred_element_type=F32)
        state_ref[...] = st * chunk_decay + u
        return y

    sf_ref[...] = jnp.zeros_like(sf_ref)
    sb_ref[...] = jnp.zeros_like(sb_ref)

    def step(i, carry):
        r0, qc, kc, vc = load(i)
        s = lax.dot_general(qc, kc, (((1,), (1,)), ((), ())), preferred_element_type=F32) * decay
        y = jnp.dot(s.astype(BF16), vc, preferred_element_type=F32)
        yf_ref[pl.ds(r0, C), :] = y + cross(sf_ref, qc, kc, vc, xi_f, zeta_f, chunk_f)
        r1, qd, kd, vd = load(N - 1 - i)
        yb_ref[pl.ds(r1, C), :] = cross(sb_ref, qd, kd, vd, xi_b, zeta_b, chunk_b)
        return carry

    lax.fori_loop(0, N, step, 0, unroll=2)

    def finish(i, carry):
        rows = pl.ds(pl.multiple_of(i * C, C), C)
        r = yf_ref[rows, :] + yb_ref[rows, :]
        r = r * lax.rsqrt(jnp.mean(r * r, axis=-1, keepdims=True) + EPS)
        g = gate_ref[0, rows, :].astype(F32)
        o_ref[0, rows, :] = (g * _sigmoid(g) * r).astype(o_ref.dtype)
        return carry

    lax.fori_loop(0, N, finish, 0)


def _retention(proj, decay_logit, q_col0, gate_col0, width):
    B, S, _ = proj.shape
    Dk = width // RET_HEADS
    qb = q_col0 // Dk
    gb = gate_col0 // Dk
    nb = RET_HEADS
    return pl.pallas_call(
        _retention_kernel,
        grid=(B, RET_HEADS),
        in_specs=[pl.BlockSpec(memory_space=pltpu.SMEM),
                  pl.BlockSpec((1, S, Dk), lambda b, h: (b, 0, qb + h)),
                  pl.BlockSpec((1, S, Dk), lambda b, h: (b, 0, qb + nb + h)),
                  pl.BlockSpec((1, S, Dk), lambda b, h: (b, 0, qb + 2 * nb + h)),
                  pl.BlockSpec((1, S, Dk), lambda b, h: (b, 0, gb + h))],
        out_specs=pl.BlockSpec((1, S, Dk), lambda b, h: (b, 0, h)),
        out_shape=jax.ShapeDtypeStruct((B, S, RET_HEADS * Dk), BF16),
        scratch_shapes=[pltpu.VMEM((S, Dk), F32), pltpu.VMEM((S, Dk), F32),
                        pltpu.VMEM((Dk, Dk), F32), pltpu.VMEM((Dk, Dk), F32)],
        compiler_params=_params("parallel", "parallel"),
        name="retention",
    )(decay_logit, proj, proj, proj, proj)


def _merge_kernel(attn_ref, ret_ref, ga0_ref, ga1_ref, gr0_ref, gr1_ref, x_ref, wa_ref, wr_ref, wo_ref, gain_ref,
                  wrt_ref, wrt_lo_ref, xo_ref, h_ref, logit_ref):
    pa = jnp.dot(attn_ref[...], wa_ref[...], preferred_element_type=F32)
    pr = jnp.dot(ret_ref[...], wr_ref[...], preferred_element_type=F32)
    ga = jnp.concatenate([ga0_ref[...], ga1_ref[...]], axis=1).astype(F32)
    gr = jnp.concatenate([gr0_ref[...], gr1_ref[...]], axis=1).astype(F32)
    merged = pa * _sigmoid(ga) + pr * _sigmoid(gr)
    x = x_ref[...] + jnp.dot(merged.astype(BF16), wo_ref[...], preferred_element_type=F32)
    xo_ref[...] = x
    h = _rms(x, gain_ref[...])
    h_hi = h.astype(BF16)
    h_ref[...] = h_hi
    h_lo = (h - h_hi.astype(F32)).astype(BF16)
    logit_ref[...] = (jnp.dot(h_hi, wrt_ref[...], preferred_element_type=F32)
                      + jnp.dot(h_lo, wrt_ref[...], preferred_element_type=F32)
                      + jnp.dot(h_hi, wrt_lo_ref[...], preferred_element_type=F32))


def _merge(attn, ret, gates, gate_col0, x2d, wa, wr, wo, gain, w_router, tm):
    T, D = x2d.shape
    gw = D // 2
    assert gate_col0 % gw == 0
    g0 = gate_col0 // gw
    w_router_pad = jnp.pad(w_router, ((0, 0), (0, LANES - w_router.shape[1])))
    wrt_hi = w_router_pad.astype(BF16)
    wrt_lo = (w_router_pad - wrt_hi.astype(F32)).astype(BF16)
    tok = lambda width, col=0: pl.BlockSpec((tm, width), lambda i: (i, col))
    full = lambda a: pl.BlockSpec(a.shape, lambda i: (0, 0))
    return pl.pallas_call(
        _merge_kernel,
        grid=(T // tm,),
        in_specs=[tok(ATTN_WIDTH), tok(D), tok(gw, g0), tok(gw, g0 + 1), tok(gw, g0 + 2), tok(gw, g0 + 3), tok(D),
                  full(wa), full(wr), full(wo), pl.BlockSpec((1, D), lambda i: (0, 0)), full(wrt_hi), full(wrt_lo)],
        out_specs=[tok(D), tok(D), tok(LANES)],
        out_shape=[jax.ShapeDtypeStruct((T, D), F32), jax.ShapeDtypeStruct((T, D), BF16),
                   jax.ShapeDtypeStruct((T, LANES), F32)],
        compiler_params=_params("parallel"),
        name="merge_out_proj",
    )(attn, ret, gates, gates, gates, gates, x2d, wa, wr, wo, gain.reshape(1, D), wrt_hi, wrt_lo)


def _route_kernel(logit_ref, pos_ref, pos_t_ref, aff_ref, before_ref, *, n_experts, cap):
    S = logit_ref.shape[1]
    rows = 256

    @pl.when(pl.program_id(0) == 0)
    def _():
        def fill(i, carry):
            r0 = pl.multiple_of(i * rows, rows)
            r = lax.broadcasted_iota(jnp.int32, (rows, S), 0) + r0
            c = lax.broadcasted_iota(jnp.int32, (rows, S), 1)
            before_ref[pl.ds(r0, rows), :] = jnp.where(r < c, 1.0, 0.0).astype(BF16)
            return carry
        lax.fori_loop(0, S // rows, fill, 0)

    lane = lax.broadcasted_iota(jnp.int32, (1, LANES), 1)
    lg = jnp.where(lane < n_experts, logit_ref[0], MASKED)
    ex = jnp.exp(lg - jnp.max(lg, axis=-1, keepdims=True))
    aff = (ex / jnp.sum(ex, axis=-1, keepdims=True)).T[:n_experts]

    def search(it, thr_bits):
        cand = thr_bits | jnp.left_shift(jnp.int32(1), 30 - it)
        cnt = jnp.sum((aff >= lax.bitcast_convert_type(cand, F32)).astype(jnp.int32), axis=-1, keepdims=True)
        return jnp.where(cnt >= cap, cand, thr_bits)

    floor = lax.bitcast_convert_type(lax.fori_loop(0, 31, search, jnp.zeros((n_experts, 1), jnp.int32)), F32)
    thr = jnp.min(jnp.where(aff >= floor, aff, jnp.inf), axis=-1, keepdims=True)
    above = aff > thr
    tied = aff == thr
    need = (cap - jnp.sum(above.astype(jnp.int32), axis=-1, keepdims=True)).astype(F32)
    count_before = lambda mask: jnp.dot(jnp.where(mask, 1.0, 0.0).astype(BF16), before_ref[...],
                                        preferred_element_type=F32)
    chosen = above | (tied & (count_before(tied) < need))
    pos = jnp.where(chosen, count_before(chosen), -1.0)
    pos_ref[0] = pos
    pos_t_ref[0] = jnp.concatenate([pos, jnp.full((LANES - n_experts, S), -1.0, F32)], axis=0).T
    aff_ref[0] = aff


def _route(logits, cap):
    B, S, _ = logits.shape
    E = N_EXPERTS
    return pl.pallas_call(
        functools.partial(_route_kernel, n_experts=E, cap=cap),
        grid=(B,),
        in_specs=[pl.BlockSpec((1, S, LANES), lambda b: (b, 0, 0))],
        out_specs=[pl.BlockSpec((1, E, S), lambda b: (b, 0, 0)), pl.BlockSpec((1, S, LANES), lambda b: (b, 0, 0)),
                   pl.BlockSpec((1, E, S), lambda b: (b, 0, 0))],
        out_shape=[jax.ShapeDtypeStruct((B, E, S), F32), jax.ShapeDtypeStruct((B, S, LANES), F32),
                   jax.ShapeDtypeStruct((B, E, S), F32)],
        scratch_shapes=[pltpu.VMEM((S, S), BF16)],
        compiler_params=_params("arbitrary"),
        name="route",
    )(logits)


def _gather_kernel(h_ref, pos_ref, aff_ref, x_ref, g_ref, *, n_experts, cap):
    S = h_ref.shape[1]
    slot = lax.broadcasted_iota(jnp.int32, (cap, S), 0)
    for e in range(n_experts):
        mask = slot == pos_ref[0, e:e + 1, :].astype(jnp.int32)
        onehot = jnp.where(mask, 1.0, 0.0).astype(BF16)
        x_ref[e, 0] = jnp.dot(onehot, h_ref[0], preferred_element_type=F32).astype(x_ref.dtype)
        g = jnp.sum(jnp.where(mask, aff_ref[0, e:e + 1, :], 0.0), axis=-1, keepdims=True)
        g_ref[e, 0] = jnp.broadcast_to(g, (cap, LANES))


def _gather(h, pos, aff, cap):
    B, S, D = h.shape
    E = pos.shape[1]
    rows = pl.BlockSpec((1, E, S), lambda b: (b, 0, 0))
    return pl.pallas_call(
        functools.partial(_gather_kernel, n_experts=E, cap=cap),
        grid=(B,),
        in_specs=[pl.BlockSpec((1, S, D), lambda b: (b, 0, 0)), rows, rows],
        out_specs=[pl.BlockSpec((E, 1, cap, D), lambda b: (0, b, 0, 0)),
                   pl.BlockSpec((E, 1, cap, LANES), lambda b: (0, b, 0, 0))],
        out_shape=[jax.ShapeDtypeStruct((E, B, cap, D), BF16), jax.ShapeDtypeStruct((E, B, cap, LANES), F32)],
        compiler_params=_params("parallel"),
        name="expert_gather",
    )(h, pos, aff)


def _expert_kernel(x_ref, g_ref, wg_ref, wu_ref, wd_ref, y_ref, act_ref, wd_bf_ref, *, nf, tf):
    f = pl.program_id(1)
    x = x_ref[0]
    a = jnp.dot(x, wg_ref[...].astype(BF16), preferred_element_type=F32)
    u = jnp.dot(x, wu_ref[...].astype(BF16), preferred_element_type=F32)
    act = (a * _sigmoid(a) * u).astype(BF16)
    wd_bf_ref[pl.ds(pl.multiple_of(f * tf, tf), tf), :] = wd_ref[...].astype(BF16)
    for k in range(nf):
        @pl.when(f == k)
        def _(k=k):
            act_ref[:, k * tf:(k + 1) * tf] = act

    @pl.when(f == nf - 1)
    def _():
        y = jnp.dot(act_ref[...], wd_bf_ref[...], preferred_element_type=F32)
        y_ref[0] = (y * g_ref[0][:, :1]).astype(y_ref.dtype)


def _experts(xin, g, w_gate, w_up, w_down, layer, tf):
    E, M, D = xin.shape
    FF = w_gate.shape[3]
    return pl.pallas_call(
        functools.partial(_expert_kernel, nf=FF // tf, tf=tf),
        grid=(E, FF // tf),
        in_specs=[pl.BlockSpec((1, M, D), lambda e, f: (e, 0, 0)),
                  pl.BlockSpec((1, M, LANES), lambda e, f: (e, 0, 0)),
                  pl.BlockSpec((None, None, D, tf), lambda e, f: (layer, e, 0, f)),
                  pl.BlockSpec((None, None, D, tf), lambda e, f: (layer, e, 0, f)),
                  pl.BlockSpec((None, None, tf, D), lambda e, f: (layer, e, f, 0))],
        out_specs=pl.BlockSpec((1, M, D), lambda e, f: (e, 0, 0)),
        out_shape=jax.ShapeDtypeStruct((E, M, D), BF16),
        scratch_shapes=[pltpu.VMEM((M, FF), BF16), pltpu.VMEM((FF, D), BF16)],
        compiler_params=_params("parallel", "arbitrary"),
        name="expert_swiglu",
    )(xin, g, w_gate, w_up, w_down)


def _scatter_kernel(x_ref, pos_t_ref, y_ref, gain_ref, o_ref, *, n_experts, cap, normalize):
    pos = pos_t_ref[0].astype(jnp.int32)
    slot = lax.broadcasted_iota(jnp.int32, (pos.shape[0], cap), 1)
    onehot = jnp.concatenate([jnp.where(pos[:, e:e + 1] == slot, 1.0, 0.0).astype(BF16) for e in range(n_experts)],
                             axis=1)
    y = y_ref[:, 0].reshape(n_experts * cap, y_ref.shape[-1])
    out = x_ref[0] + jnp.dot(onehot, y, preferred_element_type=F32)
    o_ref[0] = _rms(out, gain_ref[...]) if normalize else out


def _scatter(x, pos_t, y, cap, ts, gain, normalize):
    B, S, D = x.shape
    E = y.shape[0]
    return pl.pallas_call(
        functools.partial(_scatter_kernel, n_experts=E, cap=cap, normalize=normalize),
        grid=(B, S // ts),
        in_specs=[pl.BlockSpec((1, ts, D), lambda b, t: (b, t, 0)),
                  pl.BlockSpec((1, ts, LANES), lambda b, t: (b, t, 0)),
                  pl.BlockSpec((E, 1, cap, D), lambda b, t: (0, b, 0, 0)),
                  pl.BlockSpec((1, D), lambda b, t: (0, 0))],
        out_specs=pl.BlockSpec((1, ts, D), lambda b, t: (b, t, 0)),
        out_shape=jax.ShapeDtypeStruct((B, S, D), F32),
        compiler_params=_params("parallel", "arbitrary"),
        name="expert_scatter",
    )(x, pos_t, y, gain.reshape(1, D))


def kernel(x, w_in, w_attn_out, w_ret_out, w_out, ret_decay_logit, norm_mix, norm_ffn, w_router, w_gate, w_up,
           w_down, norm_final):
    B, S, D = x.shape
    T = B * S
    depth = w_in.shape[0]
    assert depth >= 1
    W = ATTN_WIDTH
    attn_in = len(ATTN_PATTERNS) * 3 * W
    n_in = w_in.shape[2]
    ret0, swish0, gates0 = attn_in, attn_in + 3 * D, attn_in + 4 * D
    cap = CAPACITY_FACTOR * S // N_EXPERTS
    x2d = x.reshape(T, D)
    for layer in range(depth):
        proj = _norm_proj(x2d, norm_mix[layer], w_in[layer].astype(BF16), 0, n_in, BF16, tm=2048, tn=1536)
        proj3 = proj.reshape(B, S, n_in)

        attn = _dilated_attention(proj3, 0)
        ret = _retention(proj3, ret_decay_logit[layer], ret0, swish0, D)

        x2d, h2, logits = _merge(attn.reshape(T, W), ret.reshape(T, D), proj, gates0, x2d,
                                 w_attn_out[layer].astype(BF16),
                                 w_ret_out[layer].astype(BF16), w_out[layer].astype(BF16), norm_ffn[layer],
                                 w_router[layer], tm=512)

        pos, pos_t, aff = _route(logits.reshape(B, S, LANES), cap)
        xin, g = _gather(h2.reshape(B, S, D), pos, aff, cap)
        y = _experts(xin.reshape(N_EXPERTS, B * cap, D), g.reshape(N_EXPERTS, B * cap, LANES),
                     w_gate, w_up, w_down, layer, tf=256)
        x2d = _scatter(x2d.reshape(B, S, D), pos_t, y.reshape(N_EXPERTS, B, cap, D), cap, 1024, norm_final,
                       normalize=layer == depth - 1).reshape(T, D)
    return x2d.reshape(B, S, D)
```

```python
import functools

import jax
import jax.numpy as jnp
from jax import lax
from jax.experimental import pallas as pl
from jax.experimental.pallas import tpu as pltpu

EPS = 1e-6
ATTN_PATTERNS = ((128, 1), (512, 4), (2048, 16))
ATTN_HEADS = 8
ATTN_HEAD_DIM = 64
ATTN_WIDTH = ATTN_HEADS * ATTN_HEAD_DIM
RET_HEADS = 4
RET_CHUNK = 256
N_EXPERTS = 16
CAPACITY_FACTOR = 2

LANES = 128
MASKED = -1e30
VMEM_LIMIT_BYTES = 56 * 1024 * 1024

F32 = jnp.float32
BF16 = jnp.bfloat16


def _params(*semantics):
    return pltpu.CompilerParams(dimension_semantics=semantics, vmem_limit_bytes=VMEM_LIMIT_BYTES)


def _sigmoid(x):
    return 0.5 * (1.0 + jnp.tanh(0.5 * x))


def _rms(x, gain):
    return x * lax.rsqrt(jnp.mean(x * x, axis=-1, keepdims=True) + EPS) * gain


def _norm_proj_kernel(x_ref, gain_ref, w_ref, o_ref, h_ref):
    @pl.when(pl.program_id(1) == 0)
    def _():
        h_ref[...] = _rms(x_ref[...], gain_ref[...]).astype(BF16)

    o_ref[...] = jnp.dot(h_ref[...], w_ref[...], preferred_element_type=F32).astype(o_ref.dtype)


def _norm_proj(x2d, gain, w, col0, ncols, out_dtype, tm, tn):
    T, D = x2d.shape
    assert T % tm == 0 and ncols % tn == 0 and col0 % tn == 0
    return pl.pallas_call(
        _norm_proj_kernel,
        grid=(T // tm, ncols // tn),
        in_specs=[pl.BlockSpec((tm, D), lambda i, j: (i, 0)),
                  pl.BlockSpec((1, D), lambda i, j: (0, 0)),
                  pl.BlockSpec((D, tn), lambda i, j: (0, j + col0 // tn))],
        out_specs=pl.BlockSpec((tm, tn), lambda i, j: (i, j)),
        out_shape=jax.ShapeDtypeStruct((T, ncols), out_dtype),
        scratch_shapes=[pltpu.VMEM((tm, D), BF16)],
        compiler_params=_params("parallel", "arbitrary"),
        name="norm_proj",
    )(x2d, gain.reshape(1, D), w)


ATTN_TQ = 128
ATTN_BLOCK = 1
ATTN_RING = 3
REGROUP_STRIDE = 4


def _attn_kernel(*refs, S, patterns):
    n_groups = len(patterns)
    in_refs, o_ref = refs[:3 * n_groups], refs[3 * n_groups]
    (stage_ref, stage2_ref, qs_ref, ks_ref, vs_ref, acc_ref, den_ref, max_ref, bias_ref, s_ref,
     m_ref) = refs[3 * n_groups + 1:]
    tq = ATTN_TQ
    n_tiles = S // tq
    hp = pl.program_id(0)
    lane = lax.broadcasted_iota(jnp.int32, (1, LANES), 1)
    first_head = lane < ATTN_HEAD_DIM
    geometry = []

    for gi, (window, d) in enumerate(patterns):
        half = window // (2 * d)
        tk = tq + 2 * half
        L = S // d
        seg = L + 2 * half
        geometry.append((d, half, tk, L // tq))
        q_in, k_in, v_in = in_refs[3 * gi:3 * gi + 3]

        @pl.when(pl.program_id(1) == 0)
        def _(gi=gi, d=d, half=half, tk=tk):
            r = lax.broadcasted_iota(jnp.int32, (tq, tk), 0)
            c = lax.broadcasted_iota(jnp.int32, (tq, tk), 1)
            rel = jnp.abs(c - half - r)
            dist = (d * rel).astype(F32)
            for hh in range(2):
                pow2 = jnp.full((tq, tk), jnp.left_shift(jnp.int32(2), 2 * hp + hh), jnp.int32).astype(F32)
                band = jnp.where(rel <= half, -(dist / pow2), MASKED)
                for kind in range(4):
                    tile_bias = band
                    if kind & 1:
                        tile_bias = jnp.where(c < half, MASKED, tile_bias)
                    if kind & 2:
                        tile_bias = jnp.where(c >= tq + half, MASKED, tile_bias)
                    bias_ref[gi, kind, hh * tq:(hh + 1) * tq, :] = tile_bias
            vs_ref[gi, :, LANES:] = jnp.ones((vs_ref.shape[1], LANES), BF16)

        zeros = jnp.zeros((half, LANES), BF16)
        if d == 1:
            ks_ref[gi, pl.ds(0, half), :] = zeros
            ks_ref[gi, pl.ds(half, S), :] = k_in[0]
            ks_ref[gi, pl.ds(half + S, half), :] = zeros
            vs_ref[gi, pl.ds(0, half), :LANES] = zeros
            vs_ref[gi, pl.ds(half, S), :LANES] = v_in[0]
            vs_ref[gi, pl.ds(half + S, half), :LANES] = zeros
        else:
            stage_ref[0] = q_in[0].astype(F32)
            stage_ref[1] = k_in[0].astype(F32)
            stage_ref[2] = v_in[0].astype(F32)
            d1 = d // REGROUP_STRIDE if d > REGROUP_STRIDE else 1
            src_ref = stage_ref
            if d1 > 1:
                assert d1 <= REGROUP_STRIDE
                part = S // REGROUP_STRIDE
                for t in range(3):
                    for r0 in range(REGROUP_STRIDE):
                        stage2_ref[t, r0 * part:(r0 + 1) * part, :] = stage_ref[t, pl.ds(r0, part, stride=REGROUP_STRIDE), :]
                src_ref = stage2_ref

            def regroup(r, carry, gi=gi, d=d, d1=d1, L=L, seg=seg, half=half, src_ref=src_ref):
                if d1 > 1:
                    rows = pl.ds((r % REGROUP_STRIDE) * (S // REGROUP_STRIDE) + r // REGROUP_STRIDE, L, stride=d1)
                else:
                    rows = pl.ds(r, L, stride=d)
                k0 = pl.multiple_of(r * seg, half)
                qs_ref[gi, pl.ds(pl.multiple_of(r * L, tq), L), :] = src_ref[0, rows, :].astype(BF16)
                ks_ref[gi, pl.ds(k0, half), :] = zeros
                ks_ref[gi, pl.ds(k0 + half, L), :] = src_ref[1, rows, :].astype(BF16)
                ks_ref[gi, pl.ds(k0 + half + L, half), :] = zeros
                vs_ref[gi, pl.ds(k0, half), :LANES] = zeros
                vs_ref[gi, pl.ds(k0 + half, L), :LANES] = src_ref[2, rows, :].astype(BF16)
                vs_ref[gi, pl.ds(k0 + half + L, half), :LANES] = zeros
                return carry

            lax.fori_loop(0, d, regroup, 0)

    def key_row(gi, u):
        d, half, tk, nblk = geometry[gi]
        return (u + u // nblk) * tq

    def scores(gi, u, slot):
        d, half, tk, nblk = geometry[gi]
        assert 2 * half == tq
        i = u % nblk
        kind = (1 if i == 0 else 0) + (2 if i == nblk - 1 else 0)
        qp = in_refs[3 * gi][0, u * tq:(u + 1) * tq, :] if d == 1 else qs_ref[gi, u * tq:(u + 1) * tq, :]
        qp = qp * (ATTN_HEAD_DIM ** -0.5)
        none = jnp.zeros_like(qp)
        q2 = jnp.concatenate([jnp.where(first_head, qp, none), jnp.where(first_head, none, qp)], axis=0)
        k0 = key_row(gi, u)
        s = lax.dot_general(q2, ks_ref[gi, k0:k0 + tk, :], (((1,), (1,)), ((), ())), preferred_element_type=F32)
        s_ref[slot] = s + bias_ref[gi, kind]

    def maxima(slot):
        m_ref[slot] = jnp.broadcast_to(jnp.max(s_ref[slot], axis=-1, keepdims=True), (2 * tq, LANES))

    def values(gi, u, slot):
        d, half, tk, nblk = geometry[gi]
        r, i = u // nblk, u % nblk
        m = m_ref[slot]
        p = jnp.concatenate([jnp.exp(s_ref[slot, :, c0:c0 + LANES] - m) for c0 in range(0, tk, LANES)], axis=1)
        k0 = key_row(gi, u)
        pv = jnp.dot(p.astype(BF16), vs_ref[gi, k0:k0 + tk, :], preferred_element_type=F32)
        rows = pl.ds(u * tq, tq) if d == 1 else pl.ds(i * (tq * d) + r, tq, stride=d)
        acc_ref[gi, rows, :] = jnp.where(first_head, pv[:tq, :LANES], pv[tq:, :LANES])
        den_ref[gi, rows, :] = jnp.where(first_head, pv[:tq, LANES:], pv[tq:, LANES:])
        max_ref[gi, rows, :] = jnp.where(first_head, m[:tq], m[tq:])

    blocks = [(gi, j) for gi in range(n_groups) for j in range(n_tiles // ATTN_BLOCK)]
    for step in range(len(blocks) + 2):
        for stage, fn in enumerate((scores, maxima, values)):
            t = step - stage
            if 0 <= t < len(blocks):
                gi, j = blocks[t]
                for k in range(ATTN_BLOCK):
                    slot = (t % ATTN_RING) * ATTN_BLOCK + k
                    if fn is maxima:
                        fn(slot)
                    else:
                        fn(gi, j * ATTN_BLOCK + k, slot)

    def merge(j, carry):
        rows = pl.ds(pl.multiple_of(j * tq, tq), tq)
        top = functools.reduce(jnp.maximum, [max_ref[g, rows, :] for g in range(n_groups)])
        w = [jnp.exp(max_ref[g, rows, :] - top) for g in range(n_groups)]
        num = sum(w[g] * acc_ref[g, rows, :] for g in range(n_groups))
        den = sum(w[g] * den_ref[g, rows, :] for g in range(n_groups))
        o_ref[0, rows, :] = (num / den).astype(o_ref.dtype)
        return carry

    lax.fori_loop(0, n_tiles, merge, 0)


def _dilated_attention(proj, col0):
    B, S, _ = proj.shape
    W = ATTN_WIDTH
    n_groups = len(ATTN_PATTERNS)
    halves = {w // (2 * d) for w, d in ATTN_PATTERNS}
    assert len(halves) == 1 and all(S % (d * ATTN_TQ) == 0 for _, d in ATTN_PATTERNS)
    half = halves.pop()
    tk = ATTN_TQ + 2 * half
    assert tk % LANES == 0 and (S // ATTN_TQ) % ATTN_BLOCK == 0
    pairs = W // LANES
    col = lambda c0: pl.BlockSpec((1, S, LANES), lambda hp, b: (b, 0, c0 // LANES + hp))
    return pl.pallas_call(
        functools.partial(_attn_kernel, S=S, patterns=ATTN_PATTERNS),
        grid=(pairs, B),
        in_specs=[col(col0 + (3 * g + t) * W) for g in range(n_groups) for t in range(3)],
        out_specs=pl.BlockSpec((1, S, LANES), lambda hp, b: (b, 0, hp)),
        out_shape=jax.ShapeDtypeStruct((B, S, W), BF16),
        scratch_shapes=[pltpu.VMEM((3, S, LANES), F32),
                        pltpu.VMEM((3, S, LANES), F32),
                        pltpu.VMEM((n_groups, S, LANES), BF16),
                        pltpu.VMEM((n_groups, 2 * S, LANES), BF16),
                        pltpu.VMEM((n_groups, 2 * S, 2 * LANES), BF16),
                        pltpu.VMEM((n_groups, S, LANES), F32),
                        pltpu.VMEM((n_groups, S, LANES), F32),
                        pltpu.VMEM((n_groups, S, LANES), F32),
                        pltpu.VMEM((n_groups, 4, 2 * ATTN_TQ, tk), F32),
                        pltpu.VMEM((ATTN_RING * ATTN_BLOCK, 2 * ATTN_TQ, tk), F32),
                        pltpu.VMEM((ATTN_RING * ATTN_BLOCK, 2 * ATTN_TQ, LANES), F32)],
        compiler_params=_params("arbitrary", "arbitrary"),
        name="dilated_attention",
    )(*([proj] * (3 * n_groups)))


def _log_sigmoid(x):
    return jnp.minimum(x, 0.0) - jnp.log1p(jnp.exp(-jnp.abs(x)))


def _retention_kernel(logit_ref, q_ref, k_ref, v_ref, gate_ref, o_ref, yf_ref, yb_ref, sf_ref, sb_ref):
    C = RET_CHUNK
    S, Dk = q_ref.shape[1], q_ref.shape[2]
    N = S // C
    h = pl.program_id(1)
    lf = _log_sigmoid(jnp.full((C, 1), logit_ref[0, h], F32))
    lb = _log_sigmoid(jnp.full((C, 1), logit_ref[1, h], F32))
    rel = (lax.broadcasted_iota(jnp.int32, (C, C), 0) - lax.broadcasted_iota(jnp.int32, (C, C), 1)).astype(F32)
    decay = jnp.exp(jnp.where(rel >= 0, lf * rel, -lb * rel))
    pos = lax.broadcasted_iota(jnp.int32, (C, 1), 0).astype(F32)
    xi_f = jnp.exp(lf * (pos + 1.0))
    zeta_f = jnp.exp(lf * (C - 1.0 - pos))
    xi_b = jnp.exp(lb * (C - pos))
    zeta_b = jnp.exp(lb * pos)
    chunk_f = jnp.exp(_log_sigmoid(jnp.full((Dk, 1), logit_ref[0, h], F32)) * C)
    chunk_b = jnp.exp(_log_sigmoid(jnp.full((Dk, 1), logit_ref[1, h], F32)) * C)
    scale = Dk ** -0.5

    def load(i):
        r0 = pl.multiple_of(i * C, C)
        return r0, q_ref[0, pl.ds(r0, C), :] * scale, k_ref[0, pl.ds(r0, C), :], v_ref[0, pl.ds(r0, C), :]

    def cross(state_ref, qc, kc, vc, xi, zeta, chunk_decay):
        st = state_ref[...]
        y = jnp.dot(qc, st.astype(BF16), preferred_element_type=F32) * xi
        kz = (kc.astype(F32) * zeta).astype(BF16)
        u = lax.dot_general(kz, vc, (((0,), (0,)), ((), ())), preferred_element_type=F32)
        state_ref[...] = st * chunk_decay + u
        return y

    sf_ref[...] = jnp.zeros_like(sf_ref)
    sb_ref[...] = jnp.zeros_like(sb_ref)

    def step(i, carry):
        r0, qc, kc, vc = load(i)
        s = lax.dot_general(qc, kc, (((1,), (1,)), ((), ())), preferred_element_type=F32) * decay
        y = jnp.dot(s.astype(BF16), vc, preferred_element_type=F32)
        yf_ref[pl.ds(r0, C), :] = y + cross(sf_ref, qc, kc, vc, xi_f, zeta_f, chunk_f)
        r1, qd, kd, vd = load(N - 1 - i)
        yb_ref[pl.ds(r1, C), :] = cross(sb_ref, qd, kd, vd, xi_b, zeta_b, chunk_b)
        return carry

    lax.fori_loop(0, N, step, 0, unroll=2)

    def finish(i, carry):
        rows = pl.ds(pl.multiple_of(i * C, C), C)
        r = yf_ref[rows, :] + yb_ref[rows, :]
        r = r * lax.rsqrt(jnp.mean(r * r, axis=-1, keepdims=True) + EPS)
        g = gate_ref[0, rows, :].astype(F32)
        o_ref[0, rows, :] = (g * _sigmoid(g) * r).astype(o_ref.dtype)
        return carry

    lax.fori_loop(0, N, finish, 0)


def _retention(proj, decay_logit, q_col0, gate_col0, width):
    B, S, _ = proj.shape
    Dk = width // RET_HEADS
    qb = q_col0 // Dk
    gb = gate_col0 // Dk
    nb = RET_HEADS
    return pl.pallas_call(
        _retention_kernel,
        grid=(B, RET_HEADS),
        in_specs=[pl.BlockSpec(memory_space=pltpu.SMEM),
                  pl.BlockSpec((1, S, Dk), lambda b, h: (b, 0, qb + h)),
                  pl.BlockSpec((1, S, Dk), lambda b, h: (b, 0, qb + nb + h)),
                  pl.BlockSpec((1, S, Dk), lambda b, h: (b, 0, qb + 2 * nb + h)),
                  pl.BlockSpec((1, S, Dk), lambda b, h: (b, 0, gb + h))],
        out_specs=pl.BlockSpec((1, S, Dk), lambda b, h: (b, 0, h)),
        out_shape=jax.ShapeDtypeStruct((B, S, RET_HEADS * Dk), BF16),
        scratch_shapes=[pltpu.VMEM((S, Dk), F32), pltpu.VMEM((S, Dk), F32),
                        pltpu.VMEM((Dk, Dk), F32), pltpu.VMEM((Dk, Dk), F32)],
        compiler_params=_params("parallel", "parallel"),
        name="retention",
    )(decay_logit, proj, proj, proj, proj)


MERGE_ROW_BLOCKS = 4


def _merge_kernel(attn_ref, ret_ref, ga0_ref, ga1_ref, gr0_ref, gr1_ref, x_ref, wa_ref, wr_ref, wo_ref, gain_ref,
                  wrt_ref, wrt_lo_ref, xo_ref, h_ref, logit_ref):
    tm = x_ref.shape[0]
    rows = tm // MERGE_ROW_BLOCKS
    for r in range(MERGE_ROW_BLOCKS):
        rs = slice(r * rows, (r + 1) * rows)
        pa = jnp.dot(attn_ref[rs, :], wa_ref[...], preferred_element_type=F32)
        pr = jnp.dot(ret_ref[rs, :], wr_ref[...], preferred_element_type=F32)
        ga = jnp.concatenate([ga0_ref[rs, :], ga1_ref[rs, :]], axis=1).astype(F32)
        gr = jnp.concatenate([gr0_ref[rs, :], gr1_ref[rs, :]], axis=1).astype(F32)
        merged = pa * _sigmoid(ga) + pr * _sigmoid(gr)
        x = x_ref[rs, :] + jnp.dot(merged.astype(BF16), wo_ref[...], preferred_element_type=F32)
        xo_ref[rs, :] = x
        h = _rms(x, gain_ref[...])
        h_hi = h.astype(BF16)
        h_ref[rs, :] = h_hi
        h_lo = (h - h_hi.astype(F32)).astype(BF16)
        logit_ref[rs, :] = (jnp.dot(h_hi, wrt_ref[...], preferred_element_type=F32)
                            + jnp.dot(h_lo, wrt_ref[...], preferred_element_type=F32)
                            + jnp.dot(h_hi, wrt_lo_ref[...], preferred_element_type=F32))


def _merge(attn, ret, gates, gate_col0, x2d, wa, wr, wo, gain, w_router, tm):
    T, D = x2d.shape
    gw = D // 2
    assert gate_col0 % gw == 0
    g0 = gate_col0 // gw
    w_router_pad = jnp.pad(w_router, ((0, 0), (0, LANES - w_router.shape[1])))
    wrt_hi = w_router_pad.astype(BF16)
    wrt_lo = (w_router_pad - wrt_hi.astype(F32)).astype(BF16)
    tok = lambda width, col=0: pl.BlockSpec((tm, width), lambda i: (i, col))
    full = lambda a: pl.BlockSpec(a.shape, lambda i: (0, 0))
    return pl.pallas_call(
        _merge_kernel,
        grid=(T // tm,),
        in_specs=[tok(ATTN_WIDTH), tok(D), tok(gw, g0), tok(gw, g0 + 1), tok(gw, g0 + 2), tok(gw, g0 + 3), tok(D),
                  full(wa), full(wr), full(wo), pl.BlockSpec((1, D), lambda i: (0, 0)), full(wrt_hi), full(wrt_lo)],
        out_specs=[tok(D), tok(D), tok(LANES)],
        out_shape=[jax.ShapeDtypeStruct((T, D), F32), jax.ShapeDtypeStruct((T, D), BF16),
                   jax.ShapeDtypeStruct((T, LANES), F32)],
        compiler_params=_params("parallel"),
        name="merge_out_proj",
    )(attn, ret, gates, gates, gates, gates, x2d, wa, wr, wo, gain.reshape(1, D), wrt_hi, wrt_lo)


def _route_kernel(logit_ref, pos_ref, pos_t_ref, aff_ref, before_ref, *, n_experts, cap):
    S = logit_ref.shape[1]
    rows = 256

    @pl.when(pl.program_id(0) == 0)
    def _():
        def fill(i, carry):
            r0 = pl.multiple_of(i * rows, rows)
            r = lax.broadcasted_iota(jnp.int32, (rows, S), 0) + r0
            c = lax.broadcasted_iota(jnp.int32, (rows, S), 1)
            before_ref[pl.ds(r0, rows), :] = jnp.where(r < c, 1.0, 0.0).astype(BF16)
            return carry
        lax.fori_loop(0, S // rows, fill, 0)

    lane = lax.broadcasted_iota(jnp.int32, (1, LANES), 1)
    lg = jnp.where(lane < n_experts, logit_ref[0], MASKED)
    ex = jnp.exp(lg - jnp.max(lg, axis=-1, keepdims=True))
    aff = (ex / jnp.sum(ex, axis=-1, keepdims=True)).T[:n_experts]

    def search(it, thr_bits):
        cand = thr_bits | jnp.left_shift(jnp.int32(1), 30 - it)
        cnt = jnp.sum((aff >= lax.bitcast_convert_type(cand, F32)).astype(jnp.int32), axis=-1, keepdims=True)
        return jnp.where(cnt >= cap, cand, thr_bits)

    floor = lax.bitcast_convert_type(lax.fori_loop(0, 31, search, jnp.zeros((n_experts, 1), jnp.int32)), F32)
    thr = jnp.min(jnp.where(aff >= floor, aff, jnp.inf), axis=-1, keepdims=True)
    above = aff > thr
    tied = aff == thr
    need = (cap - jnp.sum(above.astype(jnp.int32), axis=-1, keepdims=True)).astype(F32)
    count_before = lambda mask: jnp.dot(jnp.where(mask, 1.0, 0.0).astype(BF16), before_ref[...],
                                        preferred_element_type=F32)
    chosen = above | (tied & (count_before(tied) < need))
    pos = jnp.where(chosen, count_before(chosen), -1.0)
    pos_ref[0] = pos
    pos_t_ref[0] = jnp.concatenate([pos, jnp.full((LANES - n_experts, S), -1.0, F32)], axis=0).T
    aff_ref[0] = aff


def _route(logits, cap):
    B, S, _ = logits.shape
    E = N_EXPERTS
    return pl.pallas_call(
        functools.partial(_route_kernel, n_experts=E, cap=cap),
        grid=(B,),
        in_specs=[pl.BlockSpec((1, S, LANES), lambda b: (b, 0, 0))],
        out_specs=[pl.BlockSpec((1, E, S), lambda b: (b, 0, 0)), pl.BlockSpec((1, S, LANES), lambda b: (b, 0, 0)),
                   pl.BlockSpec((1, E, S), lambda b: (b, 0, 0))],
        out_shape=[jax.ShapeDtypeStruct((B, E, S), F32), jax.ShapeDtypeStruct((B, S, LANES), F32),
                   jax.ShapeDtypeStruct((B, E, S), F32)],
        scratch_shapes=[pltpu.VMEM((S, S), BF16)],
        compiler_params=_params("arbitrary"),
        name="route",
    )(logits)


def _gather_kernel(h_ref, pos_ref, aff_ref, x_ref, g_ref, *, n_experts, cap):
    S = h_ref.shape[1]
    slot = lax.broadcasted_iota(jnp.int32, (cap, S), 0)
    for e in range(n_experts):
        mask = slot == pos_ref[0, e:e + 1, :].astype(jnp.int32)
        onehot = jnp.where(mask, 1.0, 0.0).astype(BF16)
        x_ref[e, 0] = jnp.dot(onehot, h_ref[0], preferred_element_type=F32).astype(x_ref.dtype)
        g = jnp.sum(jnp.where(mask, aff_ref[0, e:e + 1, :], 0.0), axis=-1, keepdims=True)
        g_ref[e, 0] = jnp.broadcast_to(g, (cap, LANES))


def _gather(h, pos, aff, cap):
    B, S, D = h.shape
    E = pos.shape[1]
    rows = pl.BlockSpec((1, E, S), lambda b: (b, 0, 0))
    return pl.pallas_call(
        functools.partial(_gather_kernel, n_experts=E, cap=cap),
        grid=(B,),
        in_specs=[pl.BlockSpec((1, S, D), lambda b: (b, 0, 0)), rows, rows],
        out_specs=[pl.BlockSpec((E, 1, cap, D), lambda b: (0, b, 0, 0)),
                   pl.BlockSpec((E, 1, cap, LANES), lambda b: (0, b, 0, 0))],
        out_shape=[jax.ShapeDtypeStruct((E, B, cap, D), BF16), jax.ShapeDtypeStruct((E, B, cap, LANES), F32)],
        compiler_params=_params("parallel"),
        name="expert_gather",
    )(h, pos, aff)


def _expert_kernel(x_ref, g_ref, wg_ref, wu_ref, wd_ref, y_ref, act_ref, wd_bf_ref, *, nf, tf):
    f = pl.program_id(1)
    x = x_ref[0]
    a = jnp.dot(x, wg_ref[...].astype(BF16), preferred_element_type=F32)
    u = jnp.dot(x, wu_ref[...].astype(BF16), preferred_element_type=F32)
    act = (a * _sigmoid(a) * u).astype(BF16)
    wd_bf_ref[pl.ds(pl.multiple_of(f * tf, tf), tf), :] = wd_ref[...].astype(BF16)
    for k in range(nf):
        @pl.when(f == k)
        def _(k=k):
            act_ref[:, k * tf:(k + 1) * tf] = act

    @pl.when(f == nf - 1)
    def _():
        y = jnp.dot(act_ref[...], wd_bf_ref[...], preferred_element_type=F32)
        y_ref[0] = (y * g_ref[0][:, :1]).astype(y_ref.dtype)


def _experts(xin, g, w_gate, w_up, w_down, layer, tf):
    E, M, D = xin.shape
    FF = w_gate.shape[3]
    return pl.pallas_call(
        functools.partial(_expert_kernel, nf=FF // tf, tf=tf),
        grid=(E, FF // tf),
        in_specs=[pl.BlockSpec((1, M, D), lambda e, f: (e, 0, 0)),
                  pl.BlockSpec((1, M, LANES), lambda e, f: (e, 0, 0)),
                  pl.BlockSpec((None, None, D, tf), lambda e, f: (layer, e, 0, f)),
                  pl.BlockSpec((None, None, D, tf), lambda e, f: (layer, e, 0, f)),
                  pl.BlockSpec((None, None, tf, D), lambda e, f: (layer, e, f, 0))],
        out_specs=pl.BlockSpec((1, M, D), lambda e, f: (e, 0, 0)),
        out_shape=jax.ShapeDtypeStruct((E, M, D), BF16),
        scratch_shapes=[pltpu.VMEM((M, FF), BF16), pltpu.VMEM((FF, D), BF16)],
        compiler_params=_params("parallel", "arbitrary"),
        name="expert_swiglu",
    )(xin, g, w_gate, w_up, w_down)


def _scatter_kernel(x_ref, pos_t_ref, y_ref, gain_ref, o_ref, *, n_experts, cap, normalize):
    pos = pos_t_ref[0].astype(jnp.int32)
    slot = lax.broadcasted_iota(jnp.int32, (pos.shape[0], cap), 1)
    onehot = jnp.concatenate([jnp.where(pos[:, e:e + 1] == slot, 1.0, 0.0).astype(BF16) for e in range(n_experts)],
                             axis=1)
    y = y_ref[:, 0].reshape(n_experts * cap, y_ref.shape[-1])
    out = x_ref[0] + jnp.dot(onehot, y, preferred_element_type=F32)
    o_ref[0] = _rms(out, gain_ref[...]) if normalize else out


def _scatter(x, pos_t, y, cap, ts, gain, normalize):
    B, S, D = x.shape
    E = y.shape[0]
    return pl.pallas_call(
        functools.partial(_scatter_kernel, n_experts=E, cap=cap, normalize=normalize),
        grid=(B, S // ts),
        in_specs=[pl.BlockSpec((1, ts, D), lambda b, t: (b, t, 0)),
                  pl.BlockSpec((1, ts, LANES), lambda b, t: (b, t, 0)),
                  pl.BlockSpec((E, 1, cap, D), lambda b, t: (0, b, 0, 0)),
                  pl.BlockSpec((1, D), lambda b, t: (0, 0))],
        out_specs=pl.BlockSpec((1, ts, D), lambda b, t: (b, t, 0)),
        out_shape=jax.ShapeDtypeStruct((B, S, D), F32),
        compiler_params=_params("parallel", "arbitrary"),
        name="expert_scatter",
    )(x, pos_t, y, gain.reshape(1, D))


def kernel(x, w_in, w_attn_out, w_ret_out, w_out, ret_decay_logit, norm_mix, norm_ffn, w_router, w_gate, w_up,
           w_down, norm_final):
    B, S, D = x.shape
    T = B * S
    depth = w_in.shape[0]
    assert depth >= 1
    W = ATTN_WIDTH
    attn_in = len(ATTN_PATTERNS) * 3 * W
    n_in = w_in.shape[2]
    ret0, swish0, gates0 = attn_in, attn_in + 3 * D, attn_in + 4 * D
    cap = CAPACITY_FACTOR * S // N_EXPERTS
    x2d = x.reshape(T, D)
    for layer in range(depth):
        proj = _norm_proj(x2d, norm_mix[layer], w_in[layer].astype(BF16), 0, n_in, BF16, tm=2048, tn=1536)
        proj3 = proj.reshape(B, S, n_in)

        attn = _dilated_attention(proj3, 0)
        ret = _retention(proj3, ret_decay_logit[layer], ret0, swish0, D)

        x2d, h2, logits = _merge(attn.reshape(T, W), ret.reshape(T, D), proj, gates0, x2d,
                                 w_attn_out[layer].astype(BF16),
                                 w_ret_out[layer].astype(BF16), w_out[layer].astype(BF16), norm_ffn[layer],
                                 w_router[layer], tm=1024)

        pos, pos_t, aff = _route(logits.reshape(B, S, LANES), cap)
        xin, g = _gather(h2.reshape(B, S, D), pos, aff, cap)
        y = _experts(xin.reshape(N_EXPERTS, B * cap, D), g.reshape(N_EXPERTS, B * cap, LANES),
                     w_gate, w_up, w_down, layer, tf=256)
        x2d = _scatter(x2d.reshape(B, S, D), pos_t, y.reshape(N_EXPERTS, B, cap, D), cap, 1024, norm_final,
                       normalize=layer == depth - 1).reshape(T, D)
    return x2d.reshape(B, S, D)
```

```python
import functools

import jax
import jax.numpy as jnp
from jax import lax
from jax.experimental import pallas as pl
from jax.experimental.pallas import tpu as pltpu

EPS = 1e-6
ATTN_PATTERNS = ((128, 1), (512, 4), (2048, 16))
ATTN_HEADS = 8
ATTN_HEAD_DIM = 64
ATTN_WIDTH = ATTN_HEADS * ATTN_HEAD_DIM
RET_HEADS = 4
RET_CHUNK = 256
N_EXPERTS = 16
CAPACITY_FACTOR = 2

LANES = 128
MASKED = -1e30
VMEM_LIMIT_BYTES = 56 * 1024 * 1024

F32 = jnp.float32
BF16 = jnp.bfloat16


def _params(*semantics):
    return pltpu.CompilerParams(dimension_semantics=semantics, vmem_limit_bytes=VMEM_LIMIT_BYTES)


def _sigmoid(x):
    return 0.5 * (1.0 + jnp.tanh(0.5 * x))


def _rms(x, gain):
    return x * lax.rsqrt(jnp.mean(x * x, axis=-1, keepdims=True) + EPS) * gain


def _norm_proj_kernel(x_ref, gain_ref, w_ref, o_ref, h_ref):
    @pl.when(pl.program_id(1) == 0)
    def _():
        h_ref[...] = _rms(x_ref[...], gain_ref[...]).astype(BF16)

    o_ref[...] = jnp.dot(h_ref[...], w_ref[...], preferred_element_type=F32).astype(o_ref.dtype)


def _norm_proj(x2d, gain, w, layer, col0, ncols, out_dtype, tm, tn):
    T, D = x2d.shape
    assert T % tm == 0 and ncols % tn == 0 and col0 % tn == 0
    return pl.pallas_call(
        _norm_proj_kernel,
        grid=(T // tm, ncols // tn),
        in_specs=[pl.BlockSpec((tm, D), lambda i, j: (i, 0)),
                  pl.BlockSpec((1, D), lambda i, j: (0, 0)),
                  pl.BlockSpec((None, D, tn), lambda i, j: (layer, 0, j + col0 // tn))],
        out_specs=pl.BlockSpec((tm, tn), lambda i, j: (i, j)),
        out_shape=jax.ShapeDtypeStruct((T, ncols), out_dtype),
        scratch_shapes=[pltpu.VMEM((tm, D), BF16)],
        compiler_params=_params("parallel", "arbitrary"),
        name="norm_proj",
    )(x2d, gain.reshape(1, D), w)


ATTN_TQ = 128
ATTN_BLOCK = 1
ATTN_RING = 3
REGROUP_STRIDE = 4


def _attn_kernel(*refs, S, patterns):
    n_groups = len(patterns)
    in_refs, o_ref = refs[:3 * n_groups], refs[3 * n_groups]
    (stage_ref, stage2_ref, qs_ref, ks_ref, vs_ref, acc_ref, den_ref, max_ref, bias_ref, s_ref,
     m_ref) = refs[3 * n_groups + 1:]
    tq = ATTN_TQ
    n_tiles = S // tq
    hp = pl.program_id(0)
    lane = lax.broadcasted_iota(jnp.int32, (1, LANES), 1)
    first_head = lane < ATTN_HEAD_DIM
    geometry = []

    for gi, (window, d) in enumerate(patterns):
        half = window // (2 * d)
        tk = tq + 2 * half
        L = S // d
        seg = L + 2 * half
        geometry.append((d, half, tk, L // tq))
        q_in, k_in, v_in = in_refs[3 * gi:3 * gi + 3]

        @pl.when(pl.program_id(1) == 0)
        def _(gi=gi, d=d, half=half, tk=tk):
            r = lax.broadcasted_iota(jnp.int32, (tq, tk), 0)
            c = lax.broadcasted_iota(jnp.int32, (tq, tk), 1)
            rel = jnp.abs(c - half - r)
            dist = (d * rel).astype(F32)
            for hh in range(2):
                pow2 = jnp.full((tq, tk), jnp.left_shift(jnp.int32(2), 2 * hp + hh), jnp.int32).astype(F32)
                band = jnp.where(rel <= half, -(dist / pow2), MASKED)
                for kind in range(4):
                    tile_bias = band
                    if kind & 1:
                        tile_bias = jnp.where(c < half, MASKED, tile_bias)
                    if kind & 2:
                        tile_bias = jnp.where(c >= tq + half, MASKED, tile_bias)
                    bias_ref[gi, kind, hh * tq:(hh + 1) * tq, :] = tile_bias
            vs_ref[gi, :, LANES:] = jnp.ones((vs_ref.shape[1], LANES), BF16)

        zeros = jnp.zeros((half, LANES), BF16)
        if d == 1:
            ks_ref[gi, pl.ds(0, half), :] = zeros
            ks_ref[gi, pl.ds(half, S), :] = k_in[0]
            ks_ref[gi, pl.ds(half + S, half), :] = zeros
            vs_ref[gi, pl.ds(0, half), :LANES] = zeros
            vs_ref[gi, pl.ds(half, S), :LANES] = v_in[0]
            vs_ref[gi, pl.ds(half + S, half), :LANES] = zeros
        else:
            stage_ref[0] = q_in[0].astype(F32)
            stage_ref[1] = k_in[0].astype(F32)
            stage_ref[2] = v_in[0].astype(F32)
            d1 = d // REGROUP_STRIDE if d > REGROUP_STRIDE else 1
            src_ref = stage_ref
            if d1 > 1:
                assert d1 <= REGROUP_STRIDE
                part = S // REGROUP_STRIDE
                for t in range(3):
                    for r0 in range(REGROUP_STRIDE):
                        stage2_ref[t, r0 * part:(r0 + 1) * part, :] = stage_ref[t, pl.ds(r0, part, stride=REGROUP_STRIDE), :]
                src_ref = stage2_ref

            def regroup(r, carry, gi=gi, d=d, d1=d1, L=L, seg=seg, half=half, src_ref=src_ref):
                if d1 > 1:
                    rows = pl.ds((r % REGROUP_STRIDE) * (S // REGROUP_STRIDE) + r // REGROUP_STRIDE, L, stride=d1)
                else:
                    rows = pl.ds(r, L, stride=d)
                k0 = pl.multiple_of(r * seg, half)
                qs_ref[gi, pl.ds(pl.multiple_of(r * L, tq), L), :] = src_ref[0, rows, :].astype(BF16)
                ks_ref[gi, pl.ds(k0, half), :] = zeros
                ks_ref[gi, pl.ds(k0 + half, L), :] = src_ref[1, rows, :].astype(BF16)
                ks_ref[gi, pl.ds(k0 + half + L, half), :] = zeros
                vs_ref[gi, pl.ds(k0, half), :LANES] = zeros
                vs_ref[gi, pl.ds(k0 + half, L), :LANES] = src_ref[2, rows, :].astype(BF16)
                vs_ref[gi, pl.ds(k0 + half + L, half), :LANES] = zeros
                return carry

            lax.fori_loop(0, d, regroup, 0)

    def key_row(gi, u):
        d, half, tk, nblk = geometry[gi]
        return (u + u // nblk) * tq

    def scores(gi, u, slot):
        d, half, tk, nblk = geometry[gi]
        assert 2 * half == tq
        i = u % nblk
        kind = (1 if i == 0 else 0) + (2 if i == nblk - 1 else 0)
        qp = in_refs[3 * gi][0, u * tq:(u + 1) * tq, :] if d == 1 else qs_ref[gi, u * tq:(u + 1) * tq, :]
        qp = qp * (ATTN_HEAD_DIM ** -0.5)
        none = jnp.zeros_like(qp)
        q2 = jnp.concatenate([jnp.where(first_head, qp, none), jnp.where(first_head, none, qp)], axis=0)
        k0 = key_row(gi, u)
        s = lax.dot_general(q2, ks_ref[gi, k0:k0 + tk, :], (((1,), (1,)), ((), ())), preferred_element_type=F32)
        s_ref[slot] = s + bias_ref[gi, kind]

    def maxima(slot):
        m_ref[slot] = jnp.broadcast_to(jnp.max(s_ref[slot], axis=-1, keepdims=True), (2 * tq, LANES))

    def values(gi, u, slot):
        d, half, tk, nblk = geometry[gi]
        r, i = u // nblk, u % nblk
        m = m_ref[slot]
        p = jnp.concatenate([jnp.exp(s_ref[slot, :, c0:c0 + LANES] - m) for c0 in range(0, tk, LANES)], axis=1)
        k0 = key_row(gi, u)
        pv = jnp.dot(p.astype(BF16), vs_ref[gi, k0:k0 + tk, :], preferred_element_type=F32)
        rows = pl.ds(u * tq, tq) if d == 1 else pl.ds(i * (tq * d) + r, tq, stride=d)
        acc_ref[gi, rows, :] = jnp.where(first_head, pv[:tq, :LANES], pv[tq:, :LANES])
        den_ref[gi, rows, :] = jnp.where(first_head, pv[:tq, LANES:], pv[tq:, LANES:])
        max_ref[gi, rows, :] = jnp.where(first_head, m[:tq], m[tq:])

    blocks = [(gi, j) for gi in range(n_groups) for j in range(n_tiles // ATTN_BLOCK)]
    for step in range(len(blocks) + 2):
        for stage, fn in enumerate((scores, maxima, values)):
            t = step - stage
            if 0 <= t < len(blocks):
                gi, j = blocks[t]
                for k in range(ATTN_BLOCK):
                    slot = (t % ATTN_RING) * ATTN_BLOCK + k
                    if fn is maxima:
                        fn(slot)
                    else:
                        fn(gi, j * ATTN_BLOCK + k, slot)

    def merge(j, carry):
        rows = pl.ds(pl.multiple_of(j * tq, tq), tq)
        top = functools.reduce(jnp.maximum, [max_ref[g, rows, :] for g in range(n_groups)])
        w = [jnp.exp(max_ref[g, rows, :] - top) for g in range(n_groups)]
        num = sum(w[g] * acc_ref[g, rows, :] for g in range(n_groups))
        den = sum(w[g] * den_ref[g, rows, :] for g in range(n_groups))
        o_ref[0, rows, :] = (num / den).astype(o_ref.dtype)
        return carry

    lax.fori_loop(0, n_tiles, merge, 0)


def _dilated_attention(proj, col0):
    B, S, _ = proj.shape
    W = ATTN_WIDTH
    n_groups = len(ATTN_PATTERNS)
    halves = {w // (2 * d) for w, d in ATTN_PATTERNS}
    assert len(halves) == 1 and all(S % (d * ATTN_TQ) == 0 for _, d in ATTN_PATTERNS)
    half = halves.pop()
    tk = ATTN_TQ + 2 * half
    assert tk % LANES == 0 and (S // ATTN_TQ) % ATTN_BLOCK == 0
    pairs = W // LANES
    col = lambda c0: pl.BlockSpec((1, S, LANES), lambda hp, b: (b, 0, c0 // LANES + hp))
    return pl.pallas_call(
        functools.partial(_attn_kernel, S=S, patterns=ATTN_PATTERNS),
        grid=(pairs, B),
        in_specs=[col(col0 + (3 * g + t) * W) for g in range(n_groups) for t in range(3)],
        out_specs=pl.BlockSpec((1, S, LANES), lambda hp, b: (b, 0, hp)),
        out_shape=jax.ShapeDtypeStruct((B, S, W), BF16),
        scratch_shapes=[pltpu.VMEM((3, S, LANES), F32),
                        pltpu.VMEM((3, S, LANES), F32),
                        pltpu.VMEM((n_groups, S, LANES), BF16),
                        pltpu.VMEM((n_groups, 2 * S, LANES), BF16),
                        pltpu.VMEM((n_groups, 2 * S, 2 * LANES), BF16),
                        pltpu.VMEM((n_groups, S, LANES), F32),
                        pltpu.VMEM((n_groups, S, LANES), F32),
                        pltpu.VMEM((n_groups, S, LANES), F32),
                        pltpu.VMEM((n_groups, 4, 2 * ATTN_TQ, tk), F32),
                        pltpu.VMEM((ATTN_RING * ATTN_BLOCK, 2 * ATTN_TQ, tk), F32),
                        pltpu.VMEM((ATTN_RING * ATTN_BLOCK, 2 * ATTN_TQ, LANES), F32)],
        compiler_params=_params("arbitrary", "arbitrary"),
        name="dilated_attention",
    )(*([proj] * (3 * n_groups)))


def _log_sigmoid(x):
    return jnp.minimum(x, 0.0) - jnp.log1p(jnp.exp(-jnp.abs(x)))


def _retention_kernel(logit_ref, q_ref, k_ref, v_ref, gate_ref, o_ref, yf_ref, yb_ref, sf_ref, sb_ref):
    C = RET_CHUNK
    S, Dk = q_ref.shape[1], q_ref.shape[2]
    N = S // C
    h = pl.program_id(1)
    lf = _log_sigmoid(jnp.full((C, 1), logit_ref[0, h], F32))
    lb = _log_sigmoid(jnp.full((C, 1), logit_ref[1, h], F32))
    rel = (lax.broadcasted_iota(jnp.int32, (C, C), 0) - lax.broadcasted_iota(jnp.int32, (C, C), 1)).astype(F32)
    decay = jnp.exp(jnp.where(rel >= 0, lf * rel, -lb * rel))
    pos = lax.broadcasted_iota(jnp.int32, (C, 1), 0).astype(F32)
    xi_f = jnp.exp(lf * (pos + 1.0))
    zeta_f = jnp.exp(lf * (C - 1.0 - pos))
    xi_b = jnp.exp(lb * (C - pos))
    zeta_b = jnp.exp(lb * pos)
    chunk_f = jnp.exp(_log_sigmoid(jnp.full((Dk, 1), logit_ref[0, h], F32)) * C)
    chunk_b = jnp.exp(_log_sigmoid(jnp.full((Dk, 1), logit_ref[1, h], F32)) * C)
    scale = Dk ** -0.5

    def load(i):
        r0 = pl.multiple_of(i * C, C)
        return r0, q_ref[0, pl.ds(r0, C), :] * scale, k_ref[0, pl.ds(r0, C), :], v_ref[0, pl.ds(r0, C), :]

    def cross(state_ref, qc, kc, vc, xi, zeta, chunk_decay):
        st = state_ref[...]
        y = jnp.dot(qc, st.astype(BF16), preferred_element_type=F32) * xi
        kz = (kc.astype(F32) * zeta).astype(BF16)
        u = lax.dot_general(kz, vc, (((0,), (0,)), ((), ())), preferred_element_type=F32)
        state_ref[...] = st * chunk_decay + u
        return y

    sf_ref[...] = jnp.zeros_like(sf_ref)
    sb_ref[...] = jnp.zeros_like(sb_ref)

    def finish(rows, r):
        r = r * lax.rsqrt(jnp.mean(r * r, axis=-1, keepdims=True) + EPS)
        g = gate_ref[0, rows, :].astype(F32)
        o_ref[0, rows, :] = (g * _sigmoid(g) * r).astype(o_ref.dtype)

    def both(i):
        r0, qc, kc, vc = load(i)
        s = lax.dot_general(qc, kc, (((1,), (1,)), ((), ())), preferred_element_type=F32) * decay
        yf = jnp.dot(s.astype(BF16), vc, preferred_element_type=F32) + cross(sf_ref, qc, kc, vc, xi_f, zeta_f, chunk_f)
        r1, qd, kd, vd = load(N - 1 - i)
        yb = cross(sb_ref, qd, kd, vd, xi_b, zeta_b, chunk_b)
        return pl.ds(r0, C), yf, pl.ds(r1, C), yb

    def first_half(i, carry):
        rows_f, yf, rows_b, yb = both(i)
        yf_ref[rows_f, :] = yf
        yb_ref[rows_b, :] = yb
        return carry

    def second_half(i, carry):
        rows_f, yf, rows_b, yb = both(i)
        finish(rows_f, yf + yb_ref[rows_f, :])
        finish(rows_b, yf_ref[rows_b, :] + yb)
        return carry

    assert N % 2 == 0
    lax.fori_loop(0, N // 2, first_half, 0, unroll=2)
    lax.fori_loop(N // 2, N, second_half, 0, unroll=2)


def _retention(proj, decay_logit, q_col0, gate_col0, width):
    B, S, _ = proj.shape
    Dk = width // RET_HEADS
    qb = q_col0 // Dk
    gb = gate_col0 // Dk
    nb = RET_HEADS
    return pl.pallas_call(
        _retention_kernel,
        grid=(B, RET_HEADS),
        in_specs=[pl.BlockSpec(memory_space=pltpu.SMEM),
                  pl.BlockSpec((1, S, Dk), lambda b, h: (b, 0, qb + h)),
                  pl.BlockSpec((1, S, Dk), lambda b, h: (b, 0, qb + nb + h)),
                  pl.BlockSpec((1, S, Dk), lambda b, h: (b, 0, qb + 2 * nb + h)),
                  pl.BlockSpec((1, S, Dk), lambda b, h: (b, 0, gb + h))],
        out_specs=pl.BlockSpec((1, S, Dk), lambda b, h: (b, 0, h)),
        out_shape=jax.ShapeDtypeStruct((B, S, RET_HEADS * Dk), BF16),
        scratch_shapes=[pltpu.VMEM((S, Dk), F32), pltpu.VMEM((S, Dk), F32),
                        pltpu.VMEM((Dk, Dk), F32), pltpu.VMEM((Dk, Dk), F32)],
        compiler_params=_params("parallel", "parallel"),
        name="retention",
    )(decay_logit, proj, proj, proj, proj)


MERGE_ROW_BLOCKS = 4


def _merge_kernel(attn_ref, ret_ref, ga0_ref, ga1_ref, gr0_ref, gr1_ref, x_ref, wa_ref, wr_ref, wo_ref, gain_ref,
                  wrt_ref, wrt_lo_ref, xo_ref, h_ref, logit_ref):
    tm = x_ref.shape[0]
    rows = tm // MERGE_ROW_BLOCKS
    for r in range(MERGE_ROW_BLOCKS):
        rs = slice(r * rows, (r + 1) * rows)
        pa = jnp.dot(attn_ref[rs, :], wa_ref[...], preferred_element_type=F32)
        pr = jnp.dot(ret_ref[rs, :], wr_ref[...], preferred_element_type=F32)
        ga = jnp.concatenate([ga0_ref[rs, :], ga1_ref[rs, :]], axis=1).astype(F32)
        gr = jnp.concatenate([gr0_ref[rs, :], gr1_ref[rs, :]], axis=1).astype(F32)
        merged = pa * _sigmoid(ga) + pr * _sigmoid(gr)
        x = x_ref[rs, :] + jnp.dot(merged.astype(BF16), wo_ref[...], preferred_element_type=F32)
        xo_ref[rs, :] = x
        h = _rms(x, gain_ref[...])
        h_hi = h.astype(BF16)
        h_ref[rs, :] = h_hi
        h_lo = (h - h_hi.astype(F32)).astype(BF16)
        logit_ref[rs, :] = (jnp.dot(h_hi, wrt_ref[...], preferred_element_type=F32)
                            + jnp.dot(h_lo, wrt_ref[...], preferred_element_type=F32)
                            + jnp.dot(h_hi, wrt_lo_ref[...], preferred_element_type=F32))


def _merge(attn, ret, gates, gate_col0, x2d, wa, wr, wo, gain, w_router, tm):
    T, D = x2d.shape
    gw = D // 2
    assert gate_col0 % gw == 0
    g0 = gate_col0 // gw
    w_router_pad = jnp.pad(w_router, ((0, 0), (0, LANES - w_router.shape[1])))
    wrt_hi = w_router_pad.astype(BF16)
    wrt_lo = (w_router_pad - wrt_hi.astype(F32)).astype(BF16)
    tok = lambda width, col=0: pl.BlockSpec((tm, width), lambda i: (i, col))
    full = lambda a: pl.BlockSpec(a.shape, lambda i: (0, 0))
    return pl.pallas_call(
        _merge_kernel,
        grid=(T // tm,),
        in_specs=[tok(ATTN_WIDTH), tok(D), tok(gw, g0), tok(gw, g0 + 1), tok(gw, g0 + 2), tok(gw, g0 + 3), tok(D),
                  full(wa), full(wr), full(wo), pl.BlockSpec((1, D), lambda i: (0, 0)), full(wrt_hi), full(wrt_lo)],
        out_specs=[tok(D), tok(D), tok(LANES)],
        out_shape=[jax.ShapeDtypeStruct((T, D), F32), jax.ShapeDtypeStruct((T, D), BF16),
                   jax.ShapeDtypeStruct((T, LANES), F32)],
        compiler_params=_params("parallel"),
        name="merge_out_proj",
    )(attn, ret, gates, gates, gates, gates, x2d, wa, wr, wo, gain.reshape(1, D), wrt_hi, wrt_lo)


def _route_kernel(logit_ref, pos_ref, pos_t_ref, aff_ref, before_ref, *, n_experts, cap):
    S = logit_ref.shape[1]
    rows = 256

    @pl.when(pl.program_id(0) == 0)
    def _():
        def fill(i, carry):
            r0 = pl.multiple_of(i * rows, rows)
            r = lax.broadcasted_iota(jnp.int32, (rows, S), 0) + r0
            c = lax.broadcasted_iota(jnp.int32, (rows, S), 1)
            before_ref[pl.ds(r0, rows), :] = jnp.where(r < c, 1.0, 0.0).astype(BF16)
            return carry
        lax.fori_loop(0, S // rows, fill, 0)

    lane = lax.broadcasted_iota(jnp.int32, (1, LANES), 1)
    lg = jnp.where(lane < n_experts, logit_ref[0], MASKED)
    ex = jnp.exp(lg - jnp.max(lg, axis=-1, keepdims=True))
    aff = (ex / jnp.sum(ex, axis=-1, keepdims=True)).T[:n_experts]

    def search(it, thr_bits):
        cand = thr_bits | jnp.left_shift(jnp.int32(1), 30 - it)
        cnt = jnp.sum((aff >= lax.bitcast_convert_type(cand, F32)).astype(jnp.int32), axis=-1, keepdims=True)
        return jnp.where(cnt >= cap, cand, thr_bits)

    floor = lax.bitcast_convert_type(lax.fori_loop(0, 31, search, jnp.zeros((n_experts, 1), jnp.int32)), F32)
    thr = jnp.min(jnp.where(aff >= floor, aff, jnp.inf), axis=-1, keepdims=True)
    above = aff > thr
    tied = aff == thr
    need = (cap - jnp.sum(above.astype(jnp.int32), axis=-1, keepdims=True)).astype(F32)
    count_before = lambda mask: jnp.dot(jnp.where(mask, 1.0, 0.0).astype(BF16), before_ref[...],
                                        preferred_element_type=F32)
    chosen = above | (tied & (count_before(tied) < need))
    pos = jnp.where(chosen, count_before(chosen), -1.0)
    pos_ref[0] = pos
    pos_t_ref[0] = jnp.concatenate([pos, jnp.full((LANES - n_experts, S), -1.0, F32)], axis=0).T
    aff_ref[0] = aff


def _route(logits, cap):
    B, S, _ = logits.shape
    E = N_EXPERTS
    return pl.pallas_call(
        functools.partial(_route_kernel, n_experts=E, cap=cap),
        grid=(B,),
        in_specs=[pl.BlockSpec((1, S, LANES), lambda b: (b, 0, 0))],
        out_specs=[pl.BlockSpec((1, E, S), lambda b: (b, 0, 0)), pl.BlockSpec((1, S, LANES), lambda b: (b, 0, 0)),
                   pl.BlockSpec((1, E, S), lambda b: (b, 0, 0))],
        out_shape=[jax.ShapeDtypeStruct((B, E, S), F32), jax.ShapeDtypeStruct((B, S, LANES), F32),
                   jax.ShapeDtypeStruct((B, E, S), F32)],
        scratch_shapes=[pltpu.VMEM((S, S), BF16)],
        compiler_params=_params("arbitrary"),
        name="route",
    )(logits)


def _gather_kernel(h_ref, pos_ref, aff_ref, x_ref, g_ref, *, n_experts, cap):
    S = h_ref.shape[1]
    slot = lax.broadcasted_iota(jnp.int32, (cap, S), 0)
    for e in range(n_experts):
        mask = slot == pos_ref[0, e:e + 1, :].astype(jnp.int32)
        onehot = jnp.where(mask, 1.0, 0.0).astype(BF16)
        x_ref[e, 0] = jnp.dot(onehot, h_ref[0], preferred_element_type=F32).astype(x_ref.dtype)
        g = jnp.sum(jnp.where(mask, aff_ref[0, e:e + 1, :], 0.0), axis=-1, keepdims=True)
        g_ref[e, 0] = jnp.broadcast_to(g, (cap, LANES))


def _gather(h, pos, aff, cap):
    B, S, D = h.shape
    E = pos.shape[1]
    rows = pl.BlockSpec((1, E, S), lambda b: (b, 0, 0))
    return pl.pallas_call(
        functools.partial(_gather_kernel, n_experts=E, cap=cap),
        grid=(B,),
        in_specs=[pl.BlockSpec((1, S, D), lambda b: (b, 0, 0)), rows, rows],
        out_specs=[pl.BlockSpec((E, 1, cap, D), lambda b: (0, b, 0, 0)),
                   pl.BlockSpec((E, 1, cap, LANES), lambda b: (0, b, 0, 0))],
        out_shape=[jax.ShapeDtypeStruct((E, B, cap, D), BF16), jax.ShapeDtypeStruct((E, B, cap, LANES), F32)],
        compiler_params=_params("parallel"),
        name="expert_gather",
    )(h, pos, aff)


EXPERT_ROW_BLOCKS = 4


def _expert_kernel(x_ref, g_ref, wg_ref, wu_ref, wd_ref, y_ref, act_ref, wd_bf_ref, chunk_ref, *, nf, tf):
    f = pl.program_id(1)
    wg = wg_ref[...].astype(BF16)
    wu = wu_ref[...].astype(BF16)
    wd_bf_ref[pl.ds(pl.multiple_of(f * tf, tf), tf), :] = wd_ref[...].astype(BF16)
    rows = x_ref.shape[1] // EXPERT_ROW_BLOCKS
    for r in range(EXPERT_ROW_BLOCKS):
        x = x_ref[0, r * rows:(r + 1) * rows, :]
        a = jnp.dot(x, wg, preferred_element_type=F32)
        u = jnp.dot(x, wu, preferred_element_type=F32)
        chunk_ref[r * rows:(r + 1) * rows, :] = (a * _sigmoid(a) * u).astype(BF16)
    for k in range(nf):
        @pl.when(f == k)
        def _(k=k):
            act_ref[:, k * tf:(k + 1) * tf] = chunk_ref[...]

    @pl.when(f == nf - 1)
    def _():
        y = jnp.dot(act_ref[...], wd_bf_ref[...], preferred_element_type=F32)
        y_ref[0] = (y * g_ref[0][:, :1]).astype(y_ref.dtype)


def _experts(xin, g, w_gate, w_up, w_down, layer, tf):
    E, M, D = xin.shape
    FF = w_gate.shape[3]
    return pl.pallas_call(
        functools.partial(_expert_kernel, nf=FF // tf, tf=tf),
        grid=(E, FF // tf),
        in_specs=[pl.BlockSpec((1, M, D), lambda e, f: (e, 0, 0)),
                  pl.BlockSpec((1, M, LANES), lambda e, f: (e, 0, 0)),
                  pl.BlockSpec((None, None, D, tf), lambda e, f: (layer, e, 0, f)),
                  pl.BlockSpec((None, None, D, tf), lambda e, f: (layer, e, 0, f)),
                  pl.BlockSpec((None, None, tf, D), lambda e, f: (layer, e, f, 0))],
        out_specs=pl.BlockSpec((1, M, D), lambda e, f: (e, 0, 0)),
        out_shape=jax.ShapeDtypeStruct((E, M, D), BF16),
        scratch_shapes=[pltpu.VMEM((M, FF), BF16), pltpu.VMEM((FF, D), BF16), pltpu.VMEM((M, tf), BF16)],
        compiler_params=_params("parallel", "arbitrary"),
        name="expert_swiglu",
    )(xin, g, w_gate, w_up, w_down)


def _scatter_kernel(x_ref, pos_t_ref, y_ref, gain_ref, o_ref, *, n_experts, cap, normalize):
    pos = pos_t_ref[0].astype(jnp.int32)
    slot = lax.broadcasted_iota(jnp.int32, (pos.shape[0], cap), 1)
    onehot = jnp.concatenate([jnp.where(pos[:, e:e + 1] == slot, 1.0, 0.0).astype(BF16) for e in range(n_experts)],
                             axis=1)
    y = y_ref[:, 0].reshape(n_experts * cap, y_ref.shape[-1])
    out = x_ref[0] + jnp.dot(onehot, y, preferred_element_type=F32)
    o_ref[0] = _rms(out, gain_ref[...]) if normalize else out


def _scatter(x, pos_t, y, cap, ts, gain, normalize):
    B, S, D = x.shape
    E = y.shape[0]
    return pl.pallas_call(
        functools.partial(_scatter_kernel, n_experts=E, cap=cap, normalize=normalize),
        grid=(B, S // ts),
        in_specs=[pl.BlockSpec((1, ts, D), lambda b, t: (b, t, 0)),
                  pl.BlockSpec((1, ts, LANES), lambda b, t: (b, t, 0)),
                  pl.BlockSpec((E, 1, cap, D), lambda b, t: (0, b, 0, 0)),
                  pl.BlockSpec((1, D), lambda b, t: (0, 0))],
        out_specs=pl.BlockSpec((1, ts, D), lambda b, t: (b, t, 0)),
        out_shape=jax.ShapeDtypeStruct((B, S, D), F32),
        compiler_params=_params("parallel", "arbitrary"),
        name="expert_scatter",
    )(x, pos_t, y, gain.reshape(1, D))


def kernel(x, w_in, w_attn_out, w_ret_out, w_out, ret_decay_logit, norm_mix, norm_ffn, w_router, w_gate, w_up,
           w_down, norm_final):
    B, S, D = x.shape
    T = B * S
    depth = w_in.shape[0]
    assert depth >= 1
    W = ATTN_WIDTH
    attn_in = len(ATTN_PATTERNS) * 3 * W
    n_in = w_in.shape[2]
    ret0, swish0, gates0 = attn_in, attn_in + 3 * D, attn_in + 4 * D
    cap = CAPACITY_FACTOR * S // N_EXPERTS
    x2d = x.reshape(T, D)
    w_in_b = w_in.astype(BF16)
    for layer in range(depth):
        proj = _norm_proj(x2d, norm_mix[layer], w_in_b, layer, 0, n_in, BF16, tm=2048, tn=1536)
        proj3 = proj.reshape(B, S, n_in)

        attn = _dilated_attention(proj3, 0)
        ret = _retention(proj3, ret_decay_logit[layer], ret0, swish0, D)

        x2d, h2, logits = _merge(attn.reshape(T, W), ret.reshape(T, D), proj, gates0, x2d,
                                 w_attn_out[layer].astype(BF16),
                                 w_ret_out[layer].astype(BF16), w_out[layer].astype(BF16), norm_ffn[layer],
                                 w_router[layer], tm=1024)

        pos, pos_t, aff = _route(logits.reshape(B, S, LANES), cap)
        xin, g = _gather(h2.reshape(B, S, D), pos, aff, cap)
        y = _experts(xin.reshape(N_EXPERTS, B * cap, D), g.reshape(N_EXPERTS, B * cap, LANES),
                     w_gate, w_up, w_down, layer, tf=256)
        x2d = _scatter(x2d.reshape(B, S, D), pos_t, y.reshape(N_EXPERTS, B, cap, D), cap, 1024, norm_final,
                       normalize=layer == depth - 1).reshape(T, D)
    return x2d.reshape(B, S, D)
```

```python
import functools

import jax
import jax.numpy as jnp
from jax import lax
from jax.experimental import pallas as pl
from jax.experimental.pallas import tpu as pltpu

EPS = 1e-6
ATTN_PATTERNS = ((128, 1), (512, 4), (2048, 16))
ATTN_HEADS = 8
ATTN_HEAD_DIM = 64
ATTN_WIDTH = ATTN_HEADS * ATTN_HEAD_DIM
RET_HEADS = 4
RET_CHUNK = 256
N_EXPERTS = 16
CAPACITY_FACTOR = 2

LANES = 128
MASKED = -1e30
VMEM_LIMIT_BYTES = 56 * 1024 * 1024

F32 = jnp.float32
BF16 = jnp.bfloat16


def _params(*semantics):
    return pltpu.CompilerParams(dimension_semantics=semantics, vmem_limit_bytes=VMEM_LIMIT_BYTES)


def _sigmoid(x):
    return 0.5 * (1.0 + jnp.tanh(0.5 * x))


def _rms(x, gain):
    return x * lax.rsqrt(jnp.mean(x * x, axis=-1, keepdims=True) + EPS) * gain


def _norm_proj_kernel(x_ref, gain_ref, w_ref, o_ref, h_ref):
    @pl.when(pl.program_id(1) == 0)
    def _():
        h_ref[...] = _rms(x_ref[...], gain_ref[...]).astype(BF16)

    o_ref[...] = jnp.dot(h_ref[...], w_ref[...], preferred_element_type=F32).astype(o_ref.dtype)


def _norm_proj(x2d, gain, w, layer, col0, ncols, out_dtype, tm, tn):
    T, D = x2d.shape
    assert T % tm == 0 and ncols % tn == 0 and col0 % tn == 0
    return pl.pallas_call(
        _norm_proj_kernel,
        grid=(T // tm, ncols // tn),
        in_specs=[pl.BlockSpec((tm, D), lambda i, j: (i, 0)),
                  pl.BlockSpec((1, D), lambda i, j: (0, 0)),
                  pl.BlockSpec((None, D, tn), lambda i, j: (layer, 0, j + col0 // tn))],
        out_specs=pl.BlockSpec((tm, tn), lambda i, j: (i, j)),
        out_shape=jax.ShapeDtypeStruct((T, ncols), out_dtype),
        scratch_shapes=[pltpu.VMEM((tm, D), BF16)],
        compiler_params=_params("parallel", "arbitrary"),
        name="norm_proj",
    )(x2d, gain.reshape(1, D), w)


ATTN_TQ = 128
ATTN_BLOCK = 1
ATTN_RING = 3
REGROUP_STRIDE = 4


def _attn_kernel(*refs, S, patterns):
    n_groups = len(patterns)
    in_refs, o_ref = refs[:3 * n_groups], refs[3 * n_groups]
    (stage_ref, stage2_ref, qs_ref, ks_ref, vs_ref, acc_ref, den_ref, max_ref, bias_ref, s_ref,
     m_ref) = refs[3 * n_groups + 1:]
    tq = ATTN_TQ
    n_tiles = S // tq
    hp = pl.program_id(0)
    lane = lax.broadcasted_iota(jnp.int32, (1, LANES), 1)
    first_head = lane < ATTN_HEAD_DIM
    geometry = []

    for gi, (window, d) in enumerate(patterns):
        half = window // (2 * d)
        tk = tq + 2 * half
        L = S // d
        seg = L + 2 * half
        geometry.append((d, half, tk, L // tq))
        q_in, k_in, v_in = in_refs[3 * gi:3 * gi + 3]

        @pl.when(pl.program_id(1) == 0)
        def _(gi=gi, d=d, half=half, tk=tk):
            r = lax.broadcasted_iota(jnp.int32, (tq, tk), 0)
            c = lax.broadcasted_iota(jnp.int32, (tq, tk), 1)
            rel = jnp.abs(c - half - r)
            dist = (d * rel).astype(F32)
            for hh in range(2):
                pow2 = jnp.full((tq, tk), jnp.left_shift(jnp.int32(2), 2 * hp + hh), jnp.int32).astype(F32)
                band = jnp.where(rel <= half, -(dist / pow2), MASKED)
                for kind in range(4):
                    tile_bias = band
                    if kind & 1:
                        tile_bias = jnp.where(c < half, MASKED, tile_bias)
                    if kind & 2:
                        tile_bias = jnp.where(c >= tq + half, MASKED, tile_bias)
                    bias_ref[gi, kind, hh * tq:(hh + 1) * tq, :] = tile_bias
            vs_ref[gi, :, LANES:] = jnp.ones((vs_ref.shape[1], LANES), BF16)

        zeros = jnp.zeros((half, LANES), BF16)
        if d == 1:
            ks_ref[gi, pl.ds(0, half), :] = zeros
            ks_ref[gi, pl.ds(half, S), :] = k_in[0]
            ks_ref[gi, pl.ds(half + S, half), :] = zeros
            vs_ref[gi, pl.ds(0, half), :LANES] = zeros
            vs_ref[gi, pl.ds(half, S), :LANES] = v_in[0]
            vs_ref[gi, pl.ds(half + S, half), :LANES] = zeros
        else:
            stage_ref[0] = q_in[0].astype(F32)
            stage_ref[1] = k_in[0].astype(F32)
            stage_ref[2] = v_in[0].astype(F32)
            d1 = d // REGROUP_STRIDE if d > REGROUP_STRIDE else 1
            src_ref = stage_ref
            if d1 > 1:
                assert d1 <= REGROUP_STRIDE
                part = S // REGROUP_STRIDE
                for t in range(3):
                    for r0 in range(REGROUP_STRIDE):
                        stage2_ref[t, r0 * part:(r0 + 1) * part, :] = stage_ref[t, pl.ds(r0, part, stride=REGROUP_STRIDE), :]
                src_ref = stage2_ref

            def regroup(r, carry, gi=gi, d=d, d1=d1, L=L, seg=seg, half=half, src_ref=src_ref):
                if d1 > 1:
                    rows = pl.ds((r % REGROUP_STRIDE) * (S // REGROUP_STRIDE) + r // REGROUP_STRIDE, L, stride=d1)
                else:
                    rows = pl.ds(r, L, stride=d)
                k0 = pl.multiple_of(r * seg, half)
                qs_ref[gi, pl.ds(pl.multiple_of(r * L, tq), L), :] = src_ref[0, rows, :].astype(BF16)
                ks_ref[gi, pl.ds(k0, half), :] = zeros
                ks_ref[gi, pl.ds(k0 + half, L), :] = src_ref[1, rows, :].astype(BF16)
                ks_ref[gi, pl.ds(k0 + half + L, half), :] = zeros
                vs_ref[gi, pl.ds(k0, half), :LANES] = zeros
                vs_ref[gi, pl.ds(k0 + half, L), :LANES] = src_ref[2, rows, :].astype(BF16)
                vs_ref[gi, pl.ds(k0 + half + L, half), :LANES] = zeros
                return carry

            lax.fori_loop(0, d, regroup, 0)

    def key_row(gi, u):
        d, half, tk, nblk = geometry[gi]
        return (u + u // nblk) * tq

    def scores(gi, u, slot):
        d, half, tk, nblk = geometry[gi]
        assert 2 * half == tq
        i = u % nblk
        kind = (1 if i == 0 else 0) + (2 if i == nblk - 1 else 0)
        qp = in_refs[3 * gi][0, u * tq:(u + 1) * tq, :] if d == 1 else qs_ref[gi, u * tq:(u + 1) * tq, :]
        qp = qp * (ATTN_HEAD_DIM ** -0.5)
        none = jnp.zeros_like(qp)
        q2 = jnp.concatenate([jnp.where(first_head, qp, none), jnp.where(first_head, none, qp)], axis=0)
        k0 = key_row(gi, u)
        s = lax.dot_general(q2, ks_ref[gi, k0:k0 + tk, :], (((1,), (1,)), ((), ())), preferred_element_type=F32)
        s_ref[slot] = s + bias_ref[gi, kind]

    def maxima(slot):
        m_ref[slot] = jnp.broadcast_to(jnp.max(s_ref[slot], axis=-1, keepdims=True), (2 * tq, LANES))

    def values(gi, u, slot):
        d, half, tk, nblk = geometry[gi]
        r, i = u // nblk, u % nblk
        m = m_ref[slot]
        p = jnp.concatenate([jnp.exp(s_ref[slot, :, c0:c0 + LANES] - m) for c0 in range(0, tk, LANES)], axis=1)
        k0 = key_row(gi, u)
        pv = jnp.dot(p.astype(BF16), vs_ref[gi, k0:k0 + tk, :], preferred_element_type=F32)
        rows = pl.ds(u * tq, tq) if d == 1 else pl.ds(i * (tq * d) + r, tq, stride=d)
        acc_ref[gi, rows, :] = jnp.where(first_head, pv[:tq, :LANES], pv[tq:, :LANES])
        den_ref[gi, rows, :] = jnp.where(first_head, pv[:tq, LANES:], pv[tq:, LANES:])
        max_ref[gi, rows, :] = jnp.where(first_head, m[:tq], m[tq:])

    blocks = [(gi, j) for gi in range(n_groups) for j in range(n_tiles // ATTN_BLOCK)]
    for step in range(len(blocks) + 2):
        for stage, fn in enumerate((scores, maxima, values)):
            t = step - stage
            if 0 <= t < len(blocks):
                gi, j = blocks[t]
                for k in range(ATTN_BLOCK):
                    slot = (t % ATTN_RING) * ATTN_BLOCK + k
                    if fn is maxima:
                        fn(slot)
                    else:
                        fn(gi, j * ATTN_BLOCK + k, slot)

    def merge(j, carry):
        rows = pl.ds(pl.multiple_of(j * tq, tq), tq)
        top = functools.reduce(jnp.maximum, [max_ref[g, rows, :] for g in range(n_groups)])
        w = [jnp.exp(max_ref[g, rows, :] - top) for g in range(n_groups)]
        num = sum(w[g] * acc_ref[g, rows, :] for g in range(n_groups))
        den = sum(w[g] * den_ref[g, rows, :] for g in range(n_groups))
        o_ref[0, rows, :] = (num / den).astype(o_ref.dtype)
        return carry

    lax.fori_loop(0, n_tiles, merge, 0)


def _dilated_attention(proj, col0):
    B, S, _ = proj.shape
    W = ATTN_WIDTH
    n_groups = len(ATTN_PATTERNS)
    halves = {w // (2 * d) for w, d in ATTN_PATTERNS}
    assert len(halves) == 1 and all(S % (d * ATTN_TQ) == 0 for _, d in ATTN_PATTERNS)
    half = halves.pop()
    tk = ATTN_TQ + 2 * half
    assert tk % LANES == 0 and (S // ATTN_TQ) % ATTN_BLOCK == 0
    pairs = W // LANES
    col = lambda c0: pl.BlockSpec((1, S, LANES), lambda hp, b: (b, 0, c0 // LANES + hp))
    return pl.pallas_call(
        functools.partial(_attn_kernel, S=S, patterns=ATTN_PATTERNS),
        grid=(pairs, B),
        in_specs=[col(col0 + (3 * g + t) * W) for g in range(n_groups) for t in range(3)],
        out_specs=pl.BlockSpec((1, S, LANES), lambda hp, b: (b, 0, hp)),
        out_shape=jax.ShapeDtypeStruct((B, S, W), BF16),
        scratch_shapes=[pltpu.VMEM((3, S, LANES), F32),
                        pltpu.VMEM((3, S, LANES), F32),
                        pltpu.VMEM((n_groups, S, LANES), BF16),
                        pltpu.VMEM((n_groups, 2 * S, LANES), BF16),
                        pltpu.VMEM((n_groups, 2 * S, 2 * LANES), BF16),
                        pltpu.VMEM((n_groups, S, LANES), F32),
                        pltpu.VMEM((n_groups, S, LANES), F32),
                        pltpu.VMEM((n_groups, S, LANES), F32),
                        pltpu.VMEM((n_groups, 4, 2 * ATTN_TQ, tk), F32),
                        pltpu.VMEM((ATTN_RING * ATTN_BLOCK, 2 * ATTN_TQ, tk), F32),
                        pltpu.VMEM((ATTN_RING * ATTN_BLOCK, 2 * ATTN_TQ, LANES), F32)],
        compiler_params=_params("arbitrary", "arbitrary"),
        name="dilated_attention",
    )(*([proj] * (3 * n_groups)))


def _log_sigmoid(x):
    return jnp.minimum(x, 0.0) - jnp.log1p(jnp.exp(-jnp.abs(x)))


def _retention_kernel(logit_ref, q_ref, k_ref, v_ref, gate_ref, o_ref, yf_ref, yb_ref, sf_ref, sb_ref):
    C = RET_CHUNK
    S, Dk = q_ref.shape[1], q_ref.shape[2]
    N = S // C
    h = pl.program_id(1)
    lf = _log_sigmoid(jnp.full((C, 1), logit_ref[0, h], F32))
    lb = _log_sigmoid(jnp.full((C, 1), logit_ref[1, h], F32))
    rel = (lax.broadcasted_iota(jnp.int32, (C, C), 0) - lax.broadcasted_iota(jnp.int32, (C, C), 1)).astype(F32)
    decay = jnp.exp(jnp.where(rel >= 0, lf * rel, -lb * rel))
    pos = lax.broadcasted_iota(jnp.int32, (C, 1), 0).astype(F32)
    xi_f = jnp.exp(lf * (pos + 1.0))
    zeta_f = jnp.exp(lf * (C - 1.0 - pos))
    xi_b = jnp.exp(lb * (C - pos))
    zeta_b = jnp.exp(lb * pos)
    chunk_f = jnp.exp(_log_sigmoid(jnp.full((Dk, 1), logit_ref[0, h], F32)) * C)
    chunk_b = jnp.exp(_log_sigmoid(jnp.full((Dk, 1), logit_ref[1, h], F32)) * C)
    scale = Dk ** -0.5

    def load(i):
        r0 = pl.multiple_of(i * C, C)
        return r0, q_ref[0, pl.ds(r0, C), :] * scale, k_ref[0, pl.ds(r0, C), :], v_ref[0, pl.ds(r0, C), :]

    def cross(state_ref, qc, kc, vc, xi, zeta, chunk_decay):
        st = state_ref[...]
        y = jnp.dot(qc, st.astype(BF16), preferred_element_type=F32) * xi
        kz = (kc.astype(F32) * zeta).astype(BF16)
        u = lax.dot_general(kz, vc, (((0,), (0,)), ((), ())), preferred_element_type=F32)
        state_ref[...] = st * chunk_decay + u
        return y

    sf_ref[...] = jnp.zeros_like(sf_ref)
    sb_ref[...] = jnp.zeros_like(sb_ref)

    def finish(rows, r):
        r = r * lax.rsqrt(jnp.mean(r * r, axis=-1, keepdims=True) + EPS)
        g = gate_ref[0, rows, :].astype(F32)
        o_ref[0, rows, :] = (g * _sigmoid(g) * r).astype(o_ref.dtype)

    def both(i):
        r0, qc, kc, vc = load(i)
        s = lax.dot_general(qc, kc, (((1,), (1,)), ((), ())), preferred_element_type=F32) * decay
        yf = jnp.dot(s.astype(BF16), vc, preferred_element_type=F32) + cross(sf_ref, qc, kc, vc, xi_f, zeta_f, chunk_f)
        r1, qd, kd, vd = load(N - 1 - i)
        yb = cross(sb_ref, qd, kd, vd, xi_b, zeta_b, chunk_b)
        return pl.ds(r0, C), yf, pl.ds(r1, C), yb

    def first_half(i, carry):
        rows_f, yf, rows_b, yb = both(i)
        yf_ref[rows_f, :] = yf
        yb_ref[rows_b, :] = yb
        return carry

    def second_half(i, carry):
        rows_f, yf, rows_b, yb = both(i)
        finish(rows_f, yf + yb_ref[rows_f, :])
        finish(rows_b, yf_ref[rows_b, :] + yb)
        return carry

    assert N % 2 == 0
    lax.fori_loop(0, N // 2, first_half, 0, unroll=2)
    lax.fori_loop(N // 2, N, second_half, 0, unroll=2)


def _retention(proj, decay_logit, q_col0, gate_col0, width):
    B, S, _ = proj.shape
    Dk = width // RET_HEADS
    qb = q_col0 // Dk
    gb = gate_col0 // Dk
    nb = RET_HEADS
    return pl.pallas_call(
        _retention_kernel,
        grid=(B, RET_HEADS),
        in_specs=[pl.BlockSpec(memory_space=pltpu.SMEM),
                  pl.BlockSpec((1, S, Dk), lambda b, h: (b, 0, qb + h)),
                  pl.BlockSpec((1, S, Dk), lambda b, h: (b, 0, qb + nb + h)),
                  pl.BlockSpec((1, S, Dk), lambda b, h: (b, 0, qb + 2 * nb + h)),
                  pl.BlockSpec((1, S, Dk), lambda b, h: (b, 0, gb + h))],
        out_specs=pl.BlockSpec((1, S, Dk), lambda b, h: (b, 0, h)),
        out_shape=jax.ShapeDtypeStruct((B, S, RET_HEADS * Dk), BF16),
        scratch_shapes=[pltpu.VMEM((S, Dk), F32), pltpu.VMEM((S, Dk), F32),
                        pltpu.VMEM((Dk, Dk), F32), pltpu.VMEM((Dk, Dk), F32)],
        compiler_params=_params("parallel", "parallel"),
        name="retention",
    )(decay_logit, proj, proj, proj, proj)


MERGE_ROW_BLOCKS = 4


def _merge_kernel(attn_ref, ret_ref, ga0_ref, ga1_ref, gr0_ref, gr1_ref, x_ref, wa_ref, wr_ref, wo_ref, gain_ref,
                  wrt_ref, wrt_lo_ref, xo_ref, h_ref, logit_ref):
    tm = x_ref.shape[0]
    rows = tm // MERGE_ROW_BLOCKS
    for r in range(MERGE_ROW_BLOCKS):
        rs = slice(r * rows, (r + 1) * rows)
        pa = jnp.dot(attn_ref[rs, :], wa_ref[...], preferred_element_type=F32)
        pr = jnp.dot(ret_ref[rs, :], wr_ref[...], preferred_element_type=F32)
        ga = jnp.concatenate([ga0_ref[rs, :], ga1_ref[rs, :]], axis=1).astype(F32)
        gr = jnp.concatenate([gr0_ref[rs, :], gr1_ref[rs, :]], axis=1).astype(F32)
        merged = pa * _sigmoid(ga) + pr * _sigmoid(gr)
        x = x_ref[rs, :] + jnp.dot(merged.astype(BF16), wo_ref[...], preferred_element_type=F32)
        xo_ref[rs, :] = x
        h = _rms(x, gain_ref[...])
        h_hi = h.astype(BF16)
        h_ref[rs, :] = h_hi
        h_lo = (h - h_hi.astype(F32)).astype(BF16)
        logit_ref[rs, :] = (jnp.dot(h_hi, wrt_ref[...], preferred_element_type=F32)
                            + jnp.dot(h_lo, wrt_ref[...], preferred_element_type=F32)
                            + jnp.dot(h_hi, wrt_lo_ref[...], preferred_element_type=F32))


def _merge(attn, ret, gates, gate_col0, x2d, wa, wr, wo, gain, w_router, tm):
    T, D = x2d.shape
    gw = D // 2
    assert gate_col0 % gw == 0
    g0 = gate_col0 // gw
    w_router_pad = jnp.pad(w_router, ((0, 0), (0, LANES - w_router.shape[1])))
    wrt_hi = w_router_pad.astype(BF16)
    wrt_lo = (w_router_pad - wrt_hi.astype(F32)).astype(BF16)
    tok = lambda width, col=0: pl.BlockSpec((tm, width), lambda i: (i, col))
    full = lambda a: pl.BlockSpec(a.shape, lambda i: (0, 0))
    return pl.pallas_call(
        _merge_kernel,
        grid=(T // tm,),
        in_specs=[tok(ATTN_WIDTH), tok(D), tok(gw, g0), tok(gw, g0 + 1), tok(gw, g0 + 2), tok(gw, g0 + 3), tok(D),
                  full(wa), full(wr), full(wo), pl.BlockSpec((1, D), lambda i: (0, 0)), full(wrt_hi), full(wrt_lo)],
        out_specs=[tok(D), tok(D), tok(LANES)],
        out_shape=[jax.ShapeDtypeStruct((T, D), F32), jax.ShapeDtypeStruct((T, D), BF16),
                   jax.ShapeDtypeStruct((T, LANES), F32)],
        compiler_params=_params("parallel"),
        name="merge_out_proj",
    )(attn, ret, gates, gates, gates, gates, x2d, wa, wr, wo, gain.reshape(1, D), wrt_hi, wrt_lo)


def _route_kernel(logit_ref, pos_ref, pos_t_ref, aff_ref, before_ref, *, n_experts, cap):
    S = logit_ref.shape[1]
    rows = 256

    @pl.when(pl.program_id(0) == 0)
    def _():
        def fill(i, carry):
            r0 = pl.multiple_of(i * rows, rows)
            r = lax.broadcasted_iota(jnp.int32, (rows, S), 0) + r0
            c = lax.broadcasted_iota(jnp.int32, (rows, S), 1)
            before_ref[pl.ds(r0, rows), :] = jnp.where(r < c, 1.0, 0.0).astype(BF16)
            return carry
        lax.fori_loop(0, S // rows, fill, 0)

    lane = lax.broadcasted_iota(jnp.int32, (1, LANES), 1)
    lg = jnp.where(lane < n_experts, logit_ref[0], MASKED)
    ex = jnp.exp(lg - jnp.max(lg, axis=-1, keepdims=True))
    aff = (ex / jnp.sum(ex, axis=-1, keepdims=True)).T[:n_experts]

    def count_at_least(bits):
        return jnp.sum((aff >= lax.bitcast_convert_type(bits, F32)).astype(jnp.int32), axis=-1, keepdims=True)

    thr_bits = jnp.zeros((n_experts, 1), jnp.int32)
    for lo in range(29, 0, -2):
        for digit in (1, 2, 3):
            cand = thr_bits | (digit << lo)
            best = cand if digit == 1 else jnp.where(count_at_least(cand) >= cap, cand, best)
            if digit == 1:
                first_ok = count_at_least(cand) >= cap
        thr_bits = jnp.where(first_ok, best, thr_bits)
    thr_bits = jnp.where(count_at_least(thr_bits | 1) >= cap, thr_bits | 1, thr_bits)
    floor = lax.bitcast_convert_type(thr_bits, F32)
    thr = jnp.min(jnp.where(aff >= floor, aff, jnp.inf), axis=-1, keepdims=True)
    above = aff > thr
    tied = aff == thr
    need = cap - jnp.sum(above.astype(jnp.int32), axis=-1, keepdims=True)
    count_before = lambda picked: jnp.dot(picked.astype(BF16), before_ref[...], preferred_element_type=F32)
    tied_f = jnp.where(tied, 1.0, 0.0)
    all_tied_fit = jnp.all(jnp.sum(tied.astype(jnp.int32), axis=-1, keepdims=True) == need)
    tied_taken = lax.cond(all_tied_fit, lambda: tied_f,
                          lambda: jnp.where(count_before(tied_f) < need.astype(F32), tied_f, 0.0))
    chosen_f = jnp.where(above, 1.0, tied_taken)
    pos = jnp.where(chosen_f > 0.0, count_before(chosen_f), -1.0)
    pos_ref[0] = pos
    pos_t_ref[0] = jnp.concatenate([pos, jnp.full((LANES - n_experts, S), -1.0, F32)], axis=0).T
    aff_ref[0] = aff


def _route(logits, cap):
    B, S, _ = logits.shape
    E = N_EXPERTS
    return pl.pallas_call(
        functools.partial(_route_kernel, n_experts=E, cap=cap),
        grid=(B,),
        in_specs=[pl.BlockSpec((1, S, LANES), lambda b: (b, 0, 0))],
        out_specs=[pl.BlockSpec((1, E, S), lambda b: (b, 0, 0)), pl.BlockSpec((1, S, LANES), lambda b: (b, 0, 0)),
                   pl.BlockSpec((1, E, S), lambda b: (b, 0, 0))],
        out_shape=[jax.ShapeDtypeStruct((B, E, S), F32), jax.ShapeDtypeStruct((B, S, LANES), F32),
                   jax.ShapeDtypeStruct((B, E, S), F32)],
        scratch_shapes=[pltpu.VMEM((S, S), BF16)],
        compiler_params=_params("arbitrary"),
        name="route",
    )(logits)


def _gather_kernel(h_ref, pos_ref, aff_ref, x_ref, g_ref, *, n_experts, cap):
    S = h_ref.shape[1]
    slot = lax.broadcasted_iota(jnp.int32, (cap, S), 0)
    for e in range(n_experts):
        mask = slot == pos_ref[0, e:e + 1, :].astype(jnp.int32)
        onehot = jnp.where(mask, 1.0, 0.0).astype(BF16)
        x_ref[e, 0] = jnp.dot(onehot, h_ref[0], preferred_element_type=F32).astype(x_ref.dtype)
        g = jnp.sum(jnp.where(mask, aff_ref[0, e:e + 1, :], 0.0), axis=-1, keepdims=True)
        g_ref[e, 0] = jnp.broadcast_to(g, (cap, LANES))


def _gather(h, pos, aff, cap):
    B, S, D = h.shape
    E = pos.shape[1]
    rows = pl.BlockSpec((1, E, S), lambda b: (b, 0, 0))
    return pl.pallas_call(
        functools.partial(_gather_kernel, n_experts=E, cap=cap),
        grid=(B,),
        in_specs=[pl.BlockSpec((1, S, D), lambda b: (b, 0, 0)), rows, rows],
        out_specs=[pl.BlockSpec((E, 1, cap, D), lambda b: (0, b, 0, 0)),
                   pl.BlockSpec((E, 1, cap, LANES), lambda b: (0, b, 0, 0))],
        out_shape=[jax.ShapeDtypeStruct((E, B, cap, D), BF16), jax.ShapeDtypeStruct((E, B, cap, LANES), F32)],
        compiler_params=_params("parallel"),
        name="expert_gather",
    )(h, pos, aff)


EXPERT_ROW_BLOCKS = 4


def _expert_kernel(x_ref, g_ref, wg_ref, wu_ref, wd_ref, y_ref, act_ref, wd_bf_ref, chunk_ref, *, nf, tf):
    f = pl.program_id(1)
    wg = wg_ref[...].astype(BF16)
    wu = wu_ref[...].astype(BF16)
    wd_bf_ref[pl.ds(pl.multiple_of(f * tf, tf), tf), :] = wd_ref[...].astype(BF16)
    rows = x_ref.shape[1] // EXPERT_ROW_BLOCKS
    for r in range(EXPERT_ROW_BLOCKS):
        x = x_ref[0, r * rows:(r + 1) * rows, :]
        a = jnp.dot(x, wg, preferred_element_type=F32)
        u = jnp.dot(x, wu, preferred_element_type=F32)
        chunk_ref[r * rows:(r + 1) * rows, :] = (a * _sigmoid(a) * u).astype(BF16)
    for k in range(nf):
        @pl.when(f == k)
        def _(k=k):
            act_ref[:, k * tf:(k + 1) * tf] = chunk_ref[...]

    @pl.when(f == nf - 1)
    def _():
        y = jnp.dot(act_ref[...], wd_bf_ref[...], preferred_element_type=F32)
        y_ref[0] = (y * g_ref[0][:, :1]).astype(y_ref.dtype)


def _experts(xin, g, w_gate, w_up, w_down, layer, tf):
    E, M, D = xin.shape
    FF = w_gate.shape[3]
    return pl.pallas_call(
        functools.partial(_expert_kernel, nf=FF // tf, tf=tf),
        grid=(E, FF // tf),
        in_specs=[pl.BlockSpec((1, M, D), lambda e, f: (e, 0, 0)),
                  pl.BlockSpec((1, M, LANES), lambda e, f: (e, 0, 0)),
                  pl.BlockSpec((None, None, D, tf), lambda e, f: (layer, e, 0, f)),
                  pl.BlockSpec((None, None, D, tf), lambda e, f: (layer, e, 0, f)),
                  pl.BlockSpec((None, None, tf, D), lambda e, f: (layer, e, f, 0))],
        out_specs=pl.BlockSpec((1, M, D), lambda e, f: (e, 0, 0)),
        out_shape=jax.ShapeDtypeStruct((E, M, D), BF16),
        scratch_shapes=[pltpu.VMEM((M, FF), BF16), pltpu.VMEM((FF, D), BF16), pltpu.VMEM((M, tf), BF16)],
        compiler_params=_params("parallel", "arbitrary"),
        name="expert_swiglu",
    )(xin, g, w_gate, w_up, w_down)


def _scatter_kernel(x_ref, pos_t_ref, y_ref, gain_ref, o_ref, *, n_experts, cap, normalize):
    pos = pos_t_ref[0].astype(jnp.int32)
    slot = lax.broadcasted_iota(jnp.int32, (pos.shape[0], cap), 1)
    onehot = jnp.concatenate([jnp.where(pos[:, e:e + 1] == slot, 1.0, 0.0).astype(BF16) for e in range(n_experts)],
                             axis=1)
    y = y_ref[:, 0].reshape(n_experts * cap, y_ref.shape[-1])
    out = x_ref[0] + jnp.dot(onehot, y, preferred_element_type=F32)
    o_ref[0] = _rms(out, gain_ref[...]) if normalize else out


def _scatter(x, pos_t, y, cap, ts, gain, normalize):
    B, S, D = x.shape
    E = y.shape[0]
    return pl.pallas_call(
        functools.partial(_scatter_kernel, n_experts=E, cap=cap, normalize=normalize),
        grid=(B, S // ts),
        in_specs=[pl.BlockSpec((1, ts, D), lambda b, t: (b, t, 0)),
                  pl.BlockSpec((1, ts, LANES), lambda b, t: (b, t, 0)),
                  pl.BlockSpec((E, 1, cap, D), lambda b, t: (0, b, 0, 0)),
                  pl.BlockSpec((1, D), lambda b, t: (0, 0))],
        out_specs=pl.BlockSpec((1, ts, D), lambda b, t: (b, t, 0)),
        out_shape=jax.ShapeDtypeStruct((B, S, D), F32),
        compiler_params=_params("parallel", "arbitrary"),
        name="expert_scatter",
    )(x, pos_t, y, gain.reshape(1, D))


def kernel(x, w_in, w_attn_out, w_ret_out, w_out, ret_decay_logit, norm_mix, norm_ffn, w_router, w_gate, w_up,
           w_down, norm_final):
    B, S, D = x.shape
    T = B * S
    depth = w_in.shape[0]
    assert depth >= 1
    W = ATTN_WIDTH
    attn_in = len(ATTN_PATTERNS) * 3 * W
    n_in = w_in.shape[2]
    ret0, swish0, gates0 = attn_in, attn_in + 3 * D, attn_in + 4 * D
    cap = CAPACITY_FACTOR * S // N_EXPERTS
    x2d = x.reshape(T, D)
    w_in_b = w_in.astype(BF16)
    for layer in range(depth):
        proj = _norm_proj(x2d, norm_mix[layer], w_in_b, layer, 0, n_in, BF16, tm=2048, tn=1536)
        proj3 = proj.reshape(B, S, n_in)

        attn = _dilated_attention(proj3, 0)
        ret = _retention(proj3, ret_decay_logit[layer], ret0, swish0, D)

        x2d, h2, logits = _merge(attn.reshape(T, W), ret.reshape(T, D), proj, gates0, x2d,
                                 w_attn_out[layer].astype(BF16),
                                 w_ret_out[layer].astype(BF16), w_out[layer].astype(BF16), norm_ffn[layer],
                                 w_router[layer], tm=1024)

        pos, pos_t, aff = _route(logits.reshape(B, S, LANES), cap)
        xin, g = _gather(h2.reshape(B, S, D), pos, aff, cap)
        y = _experts(xin.reshape(N_EXPERTS, B * cap, D), g.reshape(N_EXPERTS, B * cap, LANES),
                     w_gate, w_up, w_down, layer, tf=256)
        x2d = _scatter(x2d.reshape(B, S, D), pos_t, y.reshape(N_EXPERTS, B, cap, D), cap, 1024, norm_final,
                       normalize=layer == depth - 1).reshape(T, D)
    return x2d.reshape(B, S, D)
```

```python
import functools

import jax
import jax.numpy as jnp
from jax import lax
from jax.experimental import pallas as pl
from jax.experimental.pallas import tpu as pltpu

EPS = 1e-6
ATTN_PATTERNS = ((128, 1), (512, 4), (2048, 16))
ATTN_HEADS = 8
ATTN_HEAD_DIM = 64
ATTN_WIDTH = ATTN_HEADS * ATTN_HEAD_DIM
RET_HEADS = 4
RET_CHUNK = 256
N_EXPERTS = 16
CAPACITY_FACTOR = 2

LANES = 128
MASKED = -1e30
VMEM_LIMIT_BYTES = 56 * 1024 * 1024

F32 = jnp.float32
BF16 = jnp.bfloat16


def _params(*semantics):
    return pltpu.CompilerParams(dimension_semantics=semantics, vmem_limit_bytes=VMEM_LIMIT_BYTES)


def _sigmoid(x):
    return 0.5 * (1.0 + jnp.tanh(0.5 * x))


def _rms(x, gain):
    return x * lax.rsqrt(jnp.mean(x * x, axis=-1, keepdims=True) + EPS) * gain


NORM_ROW_BLOCKS = 4


def _norm_proj_kernel(x_ref, gain_ref, w_ref, o_ref, h_ref):
    @pl.when(pl.program_id(1) == 0)
    def _():
        rows = x_ref.shape[0] // NORM_ROW_BLOCKS
        for r in range(NORM_ROW_BLOCKS):
            rs = slice(r * rows, (r + 1) * rows)
            h = _rms(x_ref[rs, :], gain_ref[...]).astype(BF16)
            h_ref[rs, :] = h
            o_ref[rs, :] = jnp.dot(h, w_ref[...], preferred_element_type=F32).astype(o_ref.dtype)

    @pl.when(pl.program_id(1) > 0)
    def _():
        o_ref[...] = jnp.dot(h_ref[...], w_ref[...], preferred_element_type=F32).astype(o_ref.dtype)


def _norm_proj(x2d, gain, w, layer, col0, ncols, out_dtype, tm, tn):
    T, D = x2d.shape
    assert T % tm == 0 and ncols % tn == 0 and col0 % tn == 0
    return pl.pallas_call(
        _norm_proj_kernel,
        grid=(T // tm, ncols // tn),
        in_specs=[pl.BlockSpec((tm, D), lambda i, j: (i, 0)),
                  pl.BlockSpec((1, D), lambda i, j: (0, 0)),
                  pl.BlockSpec((None, D, tn), lambda i, j: (layer, 0, j + col0 // tn))],
        out_specs=pl.BlockSpec((tm, tn), lambda i, j: (i, j)),
        out_shape=jax.ShapeDtypeStruct((T, ncols), out_dtype),
        scratch_shapes=[pltpu.VMEM((tm, D), BF16)],
        compiler_params=_params("parallel", "arbitrary"),
        name="norm_proj",
    )(x2d, gain.reshape(1, D), w)


ATTN_TQ = 128
ATTN_BLOCK = 1
ATTN_RING = 3
REGROUP_STRIDE = 4


def _attn_kernel(*refs, S, patterns):
    n_groups = len(patterns)
    in_refs, o_ref = refs[:3 * n_groups], refs[3 * n_groups]
    (stage_ref, stage2_ref, qs_ref, ks_ref, vs_ref, acc_ref, den_ref, max_ref, bias_ref, s_ref,
     m_ref) = refs[3 * n_groups + 1:]
    tq = ATTN_TQ
    n_tiles = S // tq
    hp = pl.program_id(0)
    lane = lax.broadcasted_iota(jnp.int32, (1, LANES), 1)
    first_head = lane < ATTN_HEAD_DIM
    geometry = []

    for gi, (window, d) in enumerate(patterns):
        half = window // (2 * d)
        tk = tq + 2 * half
        L = S // d
        seg = L + 2 * half
        geometry.append((d, half, tk, L // tq))
        q_in, k_in, v_in = in_refs[3 * gi:3 * gi + 3]

        @pl.when(pl.program_id(1) == 0)
        def _(gi=gi, d=d, half=half, tk=tk):
            r = lax.broadcasted_iota(jnp.int32, (tq, tk), 0)
            c = lax.broadcasted_iota(jnp.int32, (tq, tk), 1)
            rel = jnp.abs(c - half - r)
            dist = (d * rel).astype(F32)
            for hh in range(2):
                pow2 = jnp.full((tq, tk), jnp.left_shift(jnp.int32(2), 2 * hp + hh), jnp.int32).astype(F32)
                band = jnp.where(rel <= half, -(dist / pow2), MASKED)
                for kind in range(4):
                    tile_bias = band
                    if kind & 1:
                        tile_bias = jnp.where(c < half, MASKED, tile_bias)
                    if kind & 2:
                        tile_bias = jnp.where(c >= tq + half, MASKED, tile_bias)
                    bias_ref[gi, kind, hh * tq:(hh + 1) * tq, :] = tile_bias
            vs_ref[gi, :, LANES:] = jnp.ones((vs_ref.shape[1], LANES), BF16)

        zeros = jnp.zeros((half, LANES), BF16)
        if d == 1:
            ks_ref[gi, pl.ds(0, half), :] = zeros
            ks_ref[gi, pl.ds(half, S), :] = k_in[0]
            ks_ref[gi, pl.ds(half + S, half), :] = zeros
            vs_ref[gi, pl.ds(0, half), :LANES] = zeros
            vs_ref[gi, pl.ds(half, S), :LANES] = v_in[0]
            vs_ref[gi, pl.ds(half + S, half), :LANES] = zeros
        else:
            stage_ref[0] = q_in[0].astype(F32)
            stage_ref[1] = k_in[0].astype(F32)
            stage_ref[2] = v_in[0].astype(F32)
            d1 = d // REGROUP_STRIDE if d > REGROUP_STRIDE else 1
            src_ref = stage_ref
            if d1 > 1:
                assert d1 <= REGROUP_STRIDE
                part = S // REGROUP_STRIDE
                for t in range(3):
                    for r0 in range(REGROUP_STRIDE):
                        stage2_ref[t, r0 * part:(r0 + 1) * part, :] = stage_ref[t, pl.ds(r0, part, stride=REGROUP_STRIDE), :]
                src_ref = stage2_ref

            def regroup(r, carry, gi=gi, d=d, d1=d1, L=L, seg=seg, half=half, src_ref=src_ref):
                if d1 > 1:
                    rows = pl.ds((r % REGROUP_STRIDE) * (S // REGROUP_STRIDE) + r // REGROUP_STRIDE, L, stride=d1)
                else:
                    rows = pl.ds(r, L, stride=d)
                k0 = r * seg
                qs_ref[gi, pl.ds(r * L, L), :] = src_ref[0, rows, :].astype(BF16)
                ks_ref[gi, pl.ds(k0, half), :] = zeros
                ks_ref[gi, pl.ds(k0 + half, L), :] = src_ref[1, rows, :].astype(BF16)
                ks_ref[gi, pl.ds(k0 + half + L, half), :] = zeros
                vs_ref[gi, pl.ds(k0, half), :LANES] = zeros
                vs_ref[gi, pl.ds(k0 + half, L), :LANES] = src_ref[2, rows, :].astype(BF16)
                vs_ref[gi, pl.ds(k0 + half + L, half), :LANES] = zeros
                return carry

            for r in range(d):
                regroup(r, 0)

    def key_row(gi, u):
        d, half, tk, nblk = geometry[gi]
        return (u + u // nblk) * tq

    def scores(gi, u, slot):
        d, half, tk, nblk = geometry[gi]
        assert 2 * half == tq
        i = u % nblk
        kind = (1 if i == 0 else 0) + (2 if i == nblk - 1 else 0)
        qp = in_refs[3 * gi][0, u * tq:(u + 1) * tq, :] if d == 1 else qs_ref[gi, u * tq:(u + 1) * tq, :]
        qp = qp * (ATTN_HEAD_DIM ** -0.5)
        none = jnp.zeros_like(qp)
        q2 = jnp.concatenate([jnp.where(first_head, qp, none), jnp.where(first_head, none, qp)], axis=0)
        k0 = key_row(gi, u)
        s = lax.dot_general(q2, ks_ref[gi, k0:k0 + tk, :], (((1,), (1,)), ((), ())), preferred_element_type=F32)
        s_ref[slot] = s + bias_ref[gi, kind]

    def maxima(slot):
        m_ref[slot] = jnp.broadcast_to(jnp.max(s_ref[slot], axis=-1, keepdims=True), (2 * tq, LANES))

    def values(gi, u, slot):
        d, half, tk, nblk = geometry[gi]
        r, i = u // nblk, u % nblk
        m = m_ref[slot]
        p = jnp.concatenate([jnp.exp(s_ref[slot, :, c0:c0 + LANES] - m) for c0 in range(0, tk, LANES)], axis=1)
        k0 = key_row(gi, u)
        pv = jnp.dot(p.astype(BF16), vs_ref[gi, k0:k0 + tk, :], preferred_element_type=F32)
        rows = pl.ds(u * tq, tq) if d == 1 else pl.ds(i * (tq * d) + r, tq, stride=d)
        acc_ref[gi, rows, :] = jnp.where(first_head, pv[:tq, :LANES], pv[tq:, :LANES])
        den_ref[gi, rows, :] = jnp.where(first_head, pv[:tq, LANES:], pv[tq:, LANES:])
        max_ref[gi, rows, :] = jnp.where(first_head, m[:tq], m[tq:])

    blocks = [(gi, j) for gi in range(n_groups) for j in range(n_tiles // ATTN_BLOCK)]
    for step in range(len(blocks) + 2):
        for stage, fn in enumerate((scores, maxima, values)):
            t = step - stage
            if 0 <= t < len(blocks):
                gi, j = blocks[t]
                for k in range(ATTN_BLOCK):
                    slot = (t % ATTN_RING) * ATTN_BLOCK + k
                    if fn is maxima:
                        fn(slot)
                    else:
                        fn(gi, j * ATTN_BLOCK + k, slot)

    def merge(j, carry):
        rows = pl.ds(pl.multiple_of(j * tq, tq), tq)
        top = functools.reduce(jnp.maximum, [max_ref[g, rows, :] for g in range(n_groups)])
        w = [jnp.exp(max_ref[g, rows, :] - top) for g in range(n_groups)]
        num = sum(w[g] * acc_ref[g, rows, :] for g in range(n_groups))
        den = sum(w[g] * den_ref[g, rows, :] for g in range(n_groups))
        o_ref[0, rows, :] = (num / den).astype(o_ref.dtype)
        return carry

    lax.fori_loop(0, n_tiles, merge, 0)


def _dilated_attention(proj, col0):
    B, S, _ = proj.shape
    W = ATTN_WIDTH
    n_groups = len(ATTN_PATTERNS)
    halves = {w // (2 * d) for w, d in ATTN_PATTERNS}
    assert len(halves) == 1 and all(S % (d * ATTN_TQ) == 0 for _, d in ATTN_PATTERNS)
    half = halves.pop()
    tk = ATTN_TQ + 2 * half
    assert tk % LANES == 0 and (S // ATTN_TQ) % ATTN_BLOCK == 0
    pairs = W // LANES
    col = lambda c0: pl.BlockSpec((1, S, LANES), lambda hp, b: (b, 0, c0 // LANES + hp))
    return pl.pallas_call(
        functools.partial(_attn_kernel, S=S, patterns=ATTN_PATTERNS),
        grid=(pairs, B),
        in_specs=[col(col0 + (3 * g + t) * W) for g in range(n_groups) for t in range(3)],
        out_specs=pl.BlockSpec((1, S, LANES), lambda hp, b: (b, 0, hp)),
        out_shape=jax.ShapeDtypeStruct((B, S, W), BF16),
        scratch_shapes=[pltpu.VMEM((3, S, LANES), F32),
                        pltpu.VMEM((3, S, LANES), F32),
                        pltpu.VMEM((n_groups, S, LANES), BF16),
                        pltpu.VMEM((n_groups, 2 * S, LANES), BF16),
                        pltpu.VMEM((n_groups, 2 * S, 2 * LANES), BF16),
                        pltpu.VMEM((n_groups, S, LANES), F32),
                        pltpu.VMEM((n_groups, S, LANES), F32),
                        pltpu.VMEM((n_groups, S, LANES), F32),
                        pltpu.VMEM((n_groups, 4, 2 * ATTN_TQ, tk), F32),
                        pltpu.VMEM((ATTN_RING * ATTN_BLOCK, 2 * ATTN_TQ, tk), F32),
                        pltpu.VMEM((ATTN_RING * ATTN_BLOCK, 2 * ATTN_TQ, LANES), F32)],
        compiler_params=_params("arbitrary", "arbitrary"),
        name="dilated_attention",
    )(*([proj] * (3 * n_groups)))


def _log_sigmoid(x):
    return jnp.minimum(x, 0.0) - jnp.log1p(jnp.exp(-jnp.abs(x)))


def _retention_kernel(logit_ref, q_ref, k_ref, v_ref, gate_ref, o_ref, yf_ref, yb_ref, sf_ref, sb_ref):
    C = RET_CHUNK
    S, Dk = q_ref.shape[1], q_ref.shape[2]
    N = S // C
    h = pl.program_id(1)
    lf = _log_sigmoid(jnp.full((C, 1), logit_ref[0, h], F32))
    lb = _log_sigmoid(jnp.full((C, 1), logit_ref[1, h], F32))
    rel = (lax.broadcasted_iota(jnp.int32, (C, C), 0) - lax.broadcasted_iota(jnp.int32, (C, C), 1)).astype(F32)
    decay = jnp.exp(jnp.where(rel >= 0, lf * rel, -lb * rel))
    pos = lax.broadcasted_iota(jnp.int32, (C, 1), 0).astype(F32)
    xi_f = jnp.exp(lf * (pos + 1.0))
    zeta_f = jnp.exp(lf * (C - 1.0 - pos))
    xi_b = jnp.exp(lb * (C - pos))
    zeta_b = jnp.exp(lb * pos)
    chunk_f = jnp.exp(_log_sigmoid(jnp.full((Dk, 1), logit_ref[0, h], F32)) * C)
    chunk_b = jnp.exp(_log_sigmoid(jnp.full((Dk, 1), logit_ref[1, h], F32)) * C)
    scale = Dk ** -0.5

    def load(i):
        r0 = pl.multiple_of(i * C, C)
        return r0, q_ref[0, pl.ds(r0, C), :] * scale, k_ref[0, pl.ds(r0, C), :], v_ref[0, pl.ds(r0, C), :]

    def cross(state_ref, qc, kc, vc, xi, zeta, chunk_decay):
        st = state_ref[...]
        y = jnp.dot(qc, st.astype(BF16), preferred_element_type=F32) * xi
        kz = (kc.astype(F32) * zeta).astype(BF16)
        u = lax.dot_general(kz, vc, (((0,), (0,)), ((), ())), preferred_element_type=F32)
        state_ref[...] = st * chunk_decay + u
        return y

    sf_ref[...] = jnp.zeros_like(sf_ref)
    sb_ref[...] = jnp.zeros_like(sb_ref)

    def finish(rows, r):
        r = r * lax.rsqrt(jnp.mean(r * r, axis=-1, keepdims=True) + EPS)
        g = gate_ref[0, rows, :].astype(F32)
        o_ref[0, rows, :] = (g * _sigmoid(g) * r).astype(o_ref.dtype)

    def both(i):
        r0, qc, kc, vc = load(i)
        s = lax.dot_general(qc, kc, (((1,), (1,)), ((), ())), preferred_element_type=F32) * decay
        yf = jnp.dot(s.astype(BF16), vc, preferred_element_type=F32) + cross(sf_ref, qc, kc, vc, xi_f, zeta_f, chunk_f)
        r1, qd, kd, vd = load(N - 1 - i)
        yb = cross(sb_ref, qd, kd, vd, xi_b, zeta_b, chunk_b)
        return pl.ds(r0, C), yf, pl.ds(r1, C), yb

    def first_half(i, carry):
        rows_f, yf, rows_b, yb = both(i)
        yf_ref[rows_f, :] = yf
        yb_ref[rows_b, :] = yb
        return carry

    def second_half(i, carry):
        rows_f, yf, rows_b, yb = both(i)
        finish(rows_f, yf + yb_ref[rows_f, :])
        finish(rows_b, yf_ref[rows_b, :] + yb)
        return carry

    assert N % 2 == 0
    lax.fori_loop(0, N // 2, first_half, 0, unroll=2)
    lax.fori_loop(N // 2, N, second_half, 0, unroll=2)


def _retention(proj, decay_logit, q_col0, gate_col0, width):
    B, S, _ = proj.shape
    Dk = width // RET_HEADS
    qb = q_col0 // Dk
    gb = gate_col0 // Dk
    nb = RET_HEADS
    return pl.pallas_call(
        _retention_kernel,
        grid=(B, RET_HEADS),
        in_specs=[pl.BlockSpec(memory_space=pltpu.SMEM),
                  pl.BlockSpec((1, S, Dk), lambda b, h: (b, 0, qb + h)),
                  pl.BlockSpec((1, S, Dk), lambda b, h: (b, 0, qb + nb + h)),
                  pl.BlockSpec((1, S, Dk), lambda b, h: (b, 0, qb + 2 * nb + h)),
                  pl.BlockSpec((1, S, Dk), lambda b, h: (b, 0, gb + h))],
        out_specs=pl.BlockSpec((1, S, Dk), lambda b, h: (b, 0, h)),
        out_shape=jax.ShapeDtypeStruct((B, S, RET_HEADS * Dk), BF16),
        scratch_shapes=[pltpu.VMEM((S, Dk), F32), pltpu.VMEM((S, Dk), F32),
                        pltpu.VMEM((Dk, Dk), F32), pltpu.VMEM((Dk, Dk), F32)],
        compiler_params=_params("parallel", "parallel"),
        name="retention",
    )(decay_logit, proj, proj, proj, proj)


MERGE_ROW_BLOCKS = 4


def _merge_kernel(attn_ref, ret_ref, ga0_ref, ga1_ref, gr0_ref, gr1_ref, x_ref, wa_ref, wr_ref, wo_ref, gain_ref,
                  wrt_ref, wrt_lo_ref, xo_ref, h_ref, logit_ref):
    tm = x_ref.shape[0]
    rows = tm // MERGE_ROW_BLOCKS
    for r in range(MERGE_ROW_BLOCKS):
        rs = slice(r * rows, (r + 1) * rows)
        pa = jnp.dot(attn_ref[rs, :], wa_ref[...], preferred_element_type=F32)
        pr = jnp.dot(ret_ref[rs, :], wr_ref[...], preferred_element_type=F32)
        ga = jnp.concatenate([ga0_ref[rs, :], ga1_ref[rs, :]], axis=1).astype(F32)
        gr = jnp.concatenate([gr0_ref[rs, :], gr1_ref[rs, :]], axis=1).astype(F32)
        merged = pa * _sigmoid(ga) + pr * _sigmoid(gr)
        x = x_ref[rs, :] + jnp.dot(merged.astype(BF16), wo_ref[...], preferred_element_type=F32)
        xo_ref[rs, :] = x
        h = _rms(x, gain_ref[...])
        h_hi = h.astype(BF16)
        h_ref[rs, :] = h_hi
        h_lo = (h - h_hi.astype(F32)).astype(BF16)
        logit_ref[rs, :] = (jnp.dot(h_hi, wrt_ref[...], preferred_element_type=F32)
                            + jnp.dot(h_lo, wrt_ref[...], preferred_element_type=F32)
                            + jnp.dot(h_hi, wrt_lo_ref[...], preferred_element_type=F32))


def _merge(attn, ret, gates, gate_col0, x2d, wa, wr, wo, gain, w_router, tm):
    T, D = x2d.shape
    gw = D // 2
    assert gate_col0 % gw == 0
    g0 = gate_col0 // gw
    w_router_pad = jnp.pad(w_router, ((0, 0), (0, LANES - w_router.shape[1])))
    wrt_hi = w_router_pad.astype(BF16)
    wrt_lo = (w_router_pad - wrt_hi.astype(F32)).astype(BF16)
    tok = lambda width, col=0: pl.BlockSpec((tm, width), lambda i: (i, col))
    full = lambda a: pl.BlockSpec(a.shape, lambda i: (0, 0))
    return pl.pallas_call(
        _merge_kernel,
        grid=(T // tm,),
        in_specs=[tok(ATTN_WIDTH), tok(D), tok(gw, g0), tok(gw, g0 + 1), tok(gw, g0 + 2), tok(gw, g0 + 3), tok(D),
                  full(wa), full(wr), full(wo), pl.BlockSpec((1, D), lambda i: (0, 0)), full(wrt_hi), full(wrt_lo)],
        out_specs=[tok(D), tok(D), tok(LANES)],
        out_shape=[jax.ShapeDtypeStruct((T, D), F32), jax.ShapeDtypeStruct((T, D), BF16),
                   jax.ShapeDtypeStruct((T, LANES), F32)],
        compiler_params=_params("parallel"),
        name="merge_out_proj",
    )(attn, ret, gates, gates, gates, gates, x2d, wa, wr, wo, gain.reshape(1, D), wrt_hi, wrt_lo)


def _route_kernel(logit_ref, pos_ref, pos_t_ref, aff_ref, before_ref, *, n_experts, cap):
    S = logit_ref.shape[1]
    rows = 256

    @pl.when(pl.program_id(0) == 0)
    def _():
        def fill(i, carry):
            r0 = pl.multiple_of(i * rows, rows)
            r = lax.broadcasted_iota(jnp.int32, (rows, S), 0) + r0
            c = lax.broadcasted_iota(jnp.int32, (rows, S), 1)
            before_ref[pl.ds(r0, rows), :] = jnp.where(r < c, 1.0, 0.0).astype(BF16)
            return carry
        lax.fori_loop(0, S // rows, fill, 0)

    lane = lax.broadcasted_iota(jnp.int32, (1, LANES), 1)
    lg = jnp.where(lane < n_experts, logit_ref[0], MASKED)
    ex = jnp.exp(lg - jnp.max(lg, axis=-1, keepdims=True))
    aff = (ex / jnp.sum(ex, axis=-1, keepdims=True)).T[:n_experts]

    def count_at_least(bits):
        return jnp.sum((aff >= lax.bitcast_convert_type(bits, F32)).astype(jnp.int32), axis=-1, keepdims=True)

    thr_bits = jnp.zeros((n_experts, 1), jnp.int32)
    for lo in range(29, 0, -2):
        for digit in (1, 2, 3):
            cand = thr_bits | (digit << lo)
            best = cand if digit == 1 else jnp.where(count_at_least(cand) >= cap, cand, best)
            if digit == 1:
                first_ok = count_at_least(cand) >= cap
        thr_bits = jnp.where(first_ok, best, thr_bits)
    thr_bits = jnp.where(count_at_least(thr_bits | 1) >= cap, thr_bits | 1, thr_bits)
    floor = lax.bitcast_convert_type(thr_bits, F32)
    thr = jnp.min(jnp.where(aff >= floor, aff, jnp.inf), axis=-1, keepdims=True)
    above = aff > thr
    tied = aff == thr
    need = cap - jnp.sum(above.astype(jnp.int32), axis=-1, keepdims=True)
    count_before = lambda picked: jnp.dot(picked.astype(BF16), before_ref[...], preferred_element_type=F32)
    tied_f = jnp.where(tied, 1.0, 0.0)
    all_tied_fit = jnp.all(jnp.sum(tied.astype(jnp.int32), axis=-1, keepdims=True) == need)
    tied_taken = lax.cond(all_tied_fit, lambda: tied_f,
                          lambda: jnp.where(count_before(tied_f) < need.astype(F32), tied_f, 0.0))
    chosen_f = jnp.where(above, 1.0, tied_taken)
    pos = jnp.where(chosen_f > 0.0, count_before(chosen_f), -1.0)
    pos_ref[0] = pos
    pos_t_ref[0] = jnp.concatenate([pos, jnp.full((LANES - n_experts, S), -1.0, F32)], axis=0).T
    aff_ref[0] = aff


def _route(logits, cap):
    B, S, _ = logits.shape
    E = N_EXPERTS
    return pl.pallas_call(
        functools.partial(_route_kernel, n_experts=E, cap=cap),
        grid=(B,),
        in_specs=[pl.BlockSpec((1, S, LANES), lambda b: (b, 0, 0))],
        out_specs=[pl.BlockSpec((1, E, S), lambda b: (b, 0, 0)), pl.BlockSpec((1, S, LANES), lambda b: (b, 0, 0)),
                   pl.BlockSpec((1, E, S), lambda b: (b, 0, 0))],
        out_shape=[jax.ShapeDtypeStruct((B, E, S), F32), jax.ShapeDtypeStruct((B, S, LANES), F32),
                   jax.ShapeDtypeStruct((B, E, S), F32)],
        scratch_shapes=[pltpu.VMEM((S, S), BF16)],
        compiler_params=_params("arbitrary"),
        name="route",
    )(logits)


def _gather_kernel(h_ref, pos_ref, aff_ref, x_ref, g_ref, *, n_experts, cap):
    S = h_ref.shape[1]
    slot = lax.broadcasted_iota(jnp.int32, (cap, S), 0)
    for e in range(n_experts):
        mask = slot == pos_ref[0, e:e + 1, :].astype(jnp.int32)
        onehot = jnp.where(mask, 1.0, 0.0).astype(BF16)
        x_ref[e, 0] = jnp.dot(onehot, h_ref[0], preferred_element_type=F32).astype(x_ref.dtype)
        g = jnp.sum(jnp.where(mask, aff_ref[0, e:e + 1, :], 0.0), axis=-1, keepdims=True)
        g_ref[e, 0] = jnp.broadcast_to(g, (cap, LANES))


def _gather(h, pos, aff, cap):
    B, S, D = h.shape
    E = pos.shape[1]
    rows = pl.BlockSpec((1, E, S), lambda b: (b, 0, 0))
    return pl.pallas_call(
        functools.partial(_gather_kernel, n_experts=E, cap=cap),
        grid=(B,),
        in_specs=[pl.BlockSpec((1, S, D), lambda b: (b, 0, 0)), rows, rows],
        out_specs=[pl.BlockSpec((E, 1, cap, D), lambda b: (0, b, 0, 0)),
                   pl.BlockSpec((E, 1, cap, LANES), lambda b: (0, b, 0, 0))],
        out_shape=[jax.ShapeDtypeStruct((E, B, cap, D), BF16), jax.ShapeDtypeStruct((E, B, cap, LANES), F32)],
        compiler_params=_params("parallel"),
        name="expert_gather",
    )(h, pos, aff)


EXPERT_ROW_BLOCKS = 4


def _expert_kernel(x_ref, g_ref, wg_ref, wu_ref, wd_ref, y_ref, act_ref, wd_bf_ref, chunk_ref, *, nf, tf):
    f = pl.program_id(1)
    wg = wg_ref[...].astype(BF16)
    wu = wu_ref[...].astype(BF16)
    wd_bf_ref[pl.ds(pl.multiple_of(f * tf, tf), tf), :] = wd_ref[...].astype(BF16)
    rows = x_ref.shape[1] // EXPERT_ROW_BLOCKS
    for r in range(EXPERT_ROW_BLOCKS):
        x = x_ref[0, r * rows:(r + 1) * rows, :]
        a = jnp.dot(x, wg, preferred_element_type=F32)
        u = jnp.dot(x, wu, preferred_element_type=F32)
        chunk_ref[r * rows:(r + 1) * rows, :] = (a * _sigmoid(a) * u).astype(BF16)
    for k in range(nf):
        @pl.when(f == k)
        def _(k=k):
            act_ref[:, k * tf:(k + 1) * tf] = chunk_ref[...]

    @pl.when(f == nf - 1)
    def _():
        y = jnp.dot(act_ref[...], wd_bf_ref[...], preferred_element_type=F32)
        y_ref[0] = (y * g_ref[0][:, :1]).astype(y_ref.dtype)


def _experts(xin, g, w_gate, w_up, w_down, layer, tf):
    E, M, D = xin.shape
    FF = w_gate.shape[3]
    return pl.pallas_call(
        functools.partial(_expert_kernel, nf=FF // tf, tf=tf),
        grid=(E, FF // tf),
        in_specs=[pl.BlockSpec((1, M, D), lambda e, f: (e, 0, 0)),
                  pl.BlockSpec((1, M, LANES), lambda e, f: (e, 0, 0)),
                  pl.BlockSpec((None, None, D, tf), lambda e, f: (layer, e, 0, f)),
                  pl.BlockSpec((None, None, D, tf), lambda e, f: (layer, e, 0, f)),
                  pl.BlockSpec((None, None, tf, D), lambda e, f: (layer, e, f, 0))],
        out_specs=pl.BlockSpec((1, M, D), lambda e, f: (e, 0, 0)),
        out_shape=jax.ShapeDtypeStruct((E, M, D), BF16),
        scratch_shapes=[pltpu.VMEM((M, FF), BF16), pltpu.VMEM((FF, D), BF16), pltpu.VMEM((M, tf), BF16)],
        compiler_params=_params("parallel", "arbitrary"),
        name="expert_swiglu",
    )(xin, g, w_gate, w_up, w_down)


def _scatter_kernel(x_ref, pos_t_ref, y_ref, gain_ref, o_ref, *, n_experts, cap, normalize):
    pos = pos_t_ref[0].astype(jnp.int32)
    slot = lax.broadcasted_iota(jnp.int32, (pos.shape[0], cap), 1)
    onehot = jnp.concatenate([jnp.where(pos[:, e:e + 1] == slot, 1.0, 0.0).astype(BF16) for e in range(n_experts)],
                             axis=1)
    y = y_ref[:, 0].reshape(n_experts * cap, y_ref.shape[-1])
    out = x_ref[0] + jnp.dot(onehot, y, preferred_element_type=F32)
    o_ref[0] = _rms(out, gain_ref[...]) if normalize else out


def _scatter(x, pos_t, y, cap, ts, gain, normalize):
    B, S, D = x.shape
    E = y.shape[0]
    return pl.pallas_call(
        functools.partial(_scatter_kernel, n_experts=E, cap=cap, normalize=normalize),
        grid=(B, S // ts),
        in_specs=[pl.BlockSpec((1, ts, D), lambda b, t: (b, t, 0)),
                  pl.BlockSpec((1, ts, LANES), lambda b, t: (b, t, 0)),
                  pl.BlockSpec((E, 1, cap, D), lambda b, t: (0, b, 0, 0)),
                  pl.BlockSpec((1, D), lambda b, t: (0, 0))],
        out_specs=pl.BlockSpec((1, ts, D), lambda b, t: (b, t, 0)),
        out_shape=jax.ShapeDtypeStruct((B, S, D), F32),
        compiler_params=_params("parallel", "arbitrary"),
        name="expert_scatter",
    )(x, pos_t, y, gain.reshape(1, D))


def kernel(x, w_in, w_attn_out, w_ret_out, w_out, ret_decay_logit, norm_mix, norm_ffn, w_router, w_gate, w_up,
           w_down, norm_final):
    B, S, D = x.shape
    T = B * S
    depth = w_in.shape[0]
    assert depth >= 1
    W = ATTN_WIDTH
    attn_in = len(ATTN_PATTERNS) * 3 * W
    n_in = w_in.shape[2]
    ret0, swish0, gates0 = attn_in, attn_in + 3 * D, attn_in + 4 * D
    cap = CAPACITY_FACTOR * S // N_EXPERTS
    x2d = x.reshape(T, D)
    w_in_b = w_in.astype(BF16)
    for layer in range(depth):
        proj = _norm_proj(x2d, norm_mix[layer], w_in_b, layer, 0, n_in, BF16, tm=2048, tn=1536)
        proj3 = proj.reshape(B, S, n_in)

        attn = _dilated_attention(proj3, 0)
        ret = _retention(proj3, ret_decay_logit[layer], ret0, swish0, D)

        x2d, h2, logits = _merge(attn.reshape(T, W), ret.reshape(T, D), proj, gates0, x2d,
                                 w_attn_out[layer].astype(BF16),
                                 w_ret_out[layer].astype(BF16), w_out[layer].astype(BF16), norm_ffn[layer],
                                 w_router[layer], tm=1024)

        pos, pos_t, aff = _route(logits.reshape(B, S, LANES), cap)
        xin, g = _gather(h2.reshape(B, S, D), pos, aff, cap)
        y = _experts(xin.reshape(N_EXPERTS, B * cap, D), g.reshape(N_EXPERTS, B * cap, LANES),
                     w_gate, w_up, w_down, layer, tf=256)
        x2d = _scatter(x2d.reshape(B, S, D), pos_t, y.reshape(N_EXPERTS, B, cap, D), cap, 1024, norm_final,
                       normalize=layer == depth - 1).reshape(T, D)
    return x2d.reshape(B, S, D)
```

```python
import functools

import jax
import jax.numpy as jnp
from jax import lax
from jax.experimental import pallas as pl
from jax.experimental.pallas import tpu as pltpu

EPS = 1e-6
ATTN_PATTERNS = ((128, 1), (512, 4), (2048, 16))
ATTN_HEADS = 8
ATTN_HEAD_DIM = 64
ATTN_WIDTH = ATTN_HEADS * ATTN_HEAD_DIM
RET_HEADS = 4
RET_CHUNK = 256
N_EXPERTS = 16
CAPACITY_FACTOR = 2

LANES = 128
MASKED = -1e30
VMEM_LIMIT_BYTES = 56 * 1024 * 1024

F32 = jnp.float32
BF16 = jnp.bfloat16


def _params(*semantics):
    return pltpu.CompilerParams(dimension_semantics=semantics, vmem_limit_bytes=VMEM_LIMIT_BYTES)


def _sigmoid(x):
    return 0.5 * (1.0 + jnp.tanh(0.5 * x))


def _rms(x, gain):
    return x * lax.rsqrt(jnp.mean(x * x, axis=-1, keepdims=True) + EPS) * gain


NORM_ROW_BLOCKS = 4


def _norm_proj_kernel(x_ref, gain_ref, w_ref, o_ref, h_ref):
    @pl.when(pl.program_id(1) == 0)
    def _():
        rows = x_ref.shape[0] // NORM_ROW_BLOCKS
        for r in range(NORM_ROW_BLOCKS):
            rs = slice(r * rows, (r + 1) * rows)
            h = _rms(x_ref[rs, :], gain_ref[...]).astype(BF16)
            h_ref[rs, :] = h
            o_ref[rs, :] = jnp.dot(h, w_ref[...], preferred_element_type=F32).astype(o_ref.dtype)

    @pl.when(pl.program_id(1) > 0)
    def _():
        o_ref[...] = jnp.dot(h_ref[...], w_ref[...], preferred_element_type=F32).astype(o_ref.dtype)


def _norm_proj(x2d, gain, w, layer, col0, ncols, out_dtype, tm, tn):
    T, D = x2d.shape
    assert T % tm == 0 and ncols % tn == 0 and col0 % tn == 0
    return pl.pallas_call(
        _norm_proj_kernel,
        grid=(T // tm, ncols // tn),
        in_specs=[pl.BlockSpec((tm, D), lambda i, j: (i, 0)),
                  pl.BlockSpec((1, D), lambda i, j: (0, 0)),
                  pl.BlockSpec((None, D, tn), lambda i, j: (layer, 0, j + col0 // tn))],
        out_specs=pl.BlockSpec((tm, tn), lambda i, j: (i, j)),
        out_shape=jax.ShapeDtypeStruct((T, ncols), out_dtype),
        scratch_shapes=[pltpu.VMEM((tm, D), BF16)],
        compiler_params=_params("parallel", "arbitrary"),
        name="norm_proj",
    )(x2d, gain.reshape(1, D), w)


ATTN_TQ = 128
ATTN_BLOCK = 1
ATTN_RING = 3
REGROUP_STRIDE = 4


def _attn_kernel(*refs, S, patterns):
    n_groups = len(patterns)
    in_refs, o_ref = refs[:3 * n_groups], refs[3 * n_groups]
    (stage_ref, stage2_ref, qs_ref, ks_ref, vs_ref, acc_ref, den_ref, max_ref, bias_ref, s_ref,
     m_ref) = refs[3 * n_groups + 1:]
    tq = ATTN_TQ
    n_tiles = S // tq
    hp = pl.program_id(0)
    lane = lax.broadcasted_iota(jnp.int32, (1, LANES), 1)
    first_head = lane < ATTN_HEAD_DIM
    geometry = []

    for gi, (window, d) in enumerate(patterns):
        half = window // (2 * d)
        tk = tq + 2 * half
        L = S // d
        seg = L + 2 * half
        geometry.append((d, half, tk, L // tq))
        q_in, k_in, v_in = in_refs[3 * gi:3 * gi + 3]

        @pl.when(pl.program_id(1) == 0)
        def _(gi=gi, d=d, half=half, tk=tk):
            r = lax.broadcasted_iota(jnp.int32, (tq, tk), 0)
            c = lax.broadcasted_iota(jnp.int32, (tq, tk), 1)
            rel = jnp.abs(c - half - r)
            dist = (d * rel).astype(F32)
            for hh in range(2):
                pow2 = jnp.full((tq, tk), jnp.left_shift(jnp.int32(2), 2 * hp + hh), jnp.int32).astype(F32)
                band = jnp.where(rel <= half, -(dist / pow2), MASKED)
                for kind in range(4):
                    tile_bias = band
                    if kind & 1:
                        tile_bias = jnp.where(c < half, MASKED, tile_bias)
                    if kind & 2:
                        tile_bias = jnp.where(c >= tq + half, MASKED, tile_bias)
                    bias_ref[gi, kind, hh * tq:(hh + 1) * tq, :] = tile_bias
            vs_ref[gi, :, LANES:] = jnp.ones((vs_ref.shape[1], LANES), BF16)

        zeros = jnp.zeros((half, LANES), BF16)
        if d == 1:
            ks_ref[gi, pl.ds(0, half), :] = zeros
            ks_ref[gi, pl.ds(half, S), :] = k_in[0]
            ks_ref[gi, pl.ds(half + S, half), :] = zeros
            vs_ref[gi, pl.ds(0, half), :LANES] = zeros
            vs_ref[gi, pl.ds(half, S), :LANES] = v_in[0]
            vs_ref[gi, pl.ds(half + S, half), :LANES] = zeros
        else:
            stage_ref[0] = q_in[0].astype(F32)
            stage_ref[1] = k_in[0].astype(F32)
            stage_ref[2] = v_in[0].astype(F32)
            d1 = d // REGROUP_STRIDE if d > REGROUP_STRIDE else 1
            src_ref = stage_ref
            if d1 > 1:
                assert d1 <= REGROUP_STRIDE
                part = S // REGROUP_STRIDE
                for t in range(3):
                    for r0 in range(REGROUP_STRIDE):
                        stage2_ref[t, r0 * part:(r0 + 1) * part, :] = stage_ref[t, pl.ds(r0, part, stride=REGROUP_STRIDE), :]
                src_ref = stage2_ref

            def regroup(r, carry, gi=gi, d=d, d1=d1, L=L, seg=seg, half=half, src_ref=src_ref):
                if d1 > 1:
                    rows = pl.ds((r % REGROUP_STRIDE) * (S // REGROUP_STRIDE) + r // REGROUP_STRIDE, L, stride=d1)
                else:
                    rows = pl.ds(r, L, stride=d)
                k0 = r * seg
                qs_ref[gi, pl.ds(r * L, L), :] = src_ref[0, rows, :].astype(BF16)
                ks_ref[gi, pl.ds(k0, half), :] = zeros
                ks_ref[gi, pl.ds(k0 + half, L), :] = src_ref[1, rows, :].astype(BF16)
                ks_ref[gi, pl.ds(k0 + half + L, half), :] = zeros
                vs_ref[gi, pl.ds(k0, half), :LANES] = zeros
                vs_ref[gi, pl.ds(k0 + half, L), :LANES] = src_ref[2, rows, :].astype(BF16)
                vs_ref[gi, pl.ds(k0 + half + L, half), :LANES] = zeros
                return carry

            for r in range(d):
                regroup(r, 0)

    def key_row(gi, u):
        d, half, tk, nblk = geometry[gi]
        return (u + u // nblk) * tq

    def scores(gi, u, slot):
        d, half, tk, nblk = geometry[gi]
        assert 2 * half == tq
        i = u % nblk
        kind = (1 if i == 0 else 0) + (2 if i == nblk - 1 else 0)
        qp = in_refs[3 * gi][0, u * tq:(u + 1) * tq, :] if d == 1 else qs_ref[gi, u * tq:(u + 1) * tq, :]
        qp = qp * (ATTN_HEAD_DIM ** -0.5)
        none = jnp.zeros_like(qp)
        q2 = jnp.concatenate([jnp.where(first_head, qp, none), jnp.where(first_head, none, qp)], axis=0)
        k0 = key_row(gi, u)
        s = lax.dot_general(q2, ks_ref[gi, k0:k0 + tk, :], (((1,), (1,)), ((), ())), preferred_element_type=F32)
        s_ref[slot] = s + bias_ref[gi, kind]

    def maxima(slot):
        m_ref[slot] = jnp.broadcast_to(jnp.max(s_ref[slot], axis=-1, keepdims=True), (2 * tq, LANES))

    def values(gi, u, slot):
        d, half, tk, nblk = geometry[gi]
        r, i = u // nblk, u % nblk
        m = m_ref[slot]
        p = jnp.concatenate([jnp.exp(s_ref[slot, :, c0:c0 + LANES] - m) for c0 in range(0, tk, LANES)], axis=1)
        k0 = key_row(gi, u)
        pv = jnp.dot(p.astype(BF16), vs_ref[gi, k0:k0 + tk, :], preferred_element_type=F32)
        rows = pl.ds(u * tq, tq) if d == 1 else pl.ds(i * (tq * d) + r, tq, stride=d)
        acc_ref[gi, rows, :] = jnp.where(first_head, pv[:tq, :LANES], pv[tq:, :LANES])
        den_ref[gi, rows, :] = jnp.where(first_head, pv[:tq, LANES:], pv[tq:, LANES:])
        max_ref[gi, rows, :] = jnp.where(first_head, m[:tq], m[tq:])

    blocks = [(gi, j) for gi in range(n_groups) for j in range(n_tiles // ATTN_BLOCK)]
    for step in range(len(blocks) + 2):
        for stage, fn in enumerate((scores, maxima, values)):
            t = step - stage
            if 0 <= t < len(blocks):
                gi, j = blocks[t]
                for k in range(ATTN_BLOCK):
                    slot = (t % ATTN_RING) * ATTN_BLOCK + k
                    if fn is maxima:
                        fn(slot)
                    else:
                        fn(gi, j * ATTN_BLOCK + k, slot)

    def merge(j, carry):
        rows = pl.ds(pl.multiple_of(j * tq, tq), tq)
        top = functools.reduce(jnp.maximum, [max_ref[g, rows, :] for g in range(n_groups)])
        w = [jnp.exp(max_ref[g, rows, :] - top) for g in range(n_groups)]
        num = sum(w[g] * acc_ref[g, rows, :] for g in range(n_groups))
        den = sum(w[g] * den_ref[g, rows, :] for g in range(n_groups))
        o_ref[0, rows, :] = (num / den).astype(o_ref.dtype)
        return carry

    lax.fori_loop(0, n_tiles, merge, 0)


def _dilated_attention(proj, col0):
    B, S, _ = proj.shape
    W = ATTN_WIDTH
    n_groups = len(ATTN_PATTERNS)
    halves = {w // (2 * d) for w, d in ATTN_PATTERNS}
    assert len(halves) == 1 and all(S % (d * ATTN_TQ) == 0 for _, d in ATTN_PATTERNS)
    half = halves.pop()
    tk = ATTN_TQ + 2 * half
    assert tk % LANES == 0 and (S // ATTN_TQ) % ATTN_BLOCK == 0
    pairs = W // LANES
    col = lambda c0: pl.BlockSpec((1, S, LANES), lambda hp, b: (b, 0, c0 // LANES + hp))
    return pl.pallas_call(
        functools.partial(_attn_kernel, S=S, patterns=ATTN_PATTERNS),
        grid=(pairs, B),
        in_specs=[col(col0 + (3 * g + t) * W) for g in range(n_groups) for t in range(3)],
        out_specs=pl.BlockSpec((1, S, LANES), lambda hp, b: (b, 0, hp)),
        out_shape=jax.ShapeDtypeStruct((B, S, W), BF16),
        scratch_shapes=[pltpu.VMEM((3, S, LANES), F32),
                        pltpu.VMEM((3, S, LANES), F32),
                        pltpu.VMEM((n_groups, S, LANES), BF16),
                        pltpu.VMEM((n_groups, 2 * S, LANES), BF16),
                        pltpu.VMEM((n_groups, 2 * S, 2 * LANES), BF16),
                        pltpu.VMEM((n_groups, S, LANES), F32),
                        pltpu.VMEM((n_groups, S, LANES), F32),
                        pltpu.VMEM((n_groups, S, LANES), F32),
                        pltpu.VMEM((n_groups, 4, 2 * ATTN_TQ, tk), F32),
                        pltpu.VMEM((ATTN_RING * ATTN_BLOCK, 2 * ATTN_TQ, tk), F32),
                        pltpu.VMEM((ATTN_RING * ATTN_BLOCK, 2 * ATTN_TQ, LANES), F32)],
        compiler_params=_params("arbitrary", "arbitrary"),
        name="dilated_attention",
    )(*([proj] * (3 * n_groups)))


def _log_sigmoid(x):
    return jnp.minimum(x, 0.0) - jnp.log1p(jnp.exp(-jnp.abs(x)))


def _retention_kernel(logit_ref, q_ref, k_ref, v_ref, gate_ref, o_ref, yf_ref, yb_ref, sf_ref, sb_ref):
    C = RET_CHUNK
    S, Dk = q_ref.shape[1], q_ref.shape[2]
    N = S // C
    h = pl.program_id(1)
    lf = _log_sigmoid(jnp.full((C, 1), logit_ref[0, h], F32))
    lb = _log_sigmoid(jnp.full((C, 1), logit_ref[1, h], F32))
    rel = (lax.broadcasted_iota(jnp.int32, (C, C), 0) - lax.broadcasted_iota(jnp.int32, (C, C), 1)).astype(F32)
    decay = jnp.exp(jnp.where(rel >= 0, lf * rel, -lb * rel))
    pos = lax.broadcasted_iota(jnp.int32, (C, 1), 0).astype(F32)
    xi_f = jnp.exp(lf * (pos + 1.0))
    zeta_f = jnp.exp(lf * (C - 1.0 - pos))
    xi_b = jnp.exp(lb * (C - pos))
    zeta_b = jnp.exp(lb * pos)
    chunk_f = jnp.exp(_log_sigmoid(jnp.full((Dk, 1), logit_ref[0, h], F32)) * C)
    chunk_b = jnp.exp(_log_sigmoid(jnp.full((Dk, 1), logit_ref[1, h], F32)) * C)
    scale = Dk ** -0.5

    def load(i):
        r0 = pl.multiple_of(i * C, C)
        return r0, q_ref[0, pl.ds(r0, C), :] * scale, k_ref[0, pl.ds(r0, C), :], v_ref[0, pl.ds(r0, C), :]

    def cross(state_ref, qc, kc, vc, xi, zeta, chunk_decay):
        st = state_ref[...]
        y = jnp.dot(qc, st.astype(BF16), preferred_element_type=F32) * xi
        kz = (kc.astype(F32) * zeta).astype(BF16)
        u = lax.dot_general(kz, vc, (((0,), (0,)), ((), ())), preferred_element_type=F32)
        state_ref[...] = st * chunk_decay + u
        return y

    sf_ref[...] = jnp.zeros_like(sf_ref)
    sb_ref[...] = jnp.zeros_like(sb_ref)

    def finish(rows, r):
        r = r * lax.rsqrt(jnp.mean(r * r, axis=-1, keepdims=True) + EPS)
        g = gate_ref[0, rows, :].astype(F32)
        o_ref[0, rows, :] = (g * _sigmoid(g) * r).astype(o_ref.dtype)

    def both(i):
        r0, qc, kc, vc = load(i)
        s = lax.dot_general(qc, kc, (((1,), (1,)), ((), ())), preferred_element_type=F32) * decay
        yf = jnp.dot(s.astype(BF16), vc, preferred_element_type=F32) + cross(sf_ref, qc, kc, vc, xi_f, zeta_f, chunk_f)
        r1, qd, kd, vd = load(N - 1 - i)
        yb = cross(sb_ref, qd, kd, vd, xi_b, zeta_b, chunk_b)
        return pl.ds(r0, C), yf, pl.ds(r1, C), yb

    def first_half(i, carry):
        rows_f, yf, rows_b, yb = both(i)
        yf_ref[rows_f, :] = yf
        yb_ref[rows_b, :] = yb
        return carry

    def second_half(i, carry):
        rows_f, yf, rows_b, yb = both(i)
        finish(rows_f, yf + yb_ref[rows_f, :])
        finish(rows_b, yf_ref[rows_b, :] + yb)
        return carry

    assert N % 2 == 0
    lax.fori_loop(0, N // 2, first_half, 0, unroll=4)
    lax.fori_loop(N // 2, N, second_half, 0, unroll=4)


def _retention(proj, decay_logit, q_col0, gate_col0, width):
    B, S, _ = proj.shape
    Dk = width // RET_HEADS
    qb = q_col0 // Dk
    gb = gate_col0 // Dk
    nb = RET_HEADS
    return pl.pallas_call(
        _retention_kernel,
        grid=(B, RET_HEADS),
        in_specs=[pl.BlockSpec(memory_space=pltpu.SMEM),
                  pl.BlockSpec((1, S, Dk), lambda b, h: (b, 0, qb + h)),
                  pl.BlockSpec((1, S, Dk), lambda b, h: (b, 0, qb + nb + h)),
                  pl.BlockSpec((1, S, Dk), lambda b, h: (b, 0, qb + 2 * nb + h)),
                  pl.BlockSpec((1, S, Dk), lambda b, h: (b, 0, gb + h))],
        out_specs=pl.BlockSpec((1, S, Dk), lambda b, h: (b, 0, h)),
        out_shape=jax.ShapeDtypeStruct((B, S, RET_HEADS * Dk), BF16),
        scratch_shapes=[pltpu.VMEM((S, Dk), F32), pltpu.VMEM((S, Dk), F32),
                        pltpu.VMEM((Dk, Dk), F32), pltpu.VMEM((Dk, Dk), F32)],
        compiler_params=_params("parallel", "parallel"),
        name="retention",
    )(decay_logit, proj, proj, proj, proj)


MERGE_ROW_BLOCKS = 4


def _merge_kernel(attn_ref, ret_ref, ga0_ref, ga1_ref, gr0_ref, gr1_ref, x_ref, wa_ref, wr_ref, wo_ref, gain_ref,
                  wrt_ref, wrt_lo_ref, xo_ref, h_ref, logit_ref):
    tm = x_ref.shape[0]
    rows = tm // MERGE_ROW_BLOCKS
    for r in range(MERGE_ROW_BLOCKS):
        rs = slice(r * rows, (r + 1) * rows)
        pa = jnp.dot(attn_ref[rs, :], wa_ref[...], preferred_element_type=F32)
        pr = jnp.dot(ret_ref[rs, :], wr_ref[...], preferred_element_type=F32)
        ga = jnp.concatenate([ga0_ref[rs, :], ga1_ref[rs, :]], axis=1).astype(F32)
        gr = jnp.concatenate([gr0_ref[rs, :], gr1_ref[rs, :]], axis=1).astype(F32)
        merged = pa * _sigmoid(ga) + pr * _sigmoid(gr)
        x = x_ref[rs, :] + jnp.dot(merged.astype(BF16), wo_ref[...], preferred_element_type=F32)
        xo_ref[rs, :] = x
        h = _rms(x, gain_ref[...])
        h_hi = h.astype(BF16)
        h_ref[rs, :] = h_hi
        h_lo = (h - h_hi.astype(F32)).astype(BF16)
        logit_ref[rs, :] = (jnp.dot(h_hi, wrt_ref[...], preferred_element_type=F32)
                            + jnp.dot(h_lo, wrt_ref[...], preferred_element_type=F32)
                            + jnp.dot(h_hi, wrt_lo_ref[...], preferred_element_type=F32))


def _merge(attn, ret, gates, gate_col0, x2d, wa, wr, wo, gain, w_router, tm):
    T, D = x2d.shape
    gw = D // 2
    assert gate_col0 % gw == 0
    g0 = gate_col0 // gw
    w_router_pad = jnp.pad(w_router, ((0, 0), (0, LANES - w_router.shape[1])))
    wrt_hi = w_router_pad.astype(BF16)
    wrt_lo = (w_router_pad - wrt_hi.astype(F32)).astype(BF16)
    tok = lambda width, col=0: pl.BlockSpec((tm, width), lambda i: (i, col))
    full = lambda a: pl.BlockSpec(a.shape, lambda i: (0, 0))
    return pl.pallas_call(
        _merge_kernel,
        grid=(T // tm,),
        in_specs=[tok(ATTN_WIDTH), tok(D), tok(gw, g0), tok(gw, g0 + 1), tok(gw, g0 + 2), tok(gw, g0 + 3), tok(D),
                  full(wa), full(wr), full(wo), pl.BlockSpec((1, D), lambda i: (0, 0)), full(wrt_hi), full(wrt_lo)],
        out_specs=[tok(D), tok(D), tok(LANES)],
        out_shape=[jax.ShapeDtypeStruct((T, D), F32), jax.ShapeDtypeStruct((T, D), BF16),
                   jax.ShapeDtypeStruct((T, LANES), F32)],
        compiler_params=_params("parallel"),
        name="merge_out_proj",
    )(attn, ret, gates, gates, gates, gates, x2d, wa, wr, wo, gain.reshape(1, D), wrt_hi, wrt_lo)


def _route_kernel(logit_ref, pos_ref, pos_t_ref, aff_ref, before_ref, *, n_experts, cap):
    S = logit_ref.shape[1]
    rows = 256

    @pl.when(pl.program_id(0) == 0)
    def _():
        def fill(i, carry):
            r0 = pl.multiple_of(i * rows, rows)
            r = lax.broadcasted_iota(jnp.int32, (rows, S), 0) + r0
            c = lax.broadcasted_iota(jnp.int32, (rows, S), 1)
            before_ref[pl.ds(r0, rows), :] = jnp.where(r < c, 1.0, 0.0).astype(BF16)
            return carry
        lax.fori_loop(0, S // rows, fill, 0)

    lane = lax.broadcasted_iota(jnp.int32, (1, LANES), 1)
    lg = jnp.where(lane < n_experts, logit_ref[0], MASKED)
    ex = jnp.exp(lg - jnp.max(lg, axis=-1, keepdims=True))
    aff = (ex / jnp.sum(ex, axis=-1, keepdims=True)).T[:n_experts]

    def count_at_least(bits):
        return jnp.sum((aff >= lax.bitcast_convert_type(bits, F32)).astype(jnp.int32), axis=-1, keepdims=True)

    thr_bits = jnp.zeros((n_experts, 1), jnp.int32)
    for lo in range(29, 0, -2):
        for digit in (1, 2, 3):
            cand = thr_bits | (digit << lo)
            best = cand if digit == 1 else jnp.where(count_at_least(cand) >= cap, cand, best)
            if digit == 1:
                first_ok = count_at_least(cand) >= cap
        thr_bits = jnp.where(first_ok, best, thr_bits)
    thr_bits = jnp.where(count_at_least(thr_bits | 1) >= cap, thr_bits | 1, thr_bits)
    floor = lax.bitcast_convert_type(thr_bits, F32)
    thr = jnp.min(jnp.where(aff >= floor, aff, jnp.inf), axis=-1, keepdims=True)
    above = aff > thr
    tied = aff == thr
    need = cap - jnp.sum(above.astype(jnp.int32), axis=-1, keepdims=True)
    count_before = lambda picked: jnp.dot(picked.astype(BF16), before_ref[...], preferred_element_type=F32)
    tied_f = jnp.where(tied, 1.0, 0.0)
    all_tied_fit = jnp.all(jnp.sum(tied.astype(jnp.int32), axis=-1, keepdims=True) == need)
    tied_taken = lax.cond(all_tied_fit, lambda: tied_f,
                          lambda: jnp.where(count_before(tied_f) < need.astype(F32), tied_f, 0.0))
    chosen_f = jnp.where(above, 1.0, tied_taken)
    pos = jnp.where(chosen_f > 0.0, count_before(chosen_f), -1.0)
    pos_ref[0] = pos
    pos_t_ref[0] = jnp.concatenate([pos, jnp.full((LANES - n_experts, S), -1.0, F32)], axis=0).T
    aff_ref[0] = aff


def _route(logits, cap):
    B, S, _ = logits.shape
    E = N_EXPERTS
    return pl.pallas_call(
        functools.partial(_route_kernel, n_experts=E, cap=cap),
        grid=(B,),
        in_specs=[pl.BlockSpec((1, S, LANES), lambda b: (b, 0, 0))],
        out_specs=[pl.BlockSpec((1, E, S), lambda b: (b, 0, 0)), pl.BlockSpec((1, S, LANES), lambda b: (b, 0, 0)),
                   pl.BlockSpec((1, E, S), lambda b: (b, 0, 0))],
        out_shape=[jax.ShapeDtypeStruct((B, E, S), F32), jax.ShapeDtypeStruct((B, S, LANES), F32),
                   jax.ShapeDtypeStruct((B, E, S), F32)],
        scratch_shapes=[pltpu.VMEM((S, S), BF16)],
        compiler_params=_params("arbitrary"),
        name="route",
    )(logits)


def _gather_kernel(h_ref, pos_ref, aff_ref, x_ref, g_ref, *, n_experts, cap):
    S = h_ref.shape[1]
    slot = lax.broadcasted_iota(jnp.int32, (cap, S), 0)
    for e in range(n_experts):
        mask = slot == pos_ref[0, e:e + 1, :].astype(jnp.int32)
        onehot = jnp.where(mask, 1.0, 0.0).astype(BF16)
        x_ref[e, 0] = jnp.dot(onehot, h_ref[0], preferred_element_type=F32).astype(x_ref.dtype)
        g = jnp.sum(jnp.where(mask, aff_ref[0, e:e + 1, :], 0.0), axis=-1, keepdims=True)
        g_ref[e, 0] = jnp.broadcast_to(g, (cap, LANES))


def _gather(h, pos, aff, cap):
    B, S, D = h.shape
    E = pos.shape[1]
    rows = pl.BlockSpec((1, E, S), lambda b: (b, 0, 0))
    return pl.pallas_call(
        functools.partial(_gather_kernel, n_experts=E, cap=cap),
        grid=(B,),
        in_specs=[pl.BlockSpec((1, S, D), lambda b: (b, 0, 0)), rows, rows],
        out_specs=[pl.BlockSpec((E, 1, cap, D), lambda b: (0, b, 0, 0)),
                   pl.BlockSpec((E, 1, cap, LANES), lambda b: (0, b, 0, 0))],
        out_shape=[jax.ShapeDtypeStruct((E, B, cap, D), BF16), jax.ShapeDtypeStruct((E, B, cap, LANES), F32)],
        compiler_params=_params("parallel"),
        name="expert_gather",
    )(h, pos, aff)


EXPERT_ROW_BLOCKS = 4


def _expert_kernel(x_ref, g_ref, wg_ref, wu_ref, wd_ref, y_ref, act_ref, wd_bf_ref, chunk_ref, *, nf, tf):
    f = pl.program_id(1)
    wg = wg_ref[...].astype(BF16)
    wu = wu_ref[...].astype(BF16)
    wd_bf_ref[pl.ds(pl.multiple_of(f * tf, tf), tf), :] = wd_ref[...].astype(BF16)
    rows = x_ref.shape[1] // EXPERT_ROW_BLOCKS
    for r in range(EXPERT_ROW_BLOCKS):
        x = x_ref[0, r * rows:(r + 1) * rows, :]
        a = jnp.dot(x, wg, preferred_element_type=F32)
        u = jnp.dot(x, wu, preferred_element_type=F32)
        chunk_ref[r * rows:(r + 1) * rows, :] = (a * _sigmoid(a) * u).astype(BF16)
    for k in range(nf):
        @pl.when(f == k)
        def _(k=k):
            act_ref[:, k * tf:(k + 1) * tf] = chunk_ref[...]

    @pl.when(f == nf - 1)
    def _():
        y = jnp.dot(act_ref[...], wd_bf_ref[...], preferred_element_type=F32)
        y_ref[0] = (y * g_ref[0][:, :1]).astype(y_ref.dtype)


def _experts(xin, g, w_gate, w_up, w_down, layer, tf):
    E, M, D = xin.shape
    FF = w_gate.shape[3]
    return pl.pallas_call(
        functools.partial(_expert_kernel, nf=FF // tf, tf=tf),
        grid=(E, FF // tf),
        in_specs=[pl.BlockSpec((1, M, D), lambda e, f: (e, 0, 0)),
                  pl.BlockSpec((1, M, LANES), lambda e, f: (e, 0, 0)),
                  pl.BlockSpec((None, None, D, tf), lambda e, f: (layer, e, 0, f)),
                  pl.BlockSpec((None, None, D, tf), lambda e, f: (layer, e, 0, f)),
                  pl.BlockSpec((None, None, tf, D), lambda e, f: (layer, e, f, 0))],
        out_specs=pl.BlockSpec((1, M, D), lambda e, f: (e, 0, 0)),
        out_shape=jax.ShapeDtypeStruct((E, M, D), BF16),
        scratch_shapes=[pltpu.VMEM((M, FF), BF16), pltpu.VMEM((FF, D), BF16), pltpu.VMEM((M, tf), BF16)],
        compiler_params=_params("parallel", "arbitrary"),
        name="expert_swiglu",
    )(xin, g, w_gate, w_up, w_down)


def _scatter_kernel(x_ref, pos_t_ref, y_ref, gain_ref, o_ref, *, n_experts, cap, normalize):
    pos = pos_t_ref[0].astype(jnp.int32)
    slot = lax.broadcasted_iota(jnp.int32, (pos.shape[0], cap), 1)
    onehot = jnp.concatenate([jnp.where(pos[:, e:e + 1] == slot, 1.0, 0.0).astype(BF16) for e in range(n_experts)],
                             axis=1)
    y = y_ref[:, 0].reshape(n_experts * cap, y_ref.shape[-1])
    out = x_ref[0] + jnp.dot(onehot, y, preferred_element_type=F32)
    o_ref[0] = _rms(out, gain_ref[...]) if normalize else out


def _scatter(x, pos_t, y, cap, ts, gain, normalize):
    B, S, D = x.shape
    E = y.shape[0]
    return pl.pallas_call(
        functools.partial(_scatter_kernel, n_experts=E, cap=cap, normalize=normalize),
        grid=(B, S // ts),
        in_specs=[pl.BlockSpec((1, ts, D), lambda b, t: (b, t, 0)),
                  pl.BlockSpec((1, ts, LANES), lambda b, t: (b, t, 0)),
                  pl.BlockSpec((E, 1, cap, D), lambda b, t: (0, b, 0, 0)),
                  pl.BlockSpec((1, D), lambda b, t: (0, 0))],
        out_specs=pl.BlockSpec((1, ts, D), lambda b, t: (b, t, 0)),
        out_shape=jax.ShapeDtypeStruct((B, S, D), F32),
        compiler_params=_params("parallel", "arbitrary"),
        name="expert_scatter",
    )(x, pos_t, y, gain.reshape(1, D))


def kernel(x, w_in, w_attn_out, w_ret_out, w_out, ret_decay_logit, norm_mix, norm_ffn, w_router, w_gate, w_up,
           w_down, norm_final):
    B, S, D = x.shape
    T = B * S
    depth = w_in.shape[0]
    assert depth >= 1
    W = ATTN_WIDTH
    attn_in = len(ATTN_PATTERNS) * 3 * W
    n_in = w_in.shape[2]
    ret0, swish0, gates0 = attn_in, attn_in + 3 * D, attn_in + 4 * D
    cap = CAPACITY_FACTOR * S // N_EXPERTS
    x2d = x.reshape(T, D)
    w_in_b = w_in.astype(BF16)
    for layer in range(depth):
        proj = _norm_proj(x2d, norm_mix[layer], w_in_b, layer, 0, n_in, BF16, tm=2048, tn=1536)
        proj3 = proj.reshape(B, S, n_in)

        attn = _dilated_attention(proj3, 0)
        ret = _retention(proj3, ret_decay_logit[layer], ret0, swish0, D)

        x2d, h2, logits = _merge(attn.reshape(T, W), ret.reshape(T, D), proj, gates0, x2d,
                                 w_attn_out[layer].astype(BF16),
                                 w_ret_out[layer].astype(BF16), w_out[layer].astype(BF16), norm_ffn[layer],
                                 w_router[layer], tm=1024)

        pos, pos_t, aff = _route(logits.reshape(B, S, LANES), cap)
        xin, g = _gather(h2.reshape(B, S, D), pos, aff, cap)
        y = _experts(xin.reshape(N_EXPERTS, B * cap, D), g.reshape(N_EXPERTS, B * cap, LANES),
                     w_gate, w_up, w_down, layer, tf=256)
        x2d = _scatter(x2d.reshape(B, S, D), pos_t, y.reshape(N_EXPERTS, B, cap, D), cap, 1024, norm_final,
                       normalize=layer == depth - 1).reshape(T, D)
    return x2d.reshape(B, S, D)
```

```python
import functools

import jax
import jax.numpy as jnp
from jax import lax
from jax.experimental import pallas as pl
from jax.experimental.pallas import tpu as pltpu

EPS = 1e-6
ATTN_PATTERNS = ((128, 1), (512, 4), (2048, 16))
ATTN_HEADS = 8
ATTN_HEAD_DIM = 64
ATTN_WIDTH = ATTN_HEADS * ATTN_HEAD_DIM
RET_HEADS = 4
RET_CHUNK = 256
N_EXPERTS = 16
CAPACITY_FACTOR = 2

LANES = 128
MASKED = -1e30
VMEM_LIMIT_BYTES = 56 * 1024 * 1024

F32 = jnp.float32
BF16 = jnp.bfloat16


def _params(*semantics):
    return pltpu.CompilerParams(dimension_semantics=semantics, vmem_limit_bytes=VMEM_LIMIT_BYTES)


def _sigmoid(x):
    return 0.5 * (1.0 + jnp.tanh(0.5 * x))


def _rms(x, gain):
    return x * lax.rsqrt(jnp.mean(x * x, axis=-1, keepdims=True) + EPS) * gain


NORM_ROW_BLOCKS = 4


def _norm_proj_kernel(x_ref, gain_ref, w_ref, o_ref, h_ref):
    @pl.when(pl.program_id(1) == 0)
    def _():
        rows = x_ref.shape[0] // NORM_ROW_BLOCKS
        for r in range(NORM_ROW_BLOCKS):
            rs = slice(r * rows, (r + 1) * rows)
            h = _rms(x_ref[rs, :], gain_ref[...]).astype(BF16)
            h_ref[rs, :] = h
            o_ref[rs, :] = jnp.dot(h, w_ref[...], preferred_element_type=F32).astype(o_ref.dtype)

    @pl.when(pl.program_id(1) > 0)
    def _():
        o_ref[...] = jnp.dot(h_ref[...], w_ref[...], preferred_element_type=F32).astype(o_ref.dtype)


def _norm_proj(x2d, gain, w, layer, col0, ncols, out_dtype, tm, tn):
    T, D = x2d.shape
    assert T % tm == 0 and ncols % tn == 0 and col0 % tn == 0
    return pl.pallas_call(
        _norm_proj_kernel,
        grid=(T // tm, ncols // tn),
        in_specs=[pl.BlockSpec((tm, D), lambda i, j: (i, 0)),
                  pl.BlockSpec((1, D), lambda i, j: (0, 0)),
                  pl.BlockSpec((None, D, tn), lambda i, j: (layer, 0, j + col0 // tn))],
        out_specs=pl.BlockSpec((tm, tn), lambda i, j: (i, j)),
        out_shape=jax.ShapeDtypeStruct((T, ncols), out_dtype),
        scratch_shapes=[pltpu.VMEM((tm, D), BF16)],
        compiler_params=_params("parallel", "arbitrary"),
        name="norm_proj",
    )(x2d, gain.reshape(1, D), w)


ATTN_TQ = 128
ATTN_BLOCK = 1
ATTN_RING = 3
REGROUP_STRIDE = 4


def _attn_kernel(*refs, S, patterns):
    n_groups = len(patterns)
    in_ref, o_ref = refs[0], refs[1]
    (stage_ref, stage2_ref, qs_ref, ks_ref, vs_ref, acc_ref, den_ref, max_ref, bias_ref, s_ref,
     m_ref) = refs[2:]
    tq = ATTN_TQ
    n_tiles = S // tq
    hp = pl.program_id(0)
    lane = lax.broadcasted_iota(jnp.int32, (1, LANES), 1)
    first_head = lane < ATTN_HEAD_DIM
    geometry = []

    for gi, (window, d) in enumerate(patterns):
        half = window // (2 * d)
        tk = tq + 2 * half
        L = S // d
        seg = L + 2 * half
        geometry.append((d, half, tk, L // tq))
        q_in, k_in, v_in = (slice((3 * gi + t) * LANES, (3 * gi + t + 1) * LANES) for t in range(3))

        @pl.when(pl.program_id(1) == 0)
        def _(gi=gi, d=d, half=half, tk=tk):
            r = lax.broadcasted_iota(jnp.int32, (tq, tk), 0)
            c = lax.broadcasted_iota(jnp.int32, (tq, tk), 1)
            rel = jnp.abs(c - half - r)
            dist = (d * rel).astype(F32)
            for hh in range(2):
                pow2 = jnp.full((tq, tk), jnp.left_shift(jnp.int32(2), 2 * hp + hh), jnp.int32).astype(F32)
                band = jnp.where(rel <= half, -(dist / pow2), MASKED)
                for kind in range(4):
                    tile_bias = band
                    if kind & 1:
                        tile_bias = jnp.where(c < half, MASKED, tile_bias)
                    if kind & 2:
                        tile_bias = jnp.where(c >= tq + half, MASKED, tile_bias)
                    bias_ref[gi, kind, hh * tq:(hh + 1) * tq, :] = tile_bias
            vs_ref[gi, :, LANES:] = jnp.ones((vs_ref.shape[1], LANES), BF16)

        zeros = jnp.zeros((half, LANES), BF16)
        if d == 1:
            ks_ref[gi, pl.ds(0, half), :] = zeros
            ks_ref[gi, pl.ds(half, S), :] = in_ref[0, :, k_in]
            ks_ref[gi, pl.ds(half + S, half), :] = zeros
            vs_ref[gi, pl.ds(0, half), :LANES] = zeros
            vs_ref[gi, pl.ds(half, S), :LANES] = in_ref[0, :, v_in]
            vs_ref[gi, pl.ds(half + S, half), :LANES] = zeros
        else:
            stage_ref[0] = in_ref[0, :, q_in].astype(F32)
            stage_ref[1] = in_ref[0, :, k_in].astype(F32)
            stage_ref[2] = in_ref[0, :, v_in].astype(F32)
            d1 = d // REGROUP_STRIDE if d > REGROUP_STRIDE else 1
            src_ref = stage_ref
            if d1 > 1:
                assert d1 <= REGROUP_STRIDE
                part = S // REGROUP_STRIDE
                for t in range(3):
                    for r0 in range(REGROUP_STRIDE):
                        stage2_ref[t, r0 * part:(r0 + 1) * part, :] = stage_ref[t, pl.ds(r0, part, stride=REGROUP_STRIDE), :]
                src_ref = stage2_ref

            def regroup(r, carry, gi=gi, d=d, d1=d1, L=L, seg=seg, half=half, src_ref=src_ref):
                if d1 > 1:
                    rows = pl.ds((r % REGROUP_STRIDE) * (S // REGROUP_STRIDE) + r // REGROUP_STRIDE, L, stride=d1)
                else:
                    rows = pl.ds(r, L, stride=d)
                k0 = r * seg
                qs_ref[gi, pl.ds(r * L, L), :] = src_ref[0, rows, :].astype(BF16)
                ks_ref[gi, pl.ds(k0, half), :] = zeros
                ks_ref[gi, pl.ds(k0 + half, L), :] = src_ref[1, rows, :].astype(BF16)
                ks_ref[gi, pl.ds(k0 + half + L, half), :] = zeros
                vs_ref[gi, pl.ds(k0, half), :LANES] = zeros
                vs_ref[gi, pl.ds(k0 + half, L), :LANES] = src_ref[2, rows, :].astype(BF16)
                vs_ref[gi, pl.ds(k0 + half + L, half), :LANES] = zeros
                return carry

            for r in range(d):
                regroup(r, 0)

    def key_row(gi, u):
        d, half, tk, nblk = geometry[gi]
        return (u + u // nblk) * tq

    def scores(gi, u, slot):
        d, half, tk, nblk = geometry[gi]
        assert 2 * half == tq
        i = u % nblk
        kind = (1 if i == 0 else 0) + (2 if i == nblk - 1 else 0)
        q_lanes = slice(3 * gi * LANES, (3 * gi + 1) * LANES)
        qp = in_ref[0, u * tq:(u + 1) * tq, q_lanes] if d == 1 else qs_ref[gi, u * tq:(u + 1) * tq, :]
        qp = qp * (ATTN_HEAD_DIM ** -0.5)
        none = jnp.zeros_like(qp)
        q2 = jnp.concatenate([jnp.where(first_head, qp, none), jnp.where(first_head, none, qp)], axis=0)
        k0 = key_row(gi, u)
        s = lax.dot_general(q2, ks_ref[gi, k0:k0 + tk, :], (((1,), (1,)), ((), ())), preferred_element_type=F32)
        s_ref[slot] = s + bias_ref[gi, kind]

    def maxima(slot):
        m_ref[slot] = jnp.broadcast_to(jnp.max(s_ref[slot], axis=-1, keepdims=True), (2 * tq, LANES))

    def values(gi, u, slot):
        d, half, tk, nblk = geometry[gi]
        r, i = u // nblk, u % nblk
        m = m_ref[slot]
        p = jnp.concatenate([jnp.exp(s_ref[slot, :, c0:c0 + LANES] - m) for c0 in range(0, tk, LANES)], axis=1)
        k0 = key_row(gi, u)
        pv = jnp.dot(p.astype(BF16), vs_ref[gi, k0:k0 + tk, :], preferred_element_type=F32)
        rows = pl.ds(u * tq, tq) if d == 1 else pl.ds(i * (tq * d) + r, tq, stride=d)
        acc_ref[gi, rows, :] = jnp.where(first_head, pv[:tq, :LANES], pv[tq:, :LANES])
        den_ref[gi, rows, :] = jnp.where(first_head, pv[:tq, LANES:], pv[tq:, LANES:])
        max_ref[gi, rows, :] = jnp.where(first_head, m[:tq], m[tq:])

    blocks = [(gi, j) for gi in range(n_groups) for j in range(n_tiles // ATTN_BLOCK)]
    for step in range(len(blocks) + 2):
        for stage, fn in enumerate((scores, maxima, values)):
            t = step - stage
            if 0 <= t < len(blocks):
                gi, j = blocks[t]
                for k in range(ATTN_BLOCK):
                    slot = (t % ATTN_RING) * ATTN_BLOCK + k
                    if fn is maxima:
                        fn(slot)
                    else:
                        fn(gi, j * ATTN_BLOCK + k, slot)

    def merge(j, carry):
        rows = pl.ds(pl.multiple_of(j * tq, tq), tq)
        top = functools.reduce(jnp.maximum, [max_ref[g, rows, :] for g in range(n_groups)])
        w = [jnp.exp(max_ref[g, rows, :] - top) for g in range(n_groups)]
        num = sum(w[g] * acc_ref[g, rows, :] for g in range(n_groups))
        den = sum(w[g] * den_ref[g, rows, :] for g in range(n_groups))
        o_ref[0, rows, :] = (num / den).astype(o_ref.dtype)
        return carry

    lax.fori_loop(0, n_tiles, merge, 0)


def _dilated_attention(proj, col0):
    B, S, _ = proj.shape
    W = ATTN_WIDTH
    n_groups = len(ATTN_PATTERNS)
    halves = {w // (2 * d) for w, d in ATTN_PATTERNS}
    assert len(halves) == 1 and all(S % (d * ATTN_TQ) == 0 for _, d in ATTN_PATTERNS)
    half = halves.pop()
    tk = ATTN_TQ + 2 * half
    assert tk % LANES == 0 and (S // ATTN_TQ) % ATTN_BLOCK == 0
    pairs = W // LANES
    slab = 3 * n_groups * LANES
    assert col0 % slab == 0
    return pl.pallas_call(
        functools.partial(_attn_kernel, S=S, patterns=ATTN_PATTERNS),
        grid=(pairs, B),
        in_specs=[pl.BlockSpec((1, S, slab), lambda hp, b: (b, 0, col0 // slab + hp))],
        out_specs=pl.BlockSpec((1, S, LANES), lambda hp, b: (b, 0, hp)),
        out_shape=jax.ShapeDtypeStruct((B, S, W), BF16),
        scratch_shapes=[pltpu.VMEM((3, S, LANES), F32),
                        pltpu.VMEM((3, S, LANES), F32),
                        pltpu.VMEM((n_groups, S, LANES), BF16),
                        pltpu.VMEM((n_groups, 2 * S, LANES), BF16),
                        pltpu.VMEM((n_groups, 2 * S, 2 * LANES), BF16),
                        pltpu.VMEM((n_groups, S, LANES), F32),
                        pltpu.VMEM((n_groups, S, LANES), F32),
                        pltpu.VMEM((n_groups, S, LANES), F32),
                        pltpu.VMEM((n_groups, 4, 2 * ATTN_TQ, tk), F32),
                        pltpu.VMEM((ATTN_RING * ATTN_BLOCK, 2 * ATTN_TQ, tk), F32),
                        pltpu.VMEM((ATTN_RING * ATTN_BLOCK, 2 * ATTN_TQ, LANES), F32)],
        compiler_params=_params("arbitrary", "arbitrary"),
        name="dilated_attention",
    )(proj)


def _log_sigmoid(x):
    return jnp.minimum(x, 0.0) - jnp.log1p(jnp.exp(-jnp.abs(x)))


def _retention_kernel(logit_ref, q_ref, k_ref, v_ref, gate_ref, o_ref, yf_ref, yb_ref, sf_ref, sb_ref):
    C = RET_CHUNK
    S, Dk = q_ref.shape[1], q_ref.shape[2]
    N = S // C
    h = pl.program_id(1)
    lf = _log_sigmoid(jnp.full((C, 1), logit_ref[0, h], F32))
    lb = _log_sigmoid(jnp.full((C, 1), logit_ref[1, h], F32))
    rel = (lax.broadcasted_iota(jnp.int32, (C, C), 0) - lax.broadcasted_iota(jnp.int32, (C, C), 1)).astype(F32)
    decay = jnp.exp(jnp.where(rel >= 0, lf * rel, -lb * rel))
    pos = lax.broadcasted_iota(jnp.int32, (C, 1), 0).astype(F32)
    xi_f = jnp.exp(lf * (pos + 1.0))
    zeta_f = jnp.exp(lf * (C - 1.0 - pos))
    xi_b = jnp.exp(lb * (C - pos))
    zeta_b = jnp.exp(lb * pos)
    chunk_f = jnp.exp(_log_sigmoid(jnp.full((Dk, 1), logit_ref[0, h], F32)) * C)
    chunk_b = jnp.exp(_log_sigmoid(jnp.full((Dk, 1), logit_ref[1, h], F32)) * C)
    scale = Dk ** -0.5

    def load(i):
        r0 = pl.multiple_of(i * C, C)
        return r0, q_ref[0, pl.ds(r0, C), :] * scale, k_ref[0, pl.ds(r0, C), :], v_ref[0, pl.ds(r0, C), :]

    def cross(state_ref, qc, kc, vc, xi, zeta, chunk_decay):
        st = state_ref[...]
        y = jnp.dot(qc, st.astype(BF16), preferred_element_type=F32) * xi
        kz = (kc.astype(F32) * zeta).astype(BF16)
        u = lax.dot_general(kz, vc, (((0,), (0,)), ((), ())), preferred_element_type=F32)
        state_ref[...] = st * chunk_decay + u
        return y

    sf_ref[...] = jnp.zeros_like(sf_ref)
    sb_ref[...] = jnp.zeros_like(sb_ref)

    def finish(rows, r):
        r = r * lax.rsqrt(jnp.mean(r * r, axis=-1, keepdims=True) + EPS)
        g = gate_ref[0, rows, :].astype(F32)
        o_ref[0, rows, :] = (g * _sigmoid(g) * r).astype(o_ref.dtype)

    def both(i):
        r0, qc, kc, vc = load(i)
        s = lax.dot_general(qc, kc, (((1,), (1,)), ((), ())), preferred_element_type=F32) * decay
        yf = jnp.dot(s.astype(BF16), vc, preferred_element_type=F32) + cross(sf_ref, qc, kc, vc, xi_f, zeta_f, chunk_f)
        r1, qd, kd, vd = load(N - 1 - i)
        yb = cross(sb_ref, qd, kd, vd, xi_b, zeta_b, chunk_b)
        return pl.ds(r0, C), yf, pl.ds(r1, C), yb

    def first_half(i, carry):
        rows_f, yf, rows_b, yb = both(i)
        yf_ref[rows_f, :] = yf
        yb_ref[rows_b, :] = yb
        return carry

    def second_half(i, carry):
        rows_f, yf, rows_b, yb = both(i)
        finish(rows_f, yf + yb_ref[rows_f, :])
        finish(rows_b, yf_ref[rows_b, :] + yb)
        return carry

    assert N % 2 == 0
    lax.fori_loop(0, N // 2, first_half, 0, unroll=4)
    lax.fori_loop(N // 2, N, second_half, 0, unroll=4)


def _retention(proj, decay_logit, q_col0, gate_col0, width):
    B, S, _ = proj.shape
    Dk = width // RET_HEADS
    qb = q_col0 // Dk
    gb = gate_col0 // Dk
    nb = RET_HEADS
    return pl.pallas_call(
        _retention_kernel,
        grid=(B, RET_HEADS),
        in_specs=[pl.BlockSpec(memory_space=pltpu.SMEM),
                  pl.BlockSpec((1, S, Dk), lambda b, h: (b, 0, qb + h)),
                  pl.BlockSpec((1, S, Dk), lambda b, h: (b, 0, qb + nb + h)),
                  pl.BlockSpec((1, S, Dk), lambda b, h: (b, 0, qb + 2 * nb + h)),
                  pl.BlockSpec((1, S, Dk), lambda b, h: (b, 0, gb + h))],
        out_specs=pl.BlockSpec((1, S, Dk), lambda b, h: (b, 0, h)),
        out_shape=jax.ShapeDtypeStruct((B, S, RET_HEADS * Dk), BF16),
        scratch_shapes=[pltpu.VMEM((S, Dk), F32), pltpu.VMEM((S, Dk), F32),
                        pltpu.VMEM((Dk, Dk), F32), pltpu.VMEM((Dk, Dk), F32)],
        compiler_params=_params("parallel", "parallel"),
        name="retention",
    )(decay_logit, proj, proj, proj, proj)


MERGE_ROW_BLOCKS = 4


def _merge_kernel(attn_ref, ret_ref, ga0_ref, ga1_ref, gr0_ref, gr1_ref, x_ref, wa_ref, wr_ref, wo_ref, gain_ref,
                  wrt_ref, wrt_lo_ref, xo_ref, h_ref, logit_ref):
    tm = x_ref.shape[0]
    rows = tm // MERGE_ROW_BLOCKS
    for r in range(MERGE_ROW_BLOCKS):
        rs = slice(r * rows, (r + 1) * rows)
        pa = jnp.dot(attn_ref[rs, :], wa_ref[...], preferred_element_type=F32)
        pr = jnp.dot(ret_ref[rs, :], wr_ref[...], preferred_element_type=F32)
        ga = jnp.concatenate([ga0_ref[rs, :], ga1_ref[rs, :]], axis=1).astype(F32)
        gr = jnp.concatenate([gr0_ref[rs, :], gr1_ref[rs, :]], axis=1).astype(F32)
        merged = pa * _sigmoid(ga) + pr * _sigmoid(gr)
        x = x_ref[rs, :] + jnp.dot(merged.astype(BF16), wo_ref[...], preferred_element_type=F32)
        xo_ref[rs, :] = x
        h = _rms(x, gain_ref[...])
        h_hi = h.astype(BF16)
        h_ref[rs, :] = h_hi
        h_lo = (h - h_hi.astype(F32)).astype(BF16)
        logit_ref[rs, :] = (jnp.dot(h_hi, wrt_ref[...], preferred_element_type=F32)
                            + jnp.dot(h_lo, wrt_ref[...], preferred_element_type=F32)
                            + jnp.dot(h_hi, wrt_lo_ref[...], preferred_element_type=F32))


def _merge(attn, ret, gates, gate_col0, x2d, wa, wr, wo, gain, w_router, tm):
    T, D = x2d.shape
    gw = D // 2
    assert gate_col0 % gw == 0
    g0 = gate_col0 // gw
    w_router_pad = jnp.pad(w_router, ((0, 0), (0, LANES - w_router.shape[1])))
    wrt_hi = w_router_pad.astype(BF16)
    wrt_lo = (w_router_pad - wrt_hi.astype(F32)).astype(BF16)
    tok = lambda width, col=0: pl.BlockSpec((tm, width), lambda i: (i, col))
    full = lambda a: pl.BlockSpec(a.shape, lambda i: (0, 0))
    return pl.pallas_call(
        _merge_kernel,
        grid=(T // tm,),
        in_specs=[tok(ATTN_WIDTH), tok(D), tok(gw, g0), tok(gw, g0 + 1), tok(gw, g0 + 2), tok(gw, g0 + 3), tok(D),
                  full(wa), full(wr), full(wo), pl.BlockSpec((1, D), lambda i: (0, 0)), full(wrt_hi), full(wrt_lo)],
        out_specs=[tok(D), tok(D), tok(LANES)],
        out_shape=[jax.ShapeDtypeStruct((T, D), F32), jax.ShapeDtypeStruct((T, D), BF16),
                   jax.ShapeDtypeStruct((T, LANES), F32)],
        compiler_params=_params("parallel"),
        name="merge_out_proj",
    )(attn, ret, gates, gates, gates, gates, x2d, wa, wr, wo, gain.reshape(1, D), wrt_hi, wrt_lo)


def _route_kernel(logit_ref, pos_ref, pos_t_ref, aff_ref, before_ref, *, n_experts, cap):
    S = logit_ref.shape[1]
    rows = 256

    @pl.when(pl.program_id(0) == 0)
    def _():
        def fill(i, carry):
            r0 = pl.multiple_of(i * rows, rows)
            r = lax.broadcasted_iota(jnp.int32, (rows, S), 0) + r0
            c = lax.broadcasted_iota(jnp.int32, (rows, S), 1)
            before_ref[pl.ds(r0, rows), :] = jnp.where(r < c, 1.0, 0.0).astype(BF16)
            return carry
        lax.fori_loop(0, S // rows, fill, 0)

    lane = lax.broadcasted_iota(jnp.int32, (1, LANES), 1)
    lg = jnp.where(lane < n_experts, logit_ref[0], MASKED)
    ex = jnp.exp(lg - jnp.max(lg, axis=-1, keepdims=True))
    aff = (ex / jnp.sum(ex, axis=-1, keepdims=True)).T[:n_experts]

    def count_at_least(bits):
        return jnp.sum((aff >= lax.bitcast_convert_type(bits, F32)).astype(jnp.int32), axis=-1, keepdims=True)

    thr_bits = jnp.zeros((n_experts, 1), jnp.int32)
    for lo in range(29, 0, -2):
        for digit in (1, 2, 3):
            cand = thr_bits | (digit << lo)
            best = cand if digit == 1 else jnp.where(count_at_least(cand) >= cap, cand, best)
            if digit == 1:
                first_ok = count_at_least(cand) >= cap
        thr_bits = jnp.where(first_ok, best, thr_bits)
    thr_bits = jnp.where(count_at_least(thr_bits | 1) >= cap, thr_bits | 1, thr_bits)
    floor = lax.bitcast_convert_type(thr_bits, F32)
    thr = jnp.min(jnp.where(aff >= floor, aff, jnp.inf), axis=-1, keepdims=True)
    above = aff > thr
    tied = aff == thr
    need = cap - jnp.sum(above.astype(jnp.int32), axis=-1, keepdims=True)
    count_before = lambda picked: jnp.dot(picked.astype(BF16), before_ref[...], preferred_element_type=F32)
    tied_f = jnp.where(tied, 1.0, 0.0)
    all_tied_fit = jnp.all(jnp.sum(tied.astype(jnp.int32), axis=-1, keepdims=True) == need)
    tied_taken = lax.cond(all_tied_fit, lambda: tied_f,
                          lambda: jnp.where(count_before(tied_f) < need.astype(F32), tied_f, 0.0))
    chosen_f = jnp.where(above, 1.0, tied_taken)
    pos = jnp.where(chosen_f > 0.0, count_before(chosen_f), -1.0)
    pos_ref[0] = pos
    pos_t_ref[0] = jnp.concatenate([pos, jnp.full((LANES - n_experts, S), -1.0, F32)], axis=0).T
    aff_ref[0] = aff


def _route(logits, cap):
    B, S, _ = logits.shape
    E = N_EXPERTS
    return pl.pallas_call(
        functools.partial(_route_kernel, n_experts=E, cap=cap),
        grid=(B,),
        in_specs=[pl.BlockSpec((1, S, LANES), lambda b: (b, 0, 0))],
        out_specs=[pl.BlockSpec((1, E, S), lambda b: (b, 0, 0)), pl.BlockSpec((1, S, LANES), lambda b: (b, 0, 0)),
                   pl.BlockSpec((1, E, S), lambda b: (b, 0, 0))],
        out_shape=[jax.ShapeDtypeStruct((B, E, S), F32), jax.ShapeDtypeStruct((B, S, LANES), F32),
                   jax.ShapeDtypeStruct((B, E, S), F32)],
        scratch_shapes=[pltpu.VMEM((S, S), BF16)],
        compiler_params=_params("arbitrary"),
        name="route",
    )(logits)


def _gather_kernel(h_ref, pos_ref, aff_ref, x_ref, g_ref, *, n_experts, cap):
    S = h_ref.shape[1]
    slot = lax.broadcasted_iota(jnp.int32, (cap, S), 0)
    for e in range(n_experts):
        mask = slot == pos_ref[0, e:e + 1, :].astype(jnp.int32)
        onehot = jnp.where(mask, 1.0, 0.0).astype(BF16)
        x_ref[e, 0] = jnp.dot(onehot, h_ref[0], preferred_element_type=F32).astype(x_ref.dtype)
        g = jnp.sum(jnp.where(mask, aff_ref[0, e:e + 1, :], 0.0), axis=-1, keepdims=True)
        g_ref[e, 0] = jnp.broadcast_to(g, (cap, LANES))


def _gather(h, pos, aff, cap):
    B, S, D = h.shape
    E = pos.shape[1]
    rows = pl.BlockSpec((1, E, S), lambda b: (b, 0, 0))
    return pl.pallas_call(
        functools.partial(_gather_kernel, n_experts=E, cap=cap),
        grid=(B,),
        in_specs=[pl.BlockSpec((1, S, D), lambda b: (b, 0, 0)), rows, rows],
        out_specs=[pl.BlockSpec((E, 1, cap, D), lambda b: (0, b, 0, 0)),
                   pl.BlockSpec((E, 1, cap, LANES), lambda b: (0, b, 0, 0))],
        out_shape=[jax.ShapeDtypeStruct((E, B, cap, D), BF16), jax.ShapeDtypeStruct((E, B, cap, LANES), F32)],
        compiler_params=_params("parallel"),
        name="expert_gather",
    )(h, pos, aff)


EXPERT_ROW_BLOCKS = 4


def _expert_kernel(x_ref, g_ref, wg_ref, wu_ref, wd_ref, y_ref, act_ref, wd_bf_ref, chunk_ref, *, nf, tf):
    f = pl.program_id(1)
    wg = wg_ref[...].astype(BF16)
    wu = wu_ref[...].astype(BF16)
    wd_bf_ref[pl.ds(pl.multiple_of(f * tf, tf), tf), :] = wd_ref[...].astype(BF16)
    rows = x_ref.shape[1] // EXPERT_ROW_BLOCKS
    for r in range(EXPERT_ROW_BLOCKS):
        x = x_ref[0, r * rows:(r + 1) * rows, :]
        a = jnp.dot(x, wg, preferred_element_type=F32)
        u = jnp.dot(x, wu, preferred_element_type=F32)
        chunk_ref[r * rows:(r + 1) * rows, :] = (a * _sigmoid(a) * u).astype(BF16)
    for k in range(nf):
        @pl.when(f == k)
        def _(k=k):
            act_ref[:, k * tf:(k + 1) * tf] = chunk_ref[...]

    @pl.when(f == nf - 1)
    def _():
        y = jnp.dot(act_ref[...], wd_bf_ref[...], preferred_element_type=F32)
        y_ref[0] = (y * g_ref[0][:, :1]).astype(y_ref.dtype)


def _experts(xin, g, w_gate, w_up, w_down, layer, tf):
    E, M, D = xin.shape
    FF = w_gate.shape[3]
    return pl.pallas_call(
        functools.partial(_expert_kernel, nf=FF // tf, tf=tf),
        grid=(E, FF // tf),
        in_specs=[pl.BlockSpec((1, M, D), lambda e, f: (e, 0, 0)),
                  pl.BlockSpec((1, M, LANES), lambda e, f: (e, 0, 0)),
                  pl.BlockSpec((None, None, D, tf), lambda e, f: (layer, e, 0, f)),
                  pl.BlockSpec((None, None, D, tf), lambda e, f: (layer, e, 0, f)),
                  pl.BlockSpec((None, None, tf, D), lambda e, f: (layer, e, f, 0))],
        out_specs=pl.BlockSpec((1, M, D), lambda e, f: (e, 0, 0)),
        out_shape=jax.ShapeDtypeStruct((E, M, D), BF16),
        scratch_shapes=[pltpu.VMEM((M, FF), BF16), pltpu.VMEM((FF, D), BF16), pltpu.VMEM((M, tf), BF16)],
        compiler_params=_params("parallel", "arbitrary"),
        name="expert_swiglu",
    )(xin, g, w_gate, w_up, w_down)


def _scatter_kernel(x_ref, pos_t_ref, y_ref, gain_ref, o_ref, *, n_experts, cap, normalize):
    pos = pos_t_ref[0].astype(jnp.int32)
    slot = lax.broadcasted_iota(jnp.int32, (pos.shape[0], cap), 1)
    onehot = jnp.concatenate([jnp.where(pos[:, e:e + 1] == slot, 1.0, 0.0).astype(BF16) for e in range(n_experts)],
                             axis=1)
    y = y_ref[:, 0].reshape(n_experts * cap, y_ref.shape[-1])
    out = x_ref[0] + jnp.dot(onehot, y, preferred_element_type=F32)
    o_ref[0] = _rms(out, gain_ref[...]) if normalize else out


def _scatter(x, pos_t, y, cap, ts, gain, normalize):
    B, S, D = x.shape
    E = y.shape[0]
    return pl.pallas_call(
        functools.partial(_scatter_kernel, n_experts=E, cap=cap, normalize=normalize),
        grid=(B, S // ts),
        in_specs=[pl.BlockSpec((1, ts, D), lambda b, t: (b, t, 0)),
                  pl.BlockSpec((1, ts, LANES), lambda b, t: (b, t, 0)),
                  pl.BlockSpec((E, 1, cap, D), lambda b, t: (0, b, 0, 0)),
                  pl.BlockSpec((1, D), lambda b, t: (0, 0))],
        out_specs=pl.BlockSpec((1, ts, D), lambda b, t: (b, t, 0)),
        out_shape=jax.ShapeDtypeStruct((B, S, D), F32),
        compiler_params=_params("parallel", "arbitrary"),
        name="expert_scatter",
    )(x, pos_t, y, gain.reshape(1, D))


def kernel(x, w_in, w_attn_out, w_ret_out, w_out, ret_decay_logit, norm_mix, norm_ffn, w_router, w_gate, w_up,
           w_down, norm_final):
    B, S, D = x.shape
    T = B * S
    depth = w_in.shape[0]
    assert depth >= 1
    W = ATTN_WIDTH
    attn_in = len(ATTN_PATTERNS) * 3 * W
    n_in = w_in.shape[2]
    ret0, swish0, gates0 = attn_in, attn_in + 3 * D, attn_in + 4 * D
    cap = CAPACITY_FACTOR * S // N_EXPERTS
    x2d = x.reshape(T, D)
    n_groups, pairs = len(ATTN_PATTERNS), W // LANES
    w_attn = w_in[:, :, :attn_in].reshape(depth, D, n_groups, 3, pairs, LANES).transpose(0, 1, 4, 2, 3, 5)
    w_in_b = jnp.concatenate([w_attn.reshape(depth, D, attn_in), w_in[:, :, attn_in:]], axis=2).astype(BF16)
    for layer in range(depth):
        proj = _norm_proj(x2d, norm_mix[layer], w_in_b, layer, 0, n_in, BF16, tm=2048, tn=1536)
        proj3 = proj.reshape(B, S, n_in)

        attn = _dilated_attention(proj3, 0)
        ret = _retention(proj3, ret_decay_logit[layer], ret0, swish0, D)

        x2d, h2, logits = _merge(attn.reshape(T, W), ret.reshape(T, D), proj, gates0, x2d,
                                 w_attn_out[layer].astype(BF16),
                                 w_ret_out[layer].astype(BF16), w_out[layer].astype(BF16), norm_ffn[layer],
                                 w_router[layer], tm=1024)

        pos, pos_t, aff = _route(logits.reshape(B, S, LANES), cap)
        xin, g = _gather(h2.reshape(B, S, D), pos, aff, cap)
        y = _experts(xin.reshape(N_EXPERTS, B * cap, D), g.reshape(N_EXPERTS, B * cap, LANES),
                     w_gate, w_up, w_down, layer, tf=256)
        x2d = _scatter(x2d.reshape(B, S, D), pos_t, y.reshape(N_EXPERTS, B, cap, D), cap, 1024, norm_final,
                       normalize=layer == depth - 1).reshape(T, D)
    return x2d.reshape(B, S, D)
```
